```python
import jax, jax.numpy as jnp
from jax import lax
import numpy as np

D_MODEL = 2048
BATCH = 2
SEQ = 4096
DEPTH = 4
DEC_BATCH = 32
DEC_SEQ = 4
PAST_LEN = 16384
PAGE_SIZE = 128

N_MIXERS = 2
N_ATTN_LAYERS = (DEPTH + 1) // 2
N_CONV_LAYERS = DEPTH // 2
HEAD_DIM = 64
N_Q_HEADS = D_MODEL // HEAD_DIM
N_KV_HEADS = 8
GQA_GROUP = N_Q_HEADS // N_KV_HEADS
WINDOW = 128
ROPE_THETA = 10000.0
CONV_WIDTH = 3
CONV_DIM = D_MODEL
PEER_HEADS = 8
PEER_KEYS = 128
PEER_EXPERTS = PEER_KEYS * PEER_KEYS
PEER_QDIM = 256
PEER_HALF = PEER_QDIM // 2
PEER_TOPK = 16
PEER_CHUNK = 128
LN_EPS = 1e-5
NEG_INF = -1e30
DEEPNORM_ALPHA = (2 * DEPTH) ** 0.25
DEEPNORM_BETA = (8 * DEPTH) ** -0.25

kernel_name = "hybrid_swa_sink_shortconv_peer_deepnorm_step"


def layer_norm(x, g, b):
    xf = x.astype(jnp.float32)
    mu = jnp.mean(xf, axis=-1, keepdims=True)
    xc = xf - mu
    var = jnp.mean(xc * xc, axis=-1, keepdims=True)
    out = xc * lax.rsqrt(var + LN_EPS) * g.astype(jnp.float32) + b.astype(jnp.float32)
    return out.astype(x.dtype)


def rope(x, pos):
    half = HEAD_DIM // 2
    inv = jnp.power(ROPE_THETA, -jnp.arange(half, dtype=jnp.float32) * 2.0 / HEAD_DIM)
    ang = pos.astype(jnp.float32)[:, None] * inv[None, :]
    cos = jnp.cos(ang)[:, None, :]
    sin = jnp.sin(ang)[:, None, :]
    xf = x.astype(jnp.float32)
    x1, x2 = xf[..., :half], xf[..., half:]
    out = jnp.concatenate([x1 * cos - x2 * sin, x2 * cos + x1 * sin], axis=-1)
    return out.astype(x.dtype)


def attn_project(x, w_qkv, b_qkv, pos):
    b, t, _ = x.shape
    qkv = x @ w_qkv + b_qkv
    q, k, v = jnp.split(qkv, [N_Q_HEADS * HEAD_DIM, (N_Q_HEADS + N_KV_HEADS) * HEAD_DIM], axis=-1)
    q = rope(q.reshape(b, t, N_Q_HEADS, HEAD_DIM), pos).reshape(b, t, N_KV_HEADS, GQA_GROUP, HEAD_DIM)
    k = rope(k.reshape(b, t, N_KV_HEADS, HEAD_DIM), pos)
    v = v.reshape(b, t, N_KV_HEADS, HEAD_DIM)
    return q, k, v


def sink_attention(q, k, v, mask, sink):
    s = jnp.einsum('...qkgd,...skd->...kgqs', q, k, preferred_element_type=jnp.float32) * (HEAD_DIM ** -0.5)
    s = jnp.where(mask[..., None, None, :, :], s, NEG_INF)
    sink_l = sink.astype(jnp.float32)[:, :, None, None]
    m = jnp.maximum(jnp.max(s, axis=-1, keepdims=True), sink_l)
    p = jnp.exp(s - m)
    denom = jnp.sum(p, axis=-1, keepdims=True) + jnp.exp(sink_l - m)
    w = (p / denom).astype(v.dtype)
    return jnp.einsum('...kgqs,...skd->...qkgd', w, v)


def swa_prompt(x, w_qkv, b_qkv, w_o, b_o, sink):
    b, t, _ = x.shape
    q, k, v = attn_project(x, w_qkv, b_qkv, jnp.arange(t))
    nb = t // WINDOW
    qb = q.reshape(b, nb, WINDOW, N_KV_HEADS, GQA_GROUP, HEAD_DIM)

    def band(z):
        zb = z.reshape(b, nb, WINDOW, N_KV_HEADS, HEAD_DIM)
        prev = jnp.pad(zb, ((0, 0), (1, 0), (0, 0), (0, 0), (0, 0)))[:, :-1]
        return jnp.concatenate([prev, zb], axis=2)

    i = jnp.arange(WINDOW)[None, :, None]
    j = jnp.arange(2 * WINDOW)[None, None, :]
    blk = jnp.arange(nb)[:, None, None]
    dist = i + WINDOW - j
    kpos = blk * WINDOW - WINDOW + j
    mask = (dist >= 0) & (dist <= WINDOW) & (kpos >= 0)
    o = sink_attention(qb, band(k), band(v), mask, sink)
    y = o.reshape(b, t, N_Q_HEADS * HEAD_DIM) @ w_o + b_o
    return y, k[:, -WINDOW:], v[:, -WINDOW:]


def swa_sample(x, ck, cv, w_qkv, b_qkv, w_o, b_o, sink):
    b, t, _ = x.shape
    q, k, v = attn_project(x, w_qkv, b_qkv, PAST_LEN + jnp.arange(t))
    kk = jnp.concatenate([ck, k], axis=1)
    vv = jnp.concatenate([cv, v], axis=1)
    i = jnp.arange(t)[:, None]
    j = jnp.arange(WINDOW + t)[None, :]
    dist = i + WINDOW - j
    mask = (dist >= 0) & (dist <= WINDOW)
    o = sink_attention(q, kk, vv, mask, sink)
    y = o.reshape(b, t, N_Q_HEADS * HEAD_DIM) @ w_o + b_o
    return y, kk[:, -WINDOW:], vv[:, -WINDOW:]


def short_conv(x, state, w_in, conv_w, w_out):
    t = x.shape[1]
    gate_b, gate_c, h = jnp.split(x @ w_in, 3, axis=-1)
    u = gate_c * h
    up = jnp.concatenate([state, u], axis=1)
    conv = up[:, 0:t] * conv_w[0]
    for tap in range(1, CONV_WIDTH):
        conv = conv + up[:, tap:tap + t] * conv_w[tap]
    y = (gate_b * conv) @ w_out
    return y, up[:, -(CONV_WIDTH - 1):]


def peer(x, w_q, sub_keys, u, v):
    shp = x.shape
    xf = x.reshape(-1, D_MODEL)
    n = xf.shape[0]
    n_chunks = -(-n // PEER_CHUNK)
    xp = jnp.pad(xf, ((0, n_chunks * PEER_CHUNK - n), (0, 0))).reshape(n_chunks, PEER_CHUNK, D_MODEL)

    def chunk(xc):
        q = (xc @ w_q).reshape(PEER_CHUNK, PEER_HEADS, 2, PEER_HALF)
        s = jnp.einsum('chpd,hpnd->chpn', q, sub_keys, preferred_element_type=jnp.float32)
        sv, si = lax.top_k(s, PEER_TOPK)
        cand = (sv[:, :, 0, :, None] + sv[:, :, 1, None, :]).reshape(PEER_CHUNK, PEER_HEADS, -1)
        cidx = (si[:, :, 0, :, None] * PEER_KEYS + si[:, :, 1, None, :]).reshape(PEER_CHUNK, PEER_HEADS, -1)
        fv, fp = lax.top_k(cand, PEER_TOPK)
        eidx = jnp.take_along_axis(cidx, fp, axis=-1)
        g = jax.nn.softmax(fv, axis=-1)
        hid = jnp.einsum('chkd,cd->chk', u[eidx], xc, preferred_element_type=jnp.float32)
        a = (jax.nn.gelu(hid, approximate=False) * g).astype(xc.dtype)
        return jnp.einsum('chk,chkd->cd', a, v[eidx])

    out = lax.map(chunk, xp).reshape(-1, D_MODEL)[:n]
    return out.reshape(shp)


def setup_inputs(seed: int = 0) -> dict:
    key = jax.random.key(seed)
    ks = jax.random.split(key, 20)
    f32 = jnp.float32
    nrm = lambda k, shape, scale: jax.random.normal(k, shape, f32) * scale
    qkv_cols = (N_Q_HEADS + 2 * N_KV_HEADS) * HEAD_DIM
    col_scale = jnp.concatenate([
        jnp.ones(((N_Q_HEADS + N_KV_HEADS) * HEAD_DIM,), f32),
        jnp.full((N_KV_HEADS * HEAD_DIM,), DEEPNORM_BETA, f32)])
    return {
        "x_prompt": nrm(ks[0], (BATCH, SEQ, D_MODEL), 1.0),
        "x_sample": nrm(ks[1], (DEC_BATCH, DEC_SEQ, D_MODEL), 1.0),
        "cache_k": nrm(ks[2], (N_ATTN_LAYERS, DEC_BATCH, WINDOW, N_KV_HEADS, HEAD_DIM), 1.0),
        "cache_v": nrm(ks[3], (N_ATTN_LAYERS, DEC_BATCH, WINDOW, N_KV_HEADS, HEAD_DIM), 1.0),
        "state_conv": nrm(ks[4], (N_CONV_LAYERS, DEC_BATCH, CONV_WIDTH - 1, CONV_DIM), 1.0),
        "w_qkv": nrm(ks[5], (N_ATTN_LAYERS, D_MODEL, qkv_cols), D_MODEL ** -0.5) * col_scale,
        "b_qkv": nrm(ks[6], (N_ATTN_LAYERS, qkv_cols), 0.02),
        "w_o": nrm(ks[7], (N_ATTN_LAYERS, N_Q_HEADS * HEAD_DIM, D_MODEL), DEEPNORM_BETA * (N_Q_HEADS * HEAD_DIM) ** -0.5),
        "b_o": nrm(ks[8], (N_ATTN_LAYERS, D_MODEL), 0.02),
        "attn_sinks": nrm(ks[9], (N_ATTN_LAYERS, N_KV_HEADS, GQA_GROUP), 1.0),
        "w_conv_in": nrm(ks[10], (N_CONV_LAYERS, D_MODEL, 3 * CONV_DIM), D_MODEL ** -0.5),
        "conv_w": nrm(ks[11], (N_CONV_LAYERS, CONV_WIDTH, CONV_DIM), CONV_WIDTH ** -0.5),
        "w_conv_out": nrm(ks[12], (N_CONV_LAYERS, CONV_DIM, D_MODEL), DEEPNORM_BETA * CONV_DIM ** -0.5),
        "w_peer_q": nrm(ks[13], (DEPTH, D_MODEL, PEER_HEADS * PEER_QDIM), D_MODEL ** -0.5),
        "peer_sub_keys": nrm(ks[14], (DEPTH, PEER_HEADS, 2, PEER_KEYS, PEER_HALF), PEER_HALF ** -0.5),
        "peer_u": nrm(ks[15], (DEPTH, PEER_EXPERTS, D_MODEL), D_MODEL ** -0.5),
        "peer_v": nrm(ks[16], (DEPTH, PEER_EXPERTS, D_MODEL), DEEPNORM_BETA * PEER_HEADS ** -0.5),
        "ln_g": 1.0 + nrm(ks[17], (DEPTH, 2, D_MODEL), 0.02),
        "ln_b": nrm(ks[18], (DEPTH, 2, D_MODEL), 0.02),
    }


def reference(x_prompt, x_sample, cache_k, cache_v, state_conv, w_qkv, b_qkv, w_o, b_o,
              attn_sinks, w_conv_in, conv_w, w_conv_out, w_peer_q, peer_sub_keys,
              peer_u, peer_v, ln_g, ln_b):
    yp, ys = x_prompt, x_sample
    new_kp, new_vp, new_cp, new_ks, new_vs, new_cs = [], [], [], [], [], []
    zero_conv = jnp.zeros((x_prompt.shape[0], CONV_WIDTH - 1, CONV_DIM), x_prompt.dtype)
    for layer in range(DEPTH):
        j = layer // N_MIXERS
        if layer % N_MIXERS == 0:
            mp, kp, vp = swa_prompt(yp, w_qkv[j], b_qkv[j], w_o[j], b_o[j], attn_sinks[j])
            ms, k_s, v_s = swa_sample(ys, cache_k[j], cache_v[j], w_qkv[j], b_qkv[j], w_o[j], b_o[j], attn_sinks[j])
            new_kp.append(kp)
            new_vp.append(vp)
            new_ks.append(k_s)
            new_vs.append(v_s)
        else:
            mp, cp = short_conv(yp, zero_conv, w_conv_in[j], conv_w[j], w_conv_out[j])
            ms, cs = short_conv(ys, state_conv[j], w_conv_in[j], conv_w[j], w_conv_out[j])
            new_cp.append(cp)
            new_cs.append(cs)
        yp = layer_norm(DEEPNORM_ALPHA * yp + mp, ln_g[layer, 0], ln_b[layer, 0])
        ys = layer_norm(DEEPNORM_ALPHA * ys + ms, ln_g[layer, 0], ln_b[layer, 0])
        fp = peer(yp, w_peer_q[layer], peer_sub_keys[layer], peer_u[layer], peer_v[layer])
        fs = peer(ys, w_peer_q[layer], peer_sub_keys[layer], peer_u[layer], peer_v[layer])
        yp = layer_norm(DEEPNORM_ALPHA * yp + fp, ln_g[layer, 1], ln_b[layer, 1])
        ys = layer_norm(DEEPNORM_ALPHA * ys + fs, ln_g[layer, 1], ln_b[layer, 1])
    return (yp, ys, jnp.stack(new_kp), jnp.stack(new_vp), jnp.stack(new_cp),
            jnp.stack(new_ks), jnp.stack(new_vs), jnp.stack(new_cs))
```

```python
import functools

import jax
import jax.numpy as jnp
from jax import lax
from jax.experimental import pallas as pl
from jax.experimental.pallas import tpu as pltpu

F32 = jnp.float32
BF16 = jnp.bfloat16

D_MODEL = 2048
DEPTH = 4
HEAD_DIM = 64
N_Q_HEADS = 32
N_KV_HEADS = 8
GQA_GROUP = 4
WINDOW = 128
ROPE_THETA = 10000.0
PAST_LEN = 16384
CONV_WIDTH = 3
PEER_HEADS = 8
PEER_KEYS = 128
PEER_EXPERTS = PEER_KEYS * PEER_KEYS
PEER_HALF = 128
PEER_TOPK = 16
LN_EPS = 1e-5
NEG_INF = -1e30
DEEPNORM_ALPHA = (2 * DEPTH) ** 0.25

Q_COLS = N_Q_HEADS * HEAD_DIM
KV_COLS = N_KV_HEADS * HEAD_DIM
QKV_COLS = Q_COLS + 2 * KV_COLS

LANES = 128
VMEM_LIMIT_BYTES = 56 * 1024 * 1024


def _cparams(*semantics):
    return pltpu.CompilerParams(dimension_semantics=semantics, vmem_limit_bytes=VMEM_LIMIT_BYTES)


def _layer_norm(z, g, b):
    mu = jnp.mean(z, axis=-1, keepdims=True)
    zc = z - mu
    var = jnp.mean(zc * zc, axis=-1, keepdims=True)
    return zc * lax.rsqrt(var + LN_EPS) * g + b


QKV_TN = 512
N_ROPE_TILES = (Q_COLS + KV_COLS) // QKV_TN


def _qkv_kernel(x_ref, w_ref, b_ref, cos_ref, sin_ref, o_ref):
    j = pl.program_id(1)
    acc = jnp.dot(x_ref[...].astype(BF16), w_ref[...], preferred_element_type=F32) + b_ref[...]

    @pl.when(j < N_ROPE_TILES)
    def _():
        cos = cos_ref[...]
        sin = sin_ref[...]
        lane = lax.broadcasted_iota(jnp.int32, cos.shape, 1)
        first_half = (lane % HEAD_DIM) < (HEAD_DIM // 2)
        for c in range(QKV_TN // LANES):
            seg = acc[:, c * LANES:(c + 1) * LANES]
            partner = jnp.where(first_half,
                                pltpu.roll(seg, LANES - HEAD_DIM // 2, 1),
                                pltpu.roll(seg, HEAD_DIM // 2, 1))
            o_ref[:, c * LANES:(c + 1) * LANES] = seg * cos + partner * sin

    @pl.when(j >= N_ROPE_TILES)
    def _():
        o_ref[...] = acc


def _qkv_rope(x, w_bf, b, cos, sin, tm):
    n = x.shape[0]
    return pl.pallas_call(
        _qkv_kernel,
        grid=(n // tm, QKV_COLS // QKV_TN),
        in_specs=[
            pl.BlockSpec((tm, D_MODEL), lambda i, j: (i, 0)),
            pl.BlockSpec((D_MODEL, QKV_TN), lambda i, j: (0, j)),
            pl.BlockSpec((1, QKV_TN), lambda i, j: (0, j)),
            pl.BlockSpec((tm, LANES), lambda i, j: (i, 0)),
            pl.BlockSpec((tm, LANES), lambda i, j: (i, 0)),
        ],
        out_specs=pl.BlockSpec((tm, QKV_TN), lambda i, j: (i, j)),
        out_shape=jax.ShapeDtypeStruct((n, QKV_COLS), F32),
        compiler_params=_cparams("parallel", "arbitrary"),
        name="qkv_rope",
    )(x, w_bf, b.reshape(1, QKV_COLS), cos, sin)


def _sink_column(sink_ref, g, rows):
    return jnp.concatenate(
        [jnp.full((rows, 1), sink_ref[g * GQA_GROUP + j], F32) for j in range(GQA_GROUP)], axis=0)


def _attn_prompt_kernel(sink_ref, q_ref, kp_ref, kc_ref, vp_ref, vc_ref, o_ref):
    i = pl.program_id(1)
    rows = GQA_GROUP * WINDOW
    qi = lax.broadcasted_iota(jnp.int32, (rows, 2 * WINDOW), 0) % WINDOW
    kj = lax.broadcasted_iota(jnp.int32, (rows, 2 * WINDOW), 1)
    dist = qi + WINDOW - kj
    prev_ok = jnp.where(i > 0, 0, WINDOW)
    mask = (dist >= 0) & (dist <= WINDOW) & (kj >= prev_ok)
    for g in range(N_KV_HEADS):
        ks = slice(g * HEAD_DIM, (g + 1) * HEAD_DIM)
        k = jnp.concatenate([kp_ref[:, ks], kc_ref[:, ks]], axis=0).astype(BF16)
        v = jnp.concatenate([vp_ref[:, ks], vc_ref[:, ks]], axis=0).astype(BF16)
        q = jnp.concatenate(
            [q_ref[:, (g * GQA_GROUP + j) * HEAD_DIM:(g * GQA_GROUP + j + 1) * HEAD_DIM]
             for j in range(GQA_GROUP)], axis=0).astype(BF16)
        s = lax.dot_general(q, k, (((1,), (1,)), ((), ())), preferred_element_type=F32)
        s = jnp.where(mask, s * (HEAD_DIM ** -0.5), NEG_INF)
        sink = _sink_column(sink_ref, g, WINDOW)
        m = jnp.maximum(jnp.max(s, axis=1, keepdims=True), sink)
        p = jnp.exp(s - m)
        denom = jnp.sum(p, axis=1, keepdims=True) + jnp.exp(sink - m)
        o = jnp.dot(p.astype(BF16), v, preferred_element_type=F32) / denom
        o_ref[:, g * GQA_GROUP * HEAD_DIM:(g + 1) * GQA_GROUP * HEAD_DIM] = jnp.concatenate(
            [o[j * WINDOW:(j + 1) * WINDOW] for j in range(GQA_GROUP)], axis=1).astype(o_ref.dtype)


def _attn_prompt(qkv, sinks, batch, seq):
    nb = seq // WINDOW
    kcol = Q_COLS // KV_COLS
    vcol = kcol + 1

    def cur(b, i):
        return b * nb + i

    def prev(b, i):
        return b * nb + jnp.maximum(i - 1, 0)

    return pl.pallas_call(
        _attn_prompt_kernel,
        grid=(batch, nb),
        in_specs=[
            pl.BlockSpec(memory_space=pltpu.SMEM),
            pl.BlockSpec((WINDOW, Q_COLS), lambda b, i: (cur(b, i), 0)),
            pl.BlockSpec((WINDOW, KV_COLS), lambda b, i: (prev(b, i), kcol)),
            pl.BlockSpec((WINDOW, KV_COLS), lambda b, i: (cur(b, i), kcol)),
            pl.BlockSpec((WINDOW, KV_COLS), lambda b, i: (prev(b, i), vcol)),
            pl.BlockSpec((WINDOW, KV_COLS), lambda b, i: (cur(b, i), vcol)),
        ],
        out_specs=pl.BlockSpec((WINDOW, Q_COLS), lambda b, i: (cur(b, i), 0)),
        out_shape=jax.ShapeDtypeStruct((batch * seq, Q_COLS), BF16),
        compiler_params=_cparams("parallel", "arbitrary"),
        name="attn_prompt",
    )(sinks, qkv, qkv, qkv, qkv, qkv)


SAMPLE_ROWS = 8


def _attn_sample_kernel(sink_ref, qkv_ref, ck_ref, cv_ref, o_ref, *, n_new):
    qkv = qkv_ref[0]
    rows = GQA_GROUP * SAMPLE_ROWS
    tok = lax.broadcasted_iota(jnp.int32, (rows, WINDOW), 0) % SAMPLE_ROWS
    kj = lax.broadcasted_iota(jnp.int32, (rows, WINDOW), 1)
    mask_c = kj >= tok
    tok1 = lax.broadcasted_iota(jnp.int32, (rows, 1), 0) % SAMPLE_ROWS
    for g in range(N_KV_HEADS):
        ks = slice(g * HEAD_DIM, (g + 1) * HEAD_DIM)
        kc = ck_ref[0, :, ks].astype(BF16)
        vc = cv_ref[0, :, ks].astype(BF16)
        kn = qkv[:, Q_COLS + g * HEAD_DIM:Q_COLS + (g + 1) * HEAD_DIM]
        vn = qkv[:, Q_COLS + KV_COLS + g * HEAD_DIM:Q_COLS + KV_COLS + (g + 1) * HEAD_DIM]
        q = jnp.concatenate(
            [qkv[:, (g * GQA_GROUP + j) * HEAD_DIM:(g * GQA_GROUP + j + 1) * HEAD_DIM]
             for j in range(GQA_GROUP)], axis=0)
        s_c = lax.dot_general(q.astype(BF16), kc, (((1,), (1,)), ((), ())), preferred_element_type=F32)
        s_c = jnp.where(mask_c, s_c * (HEAD_DIM ** -0.5), NEG_INF)
        s_n = []
        for r in range(n_new):
            s_r = jnp.sum(q * kn[r:r + 1, :], axis=1, keepdims=True) * (HEAD_DIM ** -0.5)
            s_n.append(jnp.where(tok1 >= r, s_r, NEG_INF))
        sink = _sink_column(sink_ref, g, SAMPLE_ROWS)
        m = jnp.maximum(jnp.max(s_c, axis=1, keepdims=True), sink)
        for s_r in s_n:
            m = jnp.maximum(m, s_r)
        p_c = jnp.exp(s_c - m)
        denom = jnp.sum(p_c, axis=1, keepdims=True) + jnp.exp(sink - m)
        o = jnp.dot(p_c.astype(BF16), vc, preferred_element_type=F32)
        for r, s_r in enumerate(s_n):
            p_r = jnp.exp(s_r - m)
            denom = denom + p_r
            o = o + p_r * vn[r:r + 1, :]
        o = o / denom
        o_ref[0, :, g * GQA_GROUP * HEAD_DIM:(g + 1) * GQA_GROUP * HEAD_DIM] = jnp.concatenate(
            [o[j * SAMPLE_ROWS:(j + 1) * SAMPLE_ROWS] for j in range(GQA_GROUP)], axis=1)


def _attn_sample(qkv8, ck, cv, sinks, n_new):
    nb = qkv8.shape[0]
    return pl.pallas_call(
        functools.partial(_attn_sample_kernel, n_new=n_new),
        grid=(nb,),
        in_specs=[
            pl.BlockSpec(memory_space=pltpu.SMEM),
            pl.BlockSpec((1, SAMPLE_ROWS, QKV_COLS), lambda b: (b, 0, 0)),
            pl.BlockSpec((1, WINDOW, KV_COLS), lambda b: (b, 0, 0)),
            pl.BlockSpec((1, WINDOW, KV_COLS), lambda b: (b, 0, 0)),
        ],
        out_specs=pl.BlockSpec((1, SAMPLE_ROWS, Q_COLS), lambda b: (b, 0, 0)),
        out_shape=jax.ShapeDtypeStruct((nb, SAMPLE_ROWS, Q_COLS), F32),
        compiler_params=_cparams("parallel"),
        name="attn_sample",
    )(sinks, qkv8, ck, cv)


def _proj_ln_kernel(a_ref, w_ref, b_ref, x_ref, g_ref, beta_ref, o_ref):
    y = jnp.dot(a_ref[...].astype(BF16), w_ref[...], preferred_element_type=F32) + b_ref[...]
    o_ref[...] = _layer_norm(DEEPNORM_ALPHA * x_ref[...] + y, g_ref[...], beta_ref[...])


def _proj_ln(a, w_bf, bias, x, g, beta, tm):
    n, k = a.shape
    row = lambda i: (i, 0)
    fixed = lambda i: (0, 0)
    return pl.pallas_call(
        _proj_ln_kernel,
        grid=(n // tm,),
        in_specs=[
            pl.BlockSpec((tm, k), row),
            pl.BlockSpec((k, D_MODEL), fixed),
            pl.BlockSpec((1, D_MODEL), fixed),
            pl.BlockSpec((tm, D_MODEL), row),
            pl.BlockSpec((1, D_MODEL), fixed),
            pl.BlockSpec((1, D_MODEL), fixed),
        ],
        out_specs=pl.BlockSpec((tm, D_MODEL), row),
        out_shape=jax.ShapeDtypeStruct((n, D_MODEL), F32),
        compiler_params=_cparams("parallel"),
        name="proj_ln",
    )(a, w_bf, bias.reshape(1, D_MODEL), x, g.reshape(1, D_MODEL), beta.reshape(1, D_MODEL))


CONV_TN = 512


def _conv_in_kernel(x_ref, wb_ref, wc_ref, wh_ref, gb_ref, u_ref):
    xb = x_ref[...].astype(BF16)
    gb_ref[...] = jnp.dot(xb, wb_ref[...], preferred_element_type=F32)
    gate_c = jnp.dot(xb, wc_ref[...], preferred_element_type=F32)
    h = jnp.dot(xb, wh_ref[...], preferred_element_type=F32)
    u_ref[...] = gate_c * h


def _conv_in(x, w_in_bf, tm):
    n = x.shape[0]
    nt = D_MODEL // CONV_TN
    return pl.pallas_call(
        _conv_in_kernel,
        grid=(n // tm, nt),
        in_specs=[
            pl.BlockSpec((tm, D_MODEL), lambda i, j: (i, 0)),
            pl.BlockSpec((D_MODEL, CONV_TN), lambda i, j: (0, j)),
            pl.BlockSpec((D_MODEL, CONV_TN), lambda i, j: (0, nt + j)),
            pl.BlockSpec((D_MODEL, CONV_TN), lambda i, j: (0, 2 * nt + j)),
        ],
        out_specs=[
            pl.BlockSpec((tm, CONV_TN), lambda i, j: (i, j)),
            pl.BlockSpec((tm, CONV_TN), lambda i, j: (i, j)),
        ],
        out_shape=[jax.ShapeDtypeStruct((n, D_MODEL), F32)] * 2,
        compiler_params=_cparams("parallel", "arbitrary"),
        name="conv_in",
    )(x, w_in_bf, w_in_bf, w_in_bf)


def _conv_out_ln_kernel(gb_ref, u_ref, um1_ref, um2_ref, cw_ref, w_ref, x_ref, g_ref, beta_ref, o_ref):
    conv = um2_ref[...] * cw_ref[0:1, :]
    conv = conv + um1_ref[...] * cw_ref[1:2, :]
    conv = conv + u_ref[...] * cw_ref[2:3, :]
    a = (gb_ref[...] * conv).astype(BF16)
    y = jnp.dot(a, w_ref[...], preferred_element_type=F32)
    o_ref[...] = _layer_norm(DEEPNORM_ALPHA * x_ref[...] + y, g_ref[...], beta_ref[...])


def _conv_out_ln(gate_b, u, um1, um2, conv_w, w_out_bf, x, g, beta, tm):
    n = x.shape[0]
    row = lambda i: (i, 0)
    fixed = lambda i: (0, 0)
    tile = pl.BlockSpec((tm, D_MODEL), row)
    vec = pl.BlockSpec((1, D_MODEL), fixed)
    return pl.pallas_call(
        _conv_out_ln_kernel,
        grid=(n // tm,),
        in_specs=[tile, tile, tile, tile,
                  pl.BlockSpec((CONV_WIDTH, D_MODEL), fixed),
                  pl.BlockSpec((D_MODEL, D_MODEL), fixed),
                  tile, vec, vec],
        out_specs=tile,
        out_shape=jax.ShapeDtypeStruct((n, D_MODEL), F32),
        compiler_params=_cparams("parallel"),
        name="conv_out_ln",
    )(gate_b, u, um1, um2, conv_w, w_out_bf, x, g.reshape(1, D_MODEL), beta.reshape(1, D_MODEL))


def _top_values(s, k):
    vals = []
    work = s
    for it in range(k):
        m = jnp.max(work, axis=0, keepdims=True)
        vals.append(m)
        if it + 1 < k:
            work = jnp.where(work == m, -jnp.inf, work)
    return vals


def _peer_score_kernel(x_ref, wq_ref, keys_ref, xt_ref, s1_ref, e1z_ref, s2_ref, e2_ref, tau_ref):
    x = x_ref[...]
    xt_ref[...] = x.T.astype(BF16)
    q = jnp.dot(x.astype(BF16), wq_ref[...], preferred_element_type=F32).astype(BF16)
    nt = (((1,), (1,)), ((), ()))
    for h in range(PEER_HEADS):
        c0 = h * 2 * PEER_HALF
        s1 = lax.dot_general(keys_ref[h, 0], q[:, c0:c0 + PEER_HALF], nt, preferred_element_type=F32)
        s2 = lax.dot_general(keys_ref[h, 1], q[:, c0 + PEER_HALF:c0 + 2 * PEER_HALF], nt,
                             preferred_element_type=F32)
        a = _top_values(s1, PEER_TOPK)
        b = _top_values(s2, PEER_TOPK)
        b16 = jnp.concatenate(b, axis=0)
        cand = [a[i] + b16 for i in range(PEER_TOPK)]
        work = cand
        tau = None
        for it in range(PEER_TOPK):
            tau = jnp.max(functools.reduce(jnp.maximum, work), axis=0, keepdims=True)
            if it + 1 < PEER_TOPK:
                work = [jnp.where(w == tau, -jnp.inf, w) for w in work]
        eb16 = jnp.exp(b16 - b[0])
        z = jnp.zeros_like(tau)
        for i in range(PEER_TOPK):
            w_i = jnp.exp(a[i] - a[0]) * eb16
            z = z + jnp.sum(jnp.where(cand[i] >= tau, w_i, 0.0), axis=0, keepdims=True)
        s1_ref[h] = s1
        s2_ref[h] = s2
        e1z_ref[h] = jnp.exp(s1 - a[0]) / z
        e2_ref[h] = jnp.exp(s2 - b[0])
        tau_ref[h:h + 1, :] = tau


def _peer_scores(x, wq_bf, keys_bf, tm):
    n = x.shape[0]
    hk = (PEER_HEADS, PEER_KEYS, tm)
    hk_map = lambda i: (0, 0, i)
    hk_shape = jax.ShapeDtypeStruct((PEER_HEADS, PEER_KEYS, n), F32)
    return pl.pallas_call(
        _peer_score_kernel,
        grid=(n // tm,),
        in_specs=[
            pl.BlockSpec((tm, D_MODEL), lambda i: (i, 0)),
            pl.BlockSpec((D_MODEL, PEER_HEADS * 2 * PEER_HALF), lambda i: (0, 0)),
            pl.BlockSpec((PEER_HEADS, 2, PEER_KEYS, PEER_HALF), lambda i: (0, 0, 0, 0)),
        ],
        out_specs=[
            pl.BlockSpec((D_MODEL, tm), lambda i: (0, i)),
            pl.BlockSpec(hk, hk_map), pl.BlockSpec(hk, hk_map),
            pl.BlockSpec(hk, hk_map), pl.BlockSpec(hk, hk_map),
            pl.BlockSpec((PEER_HEADS, tm), lambda i: (0, i)),
        ],
        out_shape=[
            jax.ShapeDtypeStruct((D_MODEL, n), BF16),
            hk_shape, hk_shape, hk_shape, hk_shape,
            jax.ShapeDtypeStruct((PEER_HEADS, n), F32),
        ],
        compiler_params=_cparams("parallel"),
        name="peer_scores",
    )(x, wq_bf, keys_bf)


PEER_TE = 256
I1_PER_TILE = PEER_TE // PEER_KEYS


def _gelu_exact(x):
    return 0.5 * x * (1.0 + lax.erf(x * (2.0 ** -0.5)))


def _peer_dense_kernel(xt_ref, u_ref, v_ref, s1_ref, e1z_ref, s2_ref, e2_ref, tau_ref, o_ref,
                       hid_ref, at_ref):
    e = pl.program_id(1)
    tm = xt_ref.shape[1]
    hid_ref[...] = jnp.dot(u_ref[...].astype(BF16), xt_ref[...], preferred_element_type=F32)
    for il in range(I1_PER_TILE):
        rs = slice(il * PEER_KEYS, (il + 1) * PEER_KEYS)
        for c in range(tm // LANES):
            cs = slice(c * LANES, (c + 1) * LANES)
            gate = jnp.zeros((PEER_KEYS, LANES), F32)
            for h in range(PEER_HEADS):
                pair = s2_ref[h, :, cs] + s1_ref[il, h:h + 1, cs]
                sel = pair >= tau_ref[h:h + 1, cs]
                gate = gate + jnp.where(sel, e2_ref[h, :, cs], 0.0) * e1z_ref[il, h:h + 1, cs]
            at_ref[rs, cs] = (_gelu_exact(hid_ref[rs, cs]) * gate).astype(BF16)
    contrib = lax.dot_general(at_ref[...], v_ref[...].astype(BF16), (((0,), (0,)), ((), ())),
                              preferred_element_type=F32)

    @pl.when(e == 0)
    def _():
        o_ref[...] = contrib

    @pl.when(e > 0)
    def _():
        o_ref[...] += contrib


def _peer_dense(xt, u, v, s1r, e1zr, s2, e2, tau, tm):
    n = xt.shape[1]
    row_spec = pl.BlockSpec((I1_PER_TILE, PEER_HEADS, tm), lambda t, e: (e, 0, t))
    key_spec = pl.BlockSpec((PEER_HEADS, PEER_KEYS, tm), lambda t, e: (0, 0, t),
                            pipeline_mode=pl.Buffered(1))
    return pl.pallas_call(
        _peer_dense_kernel,
        grid=(n // tm, PEER_EXPERTS // PEER_TE),
        in_specs=[
            pl.BlockSpec((D_MODEL, tm), lambda t, e: (0, t)),
            pl.BlockSpec((PEER_TE, D_MODEL), lambda t, e: (e, 0)),
            pl.BlockSpec((PEER_TE, D_MODEL), lambda t, e: (e, 0)),
            row_spec, row_spec, key_spec, key_spec,
            pl.BlockSpec((PEER_HEADS, tm), lambda t, e: (0, t)),
        ],
        out_specs=pl.BlockSpec((tm, D_MODEL), lambda t, e: (t, 0)),
        out_shape=jax.ShapeDtypeStruct((n, D_MODEL), F32),
        scratch_shapes=[pltpu.VMEM((PEER_TE, tm), F32), pltpu.VMEM((PEER_TE, tm), BF16)],
        compiler_params=_cparams("parallel", "arbitrary"),
        name="peer_dense",
    )(xt, u, v, s1r, e1zr, s2, e2, tau)


def _res_ln_kernel(x_ref, f_ref, g_ref, beta_ref, o_ref):
    o_ref[...] = _layer_norm(DEEPNORM_ALPHA * x_ref[...] + f_ref[...], g_ref[...], beta_ref[...])


def _res_ln(x, f, g, beta, tm):
    n = x.shape[0]
    tile = pl.BlockSpec((tm, D_MODEL), lambda i: (i, 0))
    vec = pl.BlockSpec((1, D_MODEL), lambda i: (0, 0))
    return pl.pallas_call(
        _res_ln_kernel,
        grid=(n // tm,),
        in_specs=[tile, tile, vec, vec],
        out_specs=tile,
        out_shape=jax.ShapeDtypeStruct((n, D_MODEL), F32),
        compiler_params=_cparams("parallel"),
        name="res_ln",
    )(x, f, g.reshape(1, D_MODEL), beta.reshape(1, D_MODEL))


def _peer_layer(x, wq_bf, keys_bf, u, v, g, beta, tm_score, tm_dense, tm_ln):
    xt, s1, e1z, s2, e2, tau = _peer_scores(x, wq_bf, keys_bf, tm_score)
    s1r = jnp.transpose(s1, (1, 0, 2))
    e1zr = jnp.transpose(e1z, (1, 0, 2))
    f = _peer_dense(xt, u, v, s1r, e1zr, s2, e2, tau, tm_dense)
    return _res_ln(x, f, g, beta, tm_ln)


def _rope_tables(pos):
    half = HEAD_DIM // 2
    inv = jnp.power(ROPE_THETA, -jnp.arange(half, dtype=F32) * 2.0 / HEAD_DIM)
    ang = pos.astype(F32)[:, None] * inv[None, :]
    cos = jnp.cos(ang)
    sin = jnp.sin(ang)
    reps = LANES // HEAD_DIM
    return (jnp.tile(jnp.concatenate([cos, cos], axis=1), (1, reps)),
            jnp.tile(jnp.concatenate([-sin, sin], axis=1), (1, reps)))


def _shifted(state, u3):
    t = u3.shape[1]
    up = jnp.concatenate([state, u3], axis=1)
    return up[:, 1:t + 1], up[:, 0:t], up[:, -(CONV_WIDTH - 1):]


def kernel(x_prompt, x_sample, cache_k, cache_v, state_conv, w_qkv, b_qkv, w_o, b_o, attn_sinks,
           w_conv_in, conv_w, w_conv_out, w_peer_q, peer_sub_keys, peer_u, peer_v, ln_g, ln_b):
    batch, seq, _ = x_prompt.shape
    dec_batch, dec_seq, _ = x_sample.shape
    n_p = batch * seq
    n_s = dec_batch * dec_seq
    xp = x_prompt.reshape(n_p, D_MODEL)
    xs = x_sample.reshape(n_s, D_MODEL)
    cos_p, sin_p = _rope_tables(jnp.tile(jnp.arange(seq), batch))
    cos_s, sin_s = _rope_tables(jnp.tile(PAST_LEN + jnp.arange(dec_seq), dec_batch))

    new_kp, new_vp, new_cp, new_ks, new_vs, new_cs = [], [], [], [], [], []
    for layer in range(DEPTH):
        j = layer // 2
        g0, b0 = ln_g[layer, 0], ln_b[layer, 0]
        if layer % 2 == 0:
            wqkv = w_qkv[j].astype(BF16)
            wo = w_o[j].astype(BF16)
            sinks = attn_sinks[j].reshape(N_Q_HEADS)
            qkv_p = _qkv_rope(xp, wqkv, b_qkv[j], cos_p, sin_p, 512)
            qkv_s = _qkv_rope(xs, wqkv, b_qkv[j], cos_s, sin_s, n_s)
            o_p = _attn_prompt(qkv_p, sinks, batch, seq)
            qkv_s3 = qkv_s.reshape(dec_batch, dec_seq, QKV_COLS)
            qkv_s8 = jnp.pad(qkv_s3, ((0, 0), (0, SAMPLE_ROWS - dec_seq), (0, 0)))
            ck = cache_k[j].reshape(dec_batch, WINDOW, KV_COLS)
            cv = cache_v[j].reshape(dec_batch, WINDOW, KV_COLS)
            o_s = _attn_sample(qkv_s8, ck, cv, sinks, dec_seq)[:, :dec_seq].reshape(n_s, Q_COLS)
            kv_p = qkv_p.reshape(batch, seq, QKV_COLS)[:, -WINDOW:, Q_COLS:]
            new_kp.append(kv_p[..., :KV_COLS].reshape(batch, WINDOW, N_KV_HEADS, HEAD_DIM))
            new_vp.append(kv_p[..., KV_COLS:].reshape(batch, WINDOW, N_KV_HEADS, HEAD_DIM))
            k_s = jnp.concatenate([ck, qkv_s3[..., Q_COLS:Q_COLS + KV_COLS]], axis=1)[:, -WINDOW:]
            v_s = jnp.concatenate([cv, qkv_s3[..., Q_COLS + KV_COLS:]], axis=1)[:, -WINDOW:]
            new_ks.append(k_s.reshape(dec_batch, WINDOW, N_KV_HEADS, HEAD_DIM))
            new_vs.append(v_s.reshape(dec_batch, WINDOW, N_KV_HEADS, HEAD_DIM))
            xp = _proj_ln(o_p, wo, b_o[j], xp, g0, b0, 512)
            xs = _proj_ln(o_s, wo, b_o[j], xs, g0, b0, n_s)
        else:
            w_in = w_conv_in[j].astype(BF16)
            w_out = w_conv_out[j].astype(BF16)
            gb_p, u_p = _conv_in(xp, w_in, 512)
            gb_s, u_s = _conv_in(xs, w_in, n_s)
            zero_state = jnp.zeros((batch, CONV_WIDTH - 1, D_MODEL), F32)
            um1_p, um2_p, cp = _shifted(zero_state, u_p.reshape(batch, seq, D_MODEL))
            um1_s, um2_s, cs = _shifted(state_conv[j], u_s.reshape(dec_batch, dec_seq, D_MODEL))
            new_cp.append(cp)
            new_cs.append(cs)
            xp = _conv_out_ln(gb_p, u_p, um1_p.reshape(n_p, D_MODEL), um2_p.reshape(n_p, D_MODEL),
                              conv_w[j], w_out, xp, g0, b0, 256)
            xs = _conv_out_ln(gb_s, u_s, um1_s.reshape(n_s, D_MODEL), um2_s.reshape(n_s, D_MODEL),
                              conv_w[j], w_out, xs, g0, b0, n_s)
        wq = w_peer_q[layer].astype(BF16)
        keys = peer_sub_keys[layer].astype(BF16)
        g1, b1 = ln_g[layer, 1], ln_b[layer, 1]
        xp = _peer_layer(xp, wq, keys, peer_u[layer], peer_v[layer], g1, b1, 256, 1024, 512)
        xs = _peer_layer(xs, wq, keys, peer_u[layer], peer_v[layer], g1, b1, n_s, n_s, n_s)

    return (xp.reshape(batch, seq, D_MODEL), xs.reshape(dec_batch, dec_seq, D_MODEL),
            jnp.stack(new_kp), jnp.stack(new_vp), jnp.stack(new_cp),
            jnp.stack(new_ks), jnp.stack(new_vs), jnp.stack(new_cs))
```

```python
import functools

import jax
import jax.numpy as jnp
from jax import lax
from jax.experimental import pallas as pl
from jax.experimental.pallas import tpu as pltpu

F32 = jnp.float32
BF16 = jnp.bfloat16

D_MODEL = 2048
DEPTH = 4
HEAD_DIM = 64
N_Q_HEADS = 32
N_KV_HEADS = 8
GQA_GROUP = 4
WINDOW = 128
ROPE_THETA = 10000.0
PAST_LEN = 16384
CONV_WIDTH = 3
PEER_HEADS = 8
PEER_KEYS = 128
PEER_EXPERTS = PEER_KEYS * PEER_KEYS
PEER_HALF = 128
PEER_TOPK = 16
LN_EPS = 1e-5
NEG_INF = -1e30
DEEPNORM_ALPHA = (2 * DEPTH) ** 0.25

Q_COLS = N_Q_HEADS * HEAD_DIM
KV_COLS = N_KV_HEADS * HEAD_DIM
QKV_COLS = Q_COLS + 2 * KV_COLS

LANES = 128
VMEM_LIMIT_BYTES = 56 * 1024 * 1024


def _cparams(*semantics):
    return pltpu.CompilerParams(dimension_semantics=semantics, vmem_limit_bytes=VMEM_LIMIT_BYTES)


def _layer_norm(z, g, b):
    mu = jnp.mean(z, axis=-1, keepdims=True)
    zc = z - mu
    var = jnp.mean(zc * zc, axis=-1, keepdims=True)
    return zc * lax.rsqrt(var + LN_EPS) * g + b


QKV_TN = 512
N_ROPE_TILES = (Q_COLS + KV_COLS) // QKV_TN


def _qkv_kernel(x_ref, w_ref, b_ref, cos_ref, sin_ref, o_ref):
    j = pl.program_id(1)
    acc = jnp.dot(x_ref[...].astype(BF16), w_ref[...], preferred_element_type=F32) + b_ref[...]

    @pl.when(j < N_ROPE_TILES)
    def _():
        cos = cos_ref[...]
        sin = sin_ref[...]
        lane = lax.broadcasted_iota(jnp.int32, cos.shape, 1)
        first_half = (lane % HEAD_DIM) < (HEAD_DIM // 2)
        for c in range(QKV_TN // LANES):
            seg = acc[:, c * LANES:(c + 1) * LANES]
            partner = jnp.where(first_half,
                                pltpu.roll(seg, LANES - HEAD_DIM // 2, 1),
                                pltpu.roll(seg, HEAD_DIM // 2, 1))
            o_ref[:, c * LANES:(c + 1) * LANES] = seg * cos + partner * sin

    @pl.when(j >= N_ROPE_TILES)
    def _():
        o_ref[...] = acc


def _qkv_rope(x, w_bf, b, cos, sin, tm):
    n = x.shape[0]
    return pl.pallas_call(
        _qkv_kernel,
        grid=(n // tm, QKV_COLS // QKV_TN),
        in_specs=[
            pl.BlockSpec((tm, D_MODEL), lambda i, j: (i, 0)),
            pl.BlockSpec((D_MODEL, QKV_TN), lambda i, j: (0, j)),
            pl.BlockSpec((1, QKV_TN), lambda i, j: (0, j)),
            pl.BlockSpec((tm, LANES), lambda i, j: (i, 0)),
            pl.BlockSpec((tm, LANES), lambda i, j: (i, 0)),
        ],
        out_specs=pl.BlockSpec((tm, QKV_TN), lambda i, j: (i, j)),
        out_shape=jax.ShapeDtypeStruct((n, QKV_COLS), F32),
        compiler_params=_cparams("parallel", "arbitrary"),
        name="qkv_rope",
    )(x, w_bf, b.reshape(1, QKV_COLS), cos, sin)


def _sink_column(sink_ref, g, rows):
    return jnp.concatenate(
        [jnp.full((rows, 1), sink_ref[g * GQA_GROUP + j], F32) for j in range(GQA_GROUP)], axis=0)


def _attn_prompt_kernel(sink_ref, q_ref, kp_ref, kc_ref, vp_ref, vc_ref, o_ref):
    i = pl.program_id(1)
    rows = GQA_GROUP * WINDOW
    qi = lax.broadcasted_iota(jnp.int32, (rows, 2 * WINDOW), 0) % WINDOW
    kj = lax.broadcasted_iota(jnp.int32, (rows, 2 * WINDOW), 1)
    dist = qi + WINDOW - kj
    prev_ok = jnp.where(i > 0, 0, WINDOW)
    mask = (dist >= 0) & (dist <= WINDOW) & (kj >= prev_ok)
    for g in range(N_KV_HEADS):
        ks = slice(g * HEAD_DIM, (g + 1) * HEAD_DIM)
        k = jnp.concatenate([kp_ref[:, ks], kc_ref[:, ks]], axis=0).astype(BF16)
        v = jnp.concatenate([vp_ref[:, ks], vc_ref[:, ks]], axis=0).astype(BF16)
        q = jnp.concatenate(
            [q_ref[:, (g * GQA_GROUP + j) * HEAD_DIM:(g * GQA_GROUP + j + 1) * HEAD_DIM]
             for j in range(GQA_GROUP)], axis=0).astype(BF16)
        s = lax.dot_general(q, k, (((1,), (1,)), ((), ())), preferred_element_type=F32)
        s = jnp.where(mask, s * (HEAD_DIM ** -0.5), NEG_INF)
        sink = _sink_column(sink_ref, g, WINDOW)
        m = jnp.maximum(jnp.max(s, axis=1, keepdims=True), sink)
        p = jnp.exp(s - m)
        denom = jnp.sum(p, axis=1, keepdims=True) + jnp.exp(sink - m)
        o = jnp.dot(p.astype(BF16), v, preferred_element_type=F32) / denom
        o_ref[:, g * GQA_GROUP * HEAD_DIM:(g + 1) * GQA_GROUP * HEAD_DIM] = jnp.concatenate(
            [o[j * WINDOW:(j + 1) * WINDOW] for j in range(GQA_GROUP)], axis=1).astype(o_ref.dtype)


def _attn_prompt(qkv, sinks, batch, seq):
    nb = seq // WINDOW
    kcol = Q_COLS // KV_COLS
    vcol = kcol + 1

    def cur(b, i):
        return b * nb + i

    def prev(b, i):
        return b * nb + jnp.maximum(i - 1, 0)

    return pl.pallas_call(
        _attn_prompt_kernel,
        grid=(batch, nb),
        in_specs=[
            pl.BlockSpec(memory_space=pltpu.SMEM),
            pl.BlockSpec((WINDOW, Q_COLS), lambda b, i: (cur(b, i), 0)),
            pl.BlockSpec((WINDOW, KV_COLS), lambda b, i: (prev(b, i), kcol)),
            pl.BlockSpec((WINDOW, KV_COLS), lambda b, i: (cur(b, i), kcol)),
            pl.BlockSpec((WINDOW, KV_COLS), lambda b, i: (prev(b, i), vcol)),
            pl.BlockSpec((WINDOW, KV_COLS), lambda b, i: (cur(b, i), vcol)),
        ],
        out_specs=pl.BlockSpec((WINDOW, Q_COLS), lambda b, i: (cur(b, i), 0)),
        out_shape=jax.ShapeDtypeStruct((batch * seq, Q_COLS), BF16),
        compiler_params=_cparams("parallel", "arbitrary"),
        name="attn_prompt",
    )(sinks, qkv, qkv, qkv, qkv, qkv)


SAMPLE_ROWS = 8


def _attn_sample_kernel(sink_ref, qkv_ref, ck_ref, cv_ref, o_ref, *, n_new):
    qkv = qkv_ref[0]
    rows = GQA_GROUP * SAMPLE_ROWS
    tok = lax.broadcasted_iota(jnp.int32, (rows, WINDOW), 0) % SAMPLE_ROWS
    kj = lax.broadcasted_iota(jnp.int32, (rows, WINDOW), 1)
    mask_c = kj >= tok
    tok1 = lax.broadcasted_iota(jnp.int32, (rows, 1), 0) % SAMPLE_ROWS
    for g in range(N_KV_HEADS):
        ks = slice(g * HEAD_DIM, (g + 1) * HEAD_DIM)
        kc = ck_ref[0, :, ks].astype(BF16)
        vc = cv_ref[0, :, ks].astype(BF16)
        kn = qkv[:, Q_COLS + g * HEAD_DIM:Q_COLS + (g + 1) * HEAD_DIM]
        vn = qkv[:, Q_COLS + KV_COLS + g * HEAD_DIM:Q_COLS + KV_COLS + (g + 1) * HEAD_DIM]
        q = jnp.concatenate(
            [qkv[:, (g * GQA_GROUP + j) * HEAD_DIM:(g * GQA_GROUP + j + 1) * HEAD_DIM]
             for j in range(GQA_GROUP)], axis=0)
        s_c = lax.dot_general(q.astype(BF16), kc, (((1,), (1,)), ((), ())), preferred_element_type=F32)
        s_c = jnp.where(mask_c, s_c * (HEAD_DIM ** -0.5), NEG_INF)
        s_n = []
        for r in range(n_new):
            s_r = jnp.sum(q * kn[r:r + 1, :], axis=1, keepdims=True) * (HEAD_DIM ** -0.5)
            s_n.append(jnp.where(tok1 >= r, s_r, NEG_INF))
        sink = _sink_column(sink_ref, g, SAMPLE_ROWS)
        m = jnp.maximum(jnp.max(s_c, axis=1, keepdims=True), sink)
        for s_r in s_n:
            m = jnp.maximum(m, s_r)
        p_c = jnp.exp(s_c - m)
        denom = jnp.sum(p_c, axis=1, keepdims=True) + jnp.exp(sink - m)
        o = jnp.dot(p_c.astype(BF16), vc, preferred_element_type=F32)
        for r, s_r in enumerate(s_n):
            p_r = jnp.exp(s_r - m)
            denom = denom + p_r
            o = o + p_r * vn[r:r + 1, :]
        o = o / denom
        o_ref[0, :, g * GQA_GROUP * HEAD_DIM:(g + 1) * GQA_GROUP * HEAD_DIM] = jnp.concatenate(
            [o[j * SAMPLE_ROWS:(j + 1) * SAMPLE_ROWS] for j in range(GQA_GROUP)], axis=1)


def _attn_sample(qkv8, ck, cv, sinks, n_new):
    nb = qkv8.shape[0]
    return pl.pallas_call(
        functools.partial(_attn_sample_kernel, n_new=n_new),
        grid=(nb,),
        in_specs=[
            pl.BlockSpec(memory_space=pltpu.SMEM),
            pl.BlockSpec((1, SAMPLE_ROWS, QKV_COLS), lambda b: (b, 0, 0)),
            pl.BlockSpec((1, WINDOW, KV_COLS), lambda b: (b, 0, 0)),
            pl.BlockSpec((1, WINDOW, KV_COLS), lambda b: (b, 0, 0)),
        ],
        out_specs=pl.BlockSpec((1, SAMPLE_ROWS, Q_COLS), lambda b: (b, 0, 0)),
        out_shape=jax.ShapeDtypeStruct((nb, SAMPLE_ROWS, Q_COLS), F32),
        compiler_params=_cparams("parallel"),
        name="attn_sample",
    )(sinks, qkv8, ck, cv)


def _proj_ln_kernel(a_ref, w_ref, b_ref, x_ref, g_ref, beta_ref, o_ref):
    y = jnp.dot(a_ref[...].astype(BF16), w_ref[...], preferred_element_type=F32) + b_ref[...]
    o_ref[...] = _layer_norm(DEEPNORM_ALPHA * x_ref[...] + y, g_ref[...], beta_ref[...])


def _proj_ln(a, w_bf, bias, x, g, beta, tm):
    n, k = a.shape
    row = lambda i: (i, 0)
    fixed = lambda i: (0, 0)
    return pl.pallas_call(
        _proj_ln_kernel,
        grid=(n // tm,),
        in_specs=[
            pl.BlockSpec((tm, k), row),
            pl.BlockSpec((k, D_MODEL), fixed),
            pl.BlockSpec((1, D_MODEL), fixed),
            pl.BlockSpec((tm, D_MODEL), row),
            pl.BlockSpec((1, D_MODEL), fixed),
            pl.BlockSpec((1, D_MODEL), fixed),
        ],
        out_specs=pl.BlockSpec((tm, D_MODEL), row),
        out_shape=jax.ShapeDtypeStruct((n, D_MODEL), F32),
        compiler_params=_cparams("parallel"),
        name="proj_ln",
    )(a, w_bf, bias.reshape(1, D_MODEL), x, g.reshape(1, D_MODEL), beta.reshape(1, D_MODEL))


CONV_TN = 512


def _conv_in_kernel(x_ref, wb_ref, wc_ref, wh_ref, gb_ref, u_ref):
    xb = x_ref[...].astype(BF16)
    gb_ref[...] = jnp.dot(xb, wb_ref[...], preferred_element_type=F32)
    gate_c = jnp.dot(xb, wc_ref[...], preferred_element_type=F32)
    h = jnp.dot(xb, wh_ref[...], preferred_element_type=F32)
    u_ref[...] = gate_c * h


def _conv_in(x, w_in_bf, tm):
    n = x.shape[0]
    nt = D_MODEL // CONV_TN
    return pl.pallas_call(
        _conv_in_kernel,
        grid=(n // tm, nt),
        in_specs=[
            pl.BlockSpec((tm, D_MODEL), lambda i, j: (i, 0)),
            pl.BlockSpec((D_MODEL, CONV_TN), lambda i, j: (0, j)),
            pl.BlockSpec((D_MODEL, CONV_TN), lambda i, j: (0, nt + j)),
            pl.BlockSpec((D_MODEL, CONV_TN), lambda i, j: (0, 2 * nt + j)),
        ],
        out_specs=[
            pl.BlockSpec((tm, CONV_TN), lambda i, j: (i, j)),
            pl.BlockSpec((tm, CONV_TN), lambda i, j: (i, j)),
        ],
        out_shape=[jax.ShapeDtypeStruct((n, D_MODEL), F32)] * 2,
        compiler_params=_cparams("parallel", "arbitrary"),
        name="conv_in",
    )(x, w_in_bf, w_in_bf, w_in_bf)


def _conv_out_ln_kernel(gb_ref, u_ref, um1_ref, um2_ref, cw_ref, w_ref, x_ref, g_ref, beta_ref, o_ref):
    conv = um2_ref[...] * cw_ref[0:1, :]
    conv = conv + um1_ref[...] * cw_ref[1:2, :]
    conv = conv + u_ref[...] * cw_ref[2:3, :]
    a = (gb_ref[...] * conv).astype(BF16)
    y = jnp.dot(a, w_ref[...], preferred_element_type=F32)
    o_ref[...] = _layer_norm(DEEPNORM_ALPHA * x_ref[...] + y, g_ref[...], beta_ref[...])


def _conv_out_ln(gate_b, u, um1, um2, conv_w, w_out_bf, x, g, beta, tm):
    n = x.shape[0]
    row = lambda i: (i, 0)
    fixed = lambda i: (0, 0)
    tile = pl.BlockSpec((tm, D_MODEL), row)
    vec = pl.BlockSpec((1, D_MODEL), fixed)
    return pl.pallas_call(
        _conv_out_ln_kernel,
        grid=(n // tm,),
        in_specs=[tile, tile, tile, tile,
                  pl.BlockSpec((CONV_WIDTH, D_MODEL), fixed),
                  pl.BlockSpec((D_MODEL, D_MODEL), fixed),
                  tile, vec, vec],
        out_specs=tile,
        out_shape=jax.ShapeDtypeStruct((n, D_MODEL), F32),
        compiler_params=_cparams("parallel"),
        name="conv_out_ln",
    )(gate_b, u, um1, um2, conv_w, w_out_bf, x, g.reshape(1, D_MODEL), beta.reshape(1, D_MODEL))


LOG2E = 1.4426950408889634


def _top_values(s, k):
    vals = []
    work = s
    for it in range(k):
        m = jnp.max(work, axis=0, keepdims=True)
        vals.append(m)
        if it + 1 < k:
            work = jnp.where(work == m, -jnp.inf, work)
    return vals


def _peer_score_kernel(x_ref, wq_ref, keys_ref, xt_ref, s1l_ref, s2l_ref, tau_ref):
    x = x_ref[...]
    xt_ref[...] = x.T.astype(BF16)
    q = jnp.dot(x.astype(BF16), wq_ref[...], preferred_element_type=F32).astype(BF16)
    nt = (((1,), (1,)), ((), ()))
    k = PEER_TOPK
    for h in range(PEER_HEADS):
        c0 = h * 2 * PEER_HALF
        s1 = lax.dot_general(keys_ref[h, 0], q[:, c0:c0 + PEER_HALF], nt, preferred_element_type=F32)
        s2 = lax.dot_general(keys_ref[h, 1], q[:, c0 + PEER_HALF:c0 + 2 * PEER_HALF], nt,
                             preferred_element_type=F32)
        a = _top_values(s1, k + 1)
        b = _top_values(s2, k + 1)
        b16 = jnp.concatenate(b[:k], axis=0)
        cand = [a[i] + b16 for i in range(k)]
        work = cand
        tops = []
        for it in range(k + 1):
            m = jnp.max(functools.reduce(jnp.maximum, work), axis=0, keepdims=True)
            tops.append(m)
            if it < k:
                work = [jnp.where(w == m, -jnp.inf, w) for w in work]
        c17 = jnp.maximum(tops[k], jnp.maximum(a[0] + b[k], a[k] + b[0]))
        thr = 0.5 * (tops[k - 1] + c17)
        eb16 = jnp.exp(b16 - b[0])
        z = jnp.zeros_like(thr)
        for i in range(k):
            w_i = jnp.exp(a[i] - a[0]) * eb16
            z = z + jnp.sum(jnp.where(cand[i] >= thr, w_i, 0.0), axis=0, keepdims=True)
        shift = a[0] + b[0] + jnp.log(z)
        s1l_ref[h] = (s1 - shift) * LOG2E
        s2l_ref[h] = s2 * LOG2E
        tau_ref[h:h + 1, :] = (thr - shift) * LOG2E


def _peer_scores(x, wq_bf, keys_bf, tm):
    n = x.shape[0]
    hk = (PEER_HEADS, PEER_KEYS, tm)
    hk_map = lambda i: (0, 0, i)
    hk_shape = jax.ShapeDtypeStruct((PEER_HEADS, PEER_KEYS, n), F32)
    return pl.pallas_call(
        _peer_score_kernel,
        grid=(n // tm,),
        in_specs=[
            pl.BlockSpec((tm, D_MODEL), lambda i: (i, 0)),
            pl.BlockSpec((D_MODEL, PEER_HEADS * 2 * PEER_HALF), lambda i: (0, 0)),
            pl.BlockSpec((PEER_HEADS, 2, PEER_KEYS, PEER_HALF), lambda i: (0, 0, 0, 0)),
        ],
        out_specs=[
            pl.BlockSpec((D_MODEL, tm), lambda i: (0, i)),
            pl.BlockSpec(hk, hk_map), pl.BlockSpec(hk, hk_map),
            pl.BlockSpec((PEER_HEADS, tm), lambda i: (0, i)),
        ],
        out_shape=[
            jax.ShapeDtypeStruct((D_MODEL, n), BF16),
            hk_shape, hk_shape,
            jax.ShapeDtypeStruct((PEER_HEADS, n), F32),
        ],
        compiler_params=_cparams("parallel"),
        name="peer_scores",
    )(x, wq_bf, keys_bf)


PEER_TE = 256
I1_PER_TILE = PEER_TE // PEER_KEYS


def _gelu_exact(x):
    return 0.5 * x * (1.0 + lax.erf(x * (2.0 ** -0.5)))


N_PAIRS = PEER_EXPERTS // (2 * PEER_TE)


def _peer_dense_kernel(xt_ref, u_ref, v_ref, s1l_ref, s2l_ref, tau_ref, o_ref, hid0, hid1, at0, at1):
    p = pl.program_id(1)
    tm = xt_ref.shape[1]
    hid = (hid0, hid1)
    at = (at0, at1)

    @pl.when(p == 0)
    def _():
        o_ref[...] = jnp.zeros_like(o_ref)
        for buf in (hid0, hid1, at0, at1):
            buf[...] = jnp.zeros_like(buf)

    for half in range(2):
        rows = slice(half * PEER_TE, (half + 1) * PEER_TE)
        hid[half][...] = jnp.dot(u_ref[rows, :].astype(BF16), xt_ref[...], preferred_element_type=F32)

        src, dst = hid[1 - half], at[1 - half]
        for il in range(I1_PER_TILE):
            r = half * I1_PER_TILE + il
            rs = slice(il * PEER_KEYS, (il + 1) * PEER_KEYS)
            for c in range(tm // LANES):
                cs = slice(c * LANES, (c + 1) * LANES)
                gate = jnp.zeros((PEER_KEYS, LANES), F32)
                for h in range(PEER_HEADS):
                    pair = s2l_ref[h, :, cs] + s1l_ref[r, h:h + 1, cs]
                    gate = gate + jnp.where(pair >= tau_ref[h:h + 1, cs], jnp.exp2(pair), 0.0)
                dst[rs, cs] = (_gelu_exact(src[rs, cs]) * gate).astype(BF16)

        o_ref[...] += lax.dot_general(at[half][...], v_ref[rows, :].astype(BF16), (((0,), (0,)), ((), ())),
                                      preferred_element_type=F32)


def _peer_dense(xt, u_all, v_all, layer, s1lr_pad, s2l, tau, tm):
    n = xt.shape[1]
    row_spec = pl.BlockSpec((2 * I1_PER_TILE, PEER_HEADS, tm), lambda t, p: (p, 0, t))
    key_spec = pl.BlockSpec((PEER_HEADS, PEER_KEYS, tm), lambda t, p: (0, 0, t),
                            pipeline_mode=pl.Buffered(1))
    u_spec = pl.BlockSpec((None, 2 * PEER_TE, D_MODEL), lambda t, p: (layer, jnp.minimum(p, N_PAIRS - 1), 0))
    v_spec = pl.BlockSpec((None, 2 * PEER_TE, D_MODEL), lambda t, p: (layer, jnp.maximum(p - 1, 0), 0))
    return pl.pallas_call(
        _peer_dense_kernel,
        grid=(n // tm, N_PAIRS + 1),
        in_specs=[
            pl.BlockSpec((D_MODEL, tm), lambda t, p: (0, t)),
            u_spec, v_spec,
            row_spec, key_spec,
            pl.BlockSpec((PEER_HEADS, tm), lambda t, p: (0, t)),
        ],
        out_specs=pl.BlockSpec((tm, D_MODEL), lambda t, p: (t, 0)),
        out_shape=jax.ShapeDtypeStruct((n, D_MODEL), F32),
        scratch_shapes=[pltpu.VMEM((PEER_TE, tm), F32), pltpu.VMEM((PEER_TE, tm), F32),
                        pltpu.VMEM((PEER_TE, tm), BF16), pltpu.VMEM((PEER_TE, tm), BF16)],
        compiler_params=_cparams("parallel", "arbitrary"),
        name="peer_dense",
    )(xt, u_all, v_all, s1lr_pad, s2l, tau)


def _res_ln_kernel(x_ref, f_ref, g_ref, beta_ref, o_ref):
    o_ref[...] = _layer_norm(DEEPNORM_ALPHA * x_ref[...] + f_ref[...], g_ref[...], beta_ref[...])


def _res_ln(x, f, g, beta, tm):
    n = x.shape[0]
    tile = pl.BlockSpec((tm, D_MODEL), lambda i: (i, 0))
    vec = pl.BlockSpec((1, D_MODEL), lambda i: (0, 0))
    return pl.pallas_call(
        _res_ln_kernel,
        grid=(n // tm,),
        in_specs=[tile, tile, vec, vec],
        out_specs=tile,
        out_shape=jax.ShapeDtypeStruct((n, D_MODEL), F32),
        compiler_params=_cparams("parallel"),
        name="res_ln",
    )(x, f, g.reshape(1, D_MODEL), beta.reshape(1, D_MODEL))


def _peer_layer(x, wq_bf, keys_bf, u_all, v_all, layer, g, beta, tm_score, tm_dense, tm_ln):
    xt, s1l, s2l, tau = _peer_scores(x, wq_bf, keys_bf, tm_score)
    s1lr_pad = jnp.pad(jnp.transpose(s1l, (1, 0, 2)), ((I1_PER_TILE, I1_PER_TILE), (0, 0), (0, 0)))
    f = _peer_dense(xt, u_all, v_all, layer, s1lr_pad, s2l, tau, tm_dense)
    return _res_ln(x, f, g, beta, tm_ln)


def _rope_tables(pos):
    half = HEAD_DIM // 2
    inv = jnp.power(ROPE_THETA, -jnp.arange(half, dtype=F32) * 2.0 / HEAD_DIM)
    ang = pos.astype(F32)[:, None] * inv[None, :]
    cos = jnp.cos(ang)
    sin = jnp.sin(ang)
    reps = LANES // HEAD_DIM
    return (jnp.tile(jnp.concatenate([cos, cos], axis=1), (1, reps)),
            jnp.tile(jnp.concatenate([-sin, sin], axis=1), (1, reps)))


def _shifted(state, u3):
    t = u3.shape[1]
    up = jnp.concatenate([state, u3], axis=1)
    return up[:, 1:t + 1], up[:, 0:t], up[:, -(CONV_WIDTH - 1):]


def kernel(x_prompt, x_sample, cache_k, cache_v, state_conv, w_qkv, b_qkv, w_o, b_o, attn_sinks,
           w_conv_in, conv_w, w_conv_out, w_peer_q, peer_sub_keys, peer_u, peer_v, ln_g, ln_b):
    batch, seq, _ = x_prompt.shape
    dec_batch, dec_seq, _ = x_sample.shape
    n_p = batch * seq
    n_s = dec_batch * dec_seq
    xp = x_prompt.reshape(n_p, D_MODEL)
    xs = x_sample.reshape(n_s, D_MODEL)
    cos_p, sin_p = _rope_tables(jnp.tile(jnp.arange(seq), batch))
    cos_s, sin_s = _rope_tables(jnp.tile(PAST_LEN + jnp.arange(dec_seq), dec_batch))

    new_kp, new_vp, new_cp, new_ks, new_vs, new_cs = [], [], [], [], [], []
    for layer in range(DEPTH):
        j = layer // 2
        g0, b0 = ln_g[layer, 0], ln_b[layer, 0]
        if layer % 2 == 0:
            wqkv = w_qkv[j].astype(BF16)
            wo = w_o[j].astype(BF16)
            sinks = attn_sinks[j].reshape(N_Q_HEADS)
            qkv_p = _qkv_rope(xp, wqkv, b_qkv[j], cos_p, sin_p, 512)
            qkv_s = _qkv_rope(xs, wqkv, b_qkv[j], cos_s, sin_s, n_s)
            o_p = _attn_prompt(qkv_p, sinks, batch, seq)
            qkv_s3 = qkv_s.reshape(dec_batch, dec_seq, QKV_COLS)
            qkv_s8 = jnp.pad(qkv_s3, ((0, 0), (0, SAMPLE_ROWS - dec_seq), (0, 0)))
            ck = cache_k[j].reshape(dec_batch, WINDOW, KV_COLS)
            cv = cache_v[j].reshape(dec_batch, WINDOW, KV_COLS)
            o_s = _attn_sample(qkv_s8, ck, cv, sinks, dec_seq)[:, :dec_seq].reshape(n_s, Q_COLS)
            kv_p = qkv_p.reshape(batch, seq, QKV_COLS)[:, -WINDOW:, Q_COLS:]
            new_kp.append(kv_p[..., :KV_COLS].reshape(batch, WINDOW, N_KV_HEADS, HEAD_DIM))
            new_vp.append(kv_p[..., KV_COLS:].reshape(batch, WINDOW, N_KV_HEADS, HEAD_DIM))
            k_s = jnp.concatenate([ck, qkv_s3[..., Q_COLS:Q_COLS + KV_COLS]], axis=1)[:, -WINDOW:]
            v_s = jnp.concatenate([cv, qkv_s3[..., Q_COLS + KV_COLS:]], axis=1)[:, -WINDOW:]
            new_ks.append(k_s.reshape(dec_batch, WINDOW, N_KV_HEADS, HEAD_DIM))
            new_vs.append(v_s.reshape(dec_batch, WINDOW, N_KV_HEADS, HEAD_DIM))
            xp = _proj_ln(o_p, wo, b_o[j], xp, g0, b0, 512)
            xs = _proj_ln(o_s, wo, b_o[j], xs, g0, b0, n_s)
        else:
            w_in = w_conv_in[j].astype(BF16)
            w_out = w_conv_out[j].astype(BF16)
            gb_p, u_p = _conv_in(xp, w_in, 512)
            gb_s, u_s = _conv_in(xs, w_in, n_s)
            zero_state = jnp.zeros((batch, CONV_WIDTH - 1, D_MODEL), F32)
            um1_p, um2_p, cp = _shifted(zero_state, u_p.reshape(batch, seq, D_MODEL))
            um1_s, um2_s, cs = _shifted(state_conv[j], u_s.reshape(dec_batch, dec_seq, D_MODEL))
            new_cp.append(cp)
            new_cs.append(cs)
            xp = _conv_out_ln(gb_p, u_p, um1_p.reshape(n_p, D_MODEL), um2_p.reshape(n_p, D_MODEL),
                              conv_w[j], w_out, xp, g0, b0, 256)
            xs = _conv_out_ln(gb_s, u_s, um1_s.reshape(n_s, D_MODEL), um2_s.reshape(n_s, D_MODEL),
                              conv_w[j], w_out, xs, g0, b0, n_s)
        wq = w_peer_q[layer].astype(BF16)
        keys = peer_sub_keys[layer].astype(BF16)
        g1, b1 = ln_g[layer, 1], ln_b[layer, 1]
        xp = _peer_layer(xp, wq, keys, peer_u, peer_v, layer, g1, b1, 256, 1024, 512)
        xs = _peer_layer(xs, wq, keys, peer_u, peer_v, layer, g1, b1, n_s, n_s, n_s)

    return (xp.reshape(batch, seq, D_MODEL), xs.reshape(dec_batch, dec_seq, D_MODEL),
            jnp.stack(new_kp), jnp.stack(new_vp), jnp.stack(new_cp),
            jnp.stack(new_ks), jnp.stack(new_vs), jnp.stack(new_cs))
```

```python
import functools

import jax
import jax.numpy as jnp
from jax import lax
from jax.experimental import pallas as pl
from jax.experimental.pallas import tpu as pltpu

F32 = jnp.float32
BF16 = jnp.bfloat16

D_MODEL = 2048
DEPTH = 4
HEAD_DIM = 64
N_Q_HEADS = 32
N_KV_HEADS = 8
GQA_GROUP = 4
WINDOW = 128
ROPE_THETA = 10000.0
PAST_LEN = 16384
CONV_WIDTH = 3
PEER_HEADS = 8
PEER_KEYS = 128
PEER_EXPERTS = PEER_KEYS * PEER_KEYS
PEER_HALF = 128
PEER_TOPK = 16
LN_EPS = 1e-5
NEG_INF = -1e30
DEEPNORM_ALPHA = (2 * DEPTH) ** 0.25
LOG2E = 1.4426950408889634

Q_COLS = N_Q_HEADS * HEAD_DIM
KV_COLS = N_KV_HEADS * HEAD_DIM
QKV_COLS = Q_COLS + 2 * KV_COLS

LANES = 128
PEER_TE = 256
I1_PER_TILE = PEER_TE // PEER_KEYS
VMEM_LIMIT_BYTES = 56 * 1024 * 1024


def _cparams(*semantics):
    return pltpu.CompilerParams(dimension_semantics=semantics, vmem_limit_bytes=VMEM_LIMIT_BYTES)


def _layer_norm(z, g, b):
    mu = jnp.mean(z, axis=-1, keepdims=True)
    zc = z - mu
    var = jnp.mean(zc * zc, axis=-1, keepdims=True)
    return zc * lax.rsqrt(var + LN_EPS) * g + b


QKV_TN = 512
N_ROPE_TILES = (Q_COLS + KV_COLS) // QKV_TN


def _qkv_kernel(x_ref, w_ref, b_ref, cos_ref, sin_ref, o_ref):
    j = pl.program_id(1)
    acc = jnp.dot(x_ref[...].astype(BF16), w_ref[...], preferred_element_type=F32) + b_ref[...]

    @pl.when(j < N_ROPE_TILES)
    def _():
        cos = cos_ref[...]
        sin = sin_ref[...]
        lane = lax.broadcasted_iota(jnp.int32, cos.shape, 1)
        first_half = (lane % HEAD_DIM) < (HEAD_DIM // 2)
        for c in range(QKV_TN // LANES):
            seg = acc[:, c * LANES:(c + 1) * LANES]
            partner = jnp.where(first_half,
                                pltpu.roll(seg, LANES - HEAD_DIM // 2, 1),
                                pltpu.roll(seg, HEAD_DIM // 2, 1))
            o_ref[:, c * LANES:(c + 1) * LANES] = seg * cos + partner * sin

    @pl.when(j >= N_ROPE_TILES)
    def _():
        o_ref[...] = acc


def _qkv_rope(x, w_bf, b, cos, sin, tm):
    n = x.shape[0]
    return pl.pallas_call(
        _qkv_kernel,
        grid=(n // tm, QKV_COLS // QKV_TN),
        in_specs=[
            pl.BlockSpec((tm, D_MODEL), lambda i, j: (i, 0)),
            pl.BlockSpec((D_MODEL, QKV_TN), lambda i, j: (0, j)),
            pl.BlockSpec((1, QKV_TN), lambda i, j: (0, j)),
            pl.BlockSpec((tm, LANES), lambda i, j: (i, 0)),
            pl.BlockSpec((tm, LANES), lambda i, j: (i, 0)),
        ],
        out_specs=pl.BlockSpec((tm, QKV_TN), lambda i, j: (i, j)),
        out_shape=jax.ShapeDtypeStruct((n, QKV_COLS), F32),
        compiler_params=_cparams("parallel", "arbitrary"),
        name="qkv_rope",
    )(x, w_bf, b.reshape(1, QKV_COLS), cos, sin)


def _sink_column(sink_ref, g, rows):
    return jnp.concatenate(
        [jnp.full((rows, 1), sink_ref[g * GQA_GROUP + j], F32) for j in range(GQA_GROUP)], axis=0)


def _attn_prompt_kernel(sink_ref, q_ref, kp_ref, kc_ref, vp_ref, vc_ref, o_ref):
    i = pl.program_id(1)
    cols = GQA_GROUP * WINDOW
    kj = lax.broadcasted_iota(jnp.int32, (2 * WINDOW, cols), 0)
    qi = lax.broadcasted_iota(jnp.int32, (2 * WINDOW, cols), 1) % WINDOW
    dist = qi + WINDOW - kj
    prev_ok = jnp.where(i > 0, 0, WINDOW)
    mask = (dist >= 0) & (dist <= WINDOW) & (kj >= prev_ok)
    nt = (((1,), (1,)), ((), ()))
    tn = (((0,), (0,)), ((), ()))
    for g in range(N_KV_HEADS):
        ks = slice(g * HEAD_DIM, (g + 1) * HEAD_DIM)
        k = jnp.concatenate([kp_ref[:, ks], kc_ref[:, ks]], axis=0).astype(BF16)
        v = jnp.concatenate([vp_ref[:, ks], vc_ref[:, ks]], axis=0).astype(BF16)
        q = jnp.concatenate(
            [q_ref[:, (g * GQA_GROUP + j) * HEAD_DIM:(g * GQA_GROUP + j + 1) * HEAD_DIM]
             for j in range(GQA_GROUP)], axis=0).astype(BF16)
        t = lax.dot_general(k, q, nt, preferred_element_type=F32)
        t = jnp.where(mask, t * (HEAD_DIM ** -0.5 * LOG2E), NEG_INF)
        sink = jnp.concatenate(
            [jnp.full((1, WINDOW), sink_ref[g * GQA_GROUP + j] * LOG2E, F32) for j in range(GQA_GROUP)], axis=1)
        m = jnp.maximum(jnp.max(t, axis=0, keepdims=True), sink)
        p = jnp.exp2(t - m)
        denom = jnp.sum(p, axis=0, keepdims=True) + jnp.exp2(sink - m)
        o_t = lax.dot_general(v, p.astype(BF16), tn, preferred_element_type=F32) / denom
        o = o_t.T
        o_ref[:, g * GQA_GROUP * HEAD_DIM:(g + 1) * GQA_GROUP * HEAD_DIM] = jnp.concatenate(
            [o[j * WINDOW:(j + 1) * WINDOW] for j in range(GQA_GROUP)], axis=1).astype(o_ref.dtype)


def _attn_prompt(qkv, sinks, batch, seq):
    nb = seq // WINDOW
    kcol = Q_COLS // KV_COLS
    vcol = kcol + 1

    def cur(b, i):
        return b * nb + i

    def prev(b, i):
        return b * nb + jnp.maximum(i - 1, 0)

    return pl.pallas_call(
        _attn_prompt_kernel,
        grid=(batch, nb),
        in_specs=[
            pl.BlockSpec(memory_space=pltpu.SMEM),
            pl.BlockSpec((WINDOW, Q_COLS), lambda b, i: (cur(b, i), 0)),
            pl.BlockSpec((WINDOW, KV_COLS), lambda b, i: (prev(b, i), kcol)),
            pl.BlockSpec((WINDOW, KV_COLS), lambda b, i: (cur(b, i), kcol)),
            pl.BlockSpec((WINDOW, KV_COLS), lambda b, i: (prev(b, i), vcol)),
            pl.BlockSpec((WINDOW, KV_COLS), lambda b, i: (cur(b, i), vcol)),
        ],
        out_specs=pl.BlockSpec((WINDOW, Q_COLS), lambda b, i: (cur(b, i), 0)),
        out_shape=jax.ShapeDtypeStruct((batch * seq, Q_COLS), BF16),
        compiler_params=_cparams("parallel", "arbitrary"),
        name="attn_prompt",
    )(sinks, qkv, qkv, qkv, qkv, qkv)


SAMPLE_ROWS = 8


def _attn_sample_kernel(sink_ref, qkv_ref, ck_ref, cv_ref, o_ref, *, n_new):
    qkv = qkv_ref[0]
    rows = GQA_GROUP * SAMPLE_ROWS
    tok = lax.broadcasted_iota(jnp.int32, (rows, WINDOW), 0) % SAMPLE_ROWS
    kj = lax.broadcasted_iota(jnp.int32, (rows, WINDOW), 1)
    mask_c = kj >= tok
    tok1 = lax.broadcasted_iota(jnp.int32, (rows, 1), 0) % SAMPLE_ROWS
    for g in range(N_KV_HEADS):
        ks = slice(g * HEAD_DIM, (g + 1) * HEAD_DIM)
        kc = ck_ref[0, :, ks].astype(BF16)
        vc = cv_ref[0, :, ks].astype(BF16)
        kn = qkv[:, Q_COLS + g * HEAD_DIM:Q_COLS + (g + 1) * HEAD_DIM]
        vn = qkv[:, Q_COLS + KV_COLS + g * HEAD_DIM:Q_COLS + KV_COLS + (g + 1) * HEAD_DIM]
        q = jnp.concatenate(
            [qkv[:, (g * GQA_GROUP + j) * HEAD_DIM:(g * GQA_GROUP + j + 1) * HEAD_DIM]
             for j in range(GQA_GROUP)], axis=0)
        s_c = lax.dot_general(q.astype(BF16), kc, (((1,), (1,)), ((), ())), preferred_element_type=F32)
        s_c = jnp.where(mask_c, s_c * (HEAD_DIM ** -0.5), NEG_INF)
        s_n = []
        for r in range(n_new):
            s_r = jnp.sum(q * kn[r:r + 1, :], axis=1, keepdims=True) * (HEAD_DIM ** -0.5)
            s_n.append(jnp.where(tok1 >= r, s_r, NEG_INF))
        sink = _sink_column(sink_ref, g, SAMPLE_ROWS)
        m = jnp.maximum(jnp.max(s_c, axis=1, keepdims=True), sink)
        for s_r in s_n:
            m = jnp.maximum(m, s_r)
        p_c = jnp.exp(s_c - m)
        denom = jnp.sum(p_c, axis=1, keepdims=True) + jnp.exp(sink - m)
        o = jnp.dot(p_c.astype(BF16), vc, preferred_element_type=F32)
        for r, s_r in enumerate(s_n):
            p_r = jnp.exp(s_r - m)
            denom = denom + p_r
            o = o + p_r * vn[r:r + 1, :]
        o = o / denom
        o_ref[0, :, g * GQA_GROUP * HEAD_DIM:(g + 1) * GQA_GROUP * HEAD_DIM] = jnp.concatenate(
            [o[j * SAMPLE_ROWS:(j + 1) * SAMPLE_ROWS] for j in range(GQA_GROUP)], axis=1)


def _attn_sample(qkv8, ck, cv, sinks, n_new):
    nb = qkv8.shape[0]
    return pl.pallas_call(
        functools.partial(_attn_sample_kernel, n_new=n_new),
        grid=(nb,),
        in_specs=[
            pl.BlockSpec(memory_space=pltpu.SMEM),
            pl.BlockSpec((1, SAMPLE_ROWS, QKV_COLS), lambda b: (b, 0, 0)),
            pl.BlockSpec((1, WINDOW, KV_COLS), lambda b: (b, 0, 0)),
            pl.BlockSpec((1, WINDOW, KV_COLS), lambda b: (b, 0, 0)),
        ],
        out_specs=pl.BlockSpec((1, SAMPLE_ROWS, Q_COLS), lambda b: (b, 0, 0)),
        out_shape=jax.ShapeDtypeStruct((nb, SAMPLE_ROWS, Q_COLS), F32),
        compiler_params=_cparams("parallel"),
        name="attn_sample",
    )(sinks, qkv8, ck, cv)


def _proj_ln_kernel(a_ref, w_ref, b_ref, x_ref, g_ref, beta_ref, o_ref):
    y = jnp.dot(a_ref[...].astype(BF16), w_ref[...], preferred_element_type=F32) + b_ref[...]
    o_ref[...] = _layer_norm(DEEPNORM_ALPHA * x_ref[...] + y, g_ref[...], beta_ref[...])


def _proj_ln(a, w_bf, bias, x, g, beta, tm):
    n, k = a.shape
    row = lambda i: (i, 0)
    fixed = lambda i: (0, 0)
    return pl.pallas_call(
        _proj_ln_kernel,
        grid=(n // tm,),
        in_specs=[
            pl.BlockSpec((tm, k), row),
            pl.BlockSpec((k, D_MODEL), fixed),
            pl.BlockSpec((1, D_MODEL), fixed),
            pl.BlockSpec((tm, D_MODEL), row),
            pl.BlockSpec((1, D_MODEL), fixed),
            pl.BlockSpec((1, D_MODEL), fixed),
        ],
        out_specs=pl.BlockSpec((tm, D_MODEL), row),
        out_shape=jax.ShapeDtypeStruct((n, D_MODEL), F32),
        compiler_params=_cparams("parallel"),
        name="proj_ln",
    )(a, w_bf, bias.reshape(1, D_MODEL), x, g.reshape(1, D_MODEL), beta.reshape(1, D_MODEL))


CONV_TN = 512


def _conv_in_kernel(x_ref, wb_ref, wc_ref, wh_ref, gb_ref, u_ref):
    xb = x_ref[...].astype(BF16)
    gb_ref[...] = jnp.dot(xb, wb_ref[...], preferred_element_type=F32)
    gate_c = jnp.dot(xb, wc_ref[...], preferred_element_type=F32)
    h = jnp.dot(xb, wh_ref[...], preferred_element_type=F32)
    u_ref[...] = gate_c * h


def _conv_in(x, w_in_bf, tm):
    n = x.shape[0]
    nt = D_MODEL // CONV_TN
    return pl.pallas_call(
        _conv_in_kernel,
        grid=(n // tm, nt),
        in_specs=[
            pl.BlockSpec((tm, D_MODEL), lambda i, j: (i, 0)),
            pl.BlockSpec((D_MODEL, CONV_TN), lambda i, j: (0, j)),
            pl.BlockSpec((D_MODEL, CONV_TN), lambda i, j: (0, nt + j)),
            pl.BlockSpec((D_MODEL, CONV_TN), lambda i, j: (0, 2 * nt + j)),
        ],
        out_specs=[
            pl.BlockSpec((tm, CONV_TN), lambda i, j: (i, j)),
            pl.BlockSpec((tm, CONV_TN), lambda i, j: (i, j)),
        ],
        out_shape=[jax.ShapeDtypeStruct((n, D_MODEL), F32)] * 2,
        compiler_params=_cparams("parallel", "arbitrary"),
        name="conv_in",
    )(x, w_in_bf, w_in_bf, w_in_bf)


def _conv_out_ln_kernel(gb_ref, u_ref, um1_ref, um2_ref, cw_ref, w_ref, x_ref, g_ref, beta_ref, o_ref):
    conv = um2_ref[...] * cw_ref[0:1, :]
    conv = conv + um1_ref[...] * cw_ref[1:2, :]
    conv = conv + u_ref[...] * cw_ref[2:3, :]
    a = (gb_ref[...] * conv).astype(BF16)
    y = jnp.dot(a, w_ref[...], preferred_element_type=F32)
    o_ref[...] = _layer_norm(DEEPNORM_ALPHA * x_ref[...] + y, g_ref[...], beta_ref[...])


def _conv_out_ln(gate_b, u, um1, um2, conv_w, w_out_bf, x, g, beta, tm):
    n = x.shape[0]
    row = lambda i: (i, 0)
    fixed = lambda i: (0, 0)
    tile = pl.BlockSpec((tm, D_MODEL), row)
    vec = pl.BlockSpec((1, D_MODEL), fixed)
    return pl.pallas_call(
        _conv_out_ln_kernel,
        grid=(n // tm,),
        in_specs=[tile, tile, tile, tile,
                  pl.BlockSpec((CONV_WIDTH, D_MODEL), fixed),
                  pl.BlockSpec((D_MODEL, D_MODEL), fixed),
                  tile, vec, vec],
        out_specs=tile,
        out_shape=jax.ShapeDtypeStruct((n, D_MODEL), F32),
        compiler_params=_cparams("parallel"),
        name="conv_out_ln",
    )(gate_b, u, um1, um2, conv_w, w_out_bf, x, g.reshape(1, D_MODEL), beta.reshape(1, D_MODEL))


def _oddeven_merge_sort_pairs(n):
    pairs = []
    p = 1
    while p < n:
        k = p
        while k >= 1:
            for j in range(k % p, n - k, 2 * k):
                for i in range(min(k, n - j - k)):
                    if (i + j) // (2 * p) == (i + j + k) // (2 * p):
                        pairs.append((i + j, i + j + k))
            k //= 2
        p *= 2
    return pairs


def _top_values(pieces, k):
    n = len(pieces)
    size = 1
    while size < n:
        size *= 2
    v = list(pieces) + [jnp.full_like(pieces[0], -jnp.inf)] * (size - n)
    for i, j in _oddeven_merge_sort_pairs(size):
        v[i], v[j] = jnp.maximum(v[i], v[j]), jnp.minimum(v[i], v[j])
    v = v[:n]
    vals = []
    for it in range(k):
        m = jnp.max(v[0], axis=0, keepdims=True)
        vals.append(m)
        need = k - it - 1
        if need:
            hit = v[0] == m
            nxt = v[1:] + [jnp.full_like(v[0], -jnp.inf)]
            v = [jnp.where(hit, nxt[j], v[j]) for j in range(min(need, len(v)))]
    return vals


def _peer_score_kernel(x_ref, wq_ref, keys_ref, xt_ref, s1l_ref, s2l_ref, tau_ref):
    pad = jnp.zeros((I1_PER_TILE,) + s1l_ref.shape[1:], F32)
    s1l_ref[:I1_PER_TILE] = pad
    s1l_ref[I1_PER_TILE + PEER_KEYS:] = pad
    x = x_ref[...]
    xt_ref[...] = x.T.astype(BF16)
    q = jnp.dot(x.astype(BF16), wq_ref[...], preferred_element_type=F32).astype(BF16)
    nt = (((1,), (1,)), ((), ()))
    k = PEER_TOPK
    for h in range(PEER_HEADS):
        c0 = h * 2 * PEER_HALF
        s1 = lax.dot_general(keys_ref[h, 0], q[:, c0:c0 + PEER_HALF], nt, preferred_element_type=F32)
        s2 = lax.dot_general(keys_ref[h, 1], q[:, c0 + PEER_HALF:c0 + 2 * PEER_HALF], nt,
                             preferred_element_type=F32)
        a = _top_values([s1[r:r + 8] for r in range(0, PEER_KEYS, 8)], k + 1)
        b = _top_values([s2[r:r + 8] for r in range(0, PEER_KEYS, 8)], k + 1)
        neg = jnp.full_like(a[0], -jnp.inf)
        b24 = jnp.concatenate(b + [neg] * 7, axis=0)
        b8 = jnp.concatenate(b[:8], axis=0)
        a16 = jnp.concatenate(a[8:] + [neg] * 7, axis=0)
        row = lax.broadcasted_iota(jnp.int32, b8.shape, 0)
        eb24 = jnp.exp(b24 - b[0])
        eb8 = jnp.exp(b8 - b[0])
        cand = [a[0] + b24[r:r + 8] for r in (0, 8, 16)]
        wgt = [eb24[r:r + 8] for r in (0, 8, 16)]
        for i in range(1, 8):
            cand.append(jnp.where(row < (k + 1) // (i + 1), a[i] + b8, -jnp.inf))
            wgt.append(jnp.exp(a[i] - a[0]) * eb8)
        cand += [a16[r:r + 8] + b[0] for r in (0, 8)]
        wgt += [jnp.exp(a16[r:r + 8] - a[0]) for r in (0, 8)]
        tops = _top_values(cand, k + 1)
        thr = 0.5 * (tops[k - 1] + tops[k])
        z8 = functools.reduce(jnp.add, [jnp.where(c >= thr, w, 0.0) for c, w in zip(cand, wgt)])
        z = jnp.sum(z8, axis=0, keepdims=True)
        shift = a[0] + b[0] + jnp.log(z)
        s1l_ref[I1_PER_TILE:I1_PER_TILE + PEER_KEYS, h, :] = (s1 - shift) * LOG2E
        s2l_ref[h] = s2 * LOG2E
        tau_ref[h:h + 1, :] = (thr - shift) * LOG2E


def _peer_scores(x, wq_bf, keys_bf, tm):
    n = x.shape[0]
    hk = (PEER_HEADS, PEER_KEYS, tm)
    hk_map = lambda i: (0, 0, i)
    hk_shape = jax.ShapeDtypeStruct((PEER_HEADS, PEER_KEYS, n), F32)
    return pl.pallas_call(
        _peer_score_kernel,
        grid=(n // tm,),
        in_specs=[
            pl.BlockSpec((tm, D_MODEL), lambda i: (i, 0)),
            pl.BlockSpec((D_MODEL, PEER_HEADS * 2 * PEER_HALF), lambda i: (0, 0)),
            pl.BlockSpec((PEER_HEADS, 2, PEER_KEYS, PEER_HALF), lambda i: (0, 0, 0, 0)),
        ],
        out_specs=[
            pl.BlockSpec((D_MODEL, tm), lambda i: (0, i)),
            pl.BlockSpec((PEER_KEYS + 2 * I1_PER_TILE, PEER_HEADS, tm), hk_map), pl.BlockSpec(hk, hk_map),
            pl.BlockSpec((PEER_HEADS, tm), lambda i: (0, i)),
        ],
        out_shape=[
            jax.ShapeDtypeStruct((D_MODEL, n), BF16),
            jax.ShapeDtypeStruct((PEER_KEYS + 2 * I1_PER_TILE, PEER_HEADS, n), F32), hk_shape,
            jax.ShapeDtypeStruct((PEER_HEADS, n), F32),
        ],
        compiler_params=_cparams("parallel"),
        name="peer_scores",
    )(x, wq_bf, keys_bf)


def _gelu_exact(x):
    return 0.5 * x * (1.0 + lax.erf(x * (2.0 ** -0.5)))


N_PAIRS = PEER_EXPERTS // (2 * PEER_TE)


def _peer_dense_kernel(xt_ref, u_ref, v_ref, s1l_ref, s2l_ref, tau_ref, o_ref, hid0, hid1, at0, at1):
    p = pl.program_id(1)
    tm = xt_ref.shape[1]
    hid = (hid0, hid1)
    at = (at0, at1)

    @pl.when(p == 0)
    def _():
        o_ref[...] = jnp.zeros_like(o_ref)
        for buf in (hid0, hid1, at0, at1):
            buf[...] = jnp.zeros_like(buf)

    for half in range(2):
        rows = slice(half * PEER_TE, (half + 1) * PEER_TE)
        hid[half][...] = jnp.dot(u_ref[rows, :].astype(BF16), xt_ref[...], preferred_element_type=F32)

        src, dst = hid[1 - half], at[1 - half]
        for il in range(I1_PER_TILE):
            r = half * I1_PER_TILE + il
            rs = slice(il * PEER_KEYS, (il + 1) * PEER_KEYS)
            for c in range(tm // LANES):
                cs = slice(c * LANES, (c + 1) * LANES)
                gate = jnp.zeros((PEER_KEYS, LANES), F32)
                for h in range(PEER_HEADS):
                    pair = s2l_ref[h, :, cs] + s1l_ref[r, h:h + 1, cs]
                    gate = gate + jnp.where(pair >= tau_ref[h:h + 1, cs], jnp.exp2(pair), 0.0)
                dst[rs, cs] = (_gelu_exact(src[rs, cs]) * gate).astype(BF16)

        o_ref[...] += lax.dot_general(at[half][...], v_ref[rows, :].astype(BF16), (((0,), (0,)), ((), ())),
                                      preferred_element_type=F32)


def _peer_dense(xt, u_all, v_all, layer, s1lr_pad, s2l, tau, tm):
    n = xt.shape[1]
    row_spec = pl.BlockSpec((2 * I1_PER_TILE, PEER_HEADS, tm), lambda t, p: (p, 0, t))
    key_spec = pl.BlockSpec((PEER_HEADS, PEER_KEYS, tm), lambda t, p: (0, 0, t),
                            pipeline_mode=pl.Buffered(1))
    u_spec = pl.BlockSpec((None, 2 * PEER_TE, D_MODEL), lambda t, p: (layer, jnp.minimum(p, N_PAIRS - 1), 0))
    v_spec = pl.BlockSpec((None, 2 * PEER_TE, D_MODEL), lambda t, p: (layer, jnp.maximum(p - 1, 0), 0))
    return pl.pallas_call(
        _peer_dense_kernel,
        grid=(n // tm, N_PAIRS + 1),
        in_specs=[
            pl.BlockSpec((D_MODEL, tm), lambda t, p: (0, t)),
            u_spec, v_spec,
            row_spec, key_spec,
            pl.BlockSpec((PEER_HEADS, tm), lambda t, p: (0, t)),
        ],
        out_specs=pl.BlockSpec((tm, D_MODEL), lambda t, p: (t, 0)),
        out_shape=jax.ShapeDtypeStruct((n, D_MODEL), F32),
        scratch_shapes=[pltpu.VMEM((PEER_TE, tm), F32), pltpu.VMEM((PEER_TE, tm), F32),
                        pltpu.VMEM((PEER_TE, tm), BF16), pltpu.VMEM((PEER_TE, tm), BF16)],
        compiler_params=_cparams("parallel", "arbitrary"),
        name="peer_dense",
    )(xt, u_all, v_all, s1lr_pad, s2l, tau)


def _res_ln_kernel(x_ref, f_ref, g_ref, beta_ref, o_ref):
    o_ref[...] = _layer_norm(DEEPNORM_ALPHA * x_ref[...] + f_ref[...], g_ref[...], beta_ref[...])


def _res_ln(x, f, g, beta, tm):
    n = x.shape[0]
    tile = pl.BlockSpec((tm, D_MODEL), lambda i: (i, 0))
    vec = pl.BlockSpec((1, D_MODEL), lambda i: (0, 0))
    return pl.pallas_call(
        _res_ln_kernel,
        grid=(n // tm,),
        in_specs=[tile, tile, vec, vec],
        out_specs=tile,
        out_shape=jax.ShapeDtypeStruct((n, D_MODEL), F32),
        compiler_params=_cparams("parallel"),
        name="res_ln",
    )(x, f, g.reshape(1, D_MODEL), beta.reshape(1, D_MODEL))


def _peer_layer(x, wq_bf, keys_bf, u_all, v_all, layer, g, beta, tm_score, tm_dense, tm_ln):
    xt, s1lr_pad, s2l, tau = _peer_scores(x, wq_bf, keys_bf, tm_score)
    f = _peer_dense(xt, u_all, v_all, layer, s1lr_pad, s2l, tau, tm_dense)
    return _res_ln(x, f, g, beta, tm_ln)


def _rope_tables(pos):
    half = HEAD_DIM // 2
    inv = jnp.power(ROPE_THETA, -jnp.arange(half, dtype=F32) * 2.0 / HEAD_DIM)
    ang = pos.astype(F32)[:, None] * inv[None, :]
    cos = jnp.cos(ang)
    sin = jnp.sin(ang)
    reps = LANES // HEAD_DIM
    return (jnp.tile(jnp.concatenate([cos, cos], axis=1), (1, reps)),
            jnp.tile(jnp.concatenate([-sin, sin], axis=1), (1, reps)))


def _shifted(state, u3):
    t = u3.shape[1]
    up = jnp.concatenate([state, u3], axis=1)
    return up[:, 1:t + 1], up[:, 0:t], up[:, -(CONV_WIDTH - 1):]


def kernel(x_prompt, x_sample, cache_k, cache_v, state_conv, w_qkv, b_qkv, w_o, b_o, attn_sinks,
           w_conv_in, conv_w, w_conv_out, w_peer_q, peer_sub_keys, peer_u, peer_v, ln_g, ln_b):
    batch, seq, _ = x_prompt.shape
    dec_batch, dec_seq, _ = x_sample.shape
    n_p = batch * seq
    n_s = dec_batch * dec_seq
    xp = x_prompt.reshape(n_p, D_MODEL)
    xs = x_sample.reshape(n_s, D_MODEL)
    cos_p, sin_p = _rope_tables(jnp.tile(jnp.arange(seq), batch))
    cos_s, sin_s = _rope_tables(jnp.tile(PAST_LEN + jnp.arange(dec_seq), dec_batch))

    new_kp, new_vp, new_cp, new_ks, new_vs, new_cs = [], [], [], [], [], []
    for layer in range(DEPTH):
        j = layer // 2
        g0, b0 = ln_g[layer, 0], ln_b[layer, 0]
        if layer % 2 == 0:
            wqkv = w_qkv[j].astype(BF16)
            wo = w_o[j].astype(BF16)
            sinks = attn_sinks[j].reshape(N_Q_HEADS)
            qkv_p = _qkv_rope(xp, wqkv, b_qkv[j], cos_p, sin_p, 512)
            qkv_s = _qkv_rope(xs, wqkv, b_qkv[j], cos_s, sin_s, n_s)
            o_p = _attn_prompt(qkv_p, sinks, batch, seq)
            qkv_s3 = qkv_s.reshape(dec_batch, dec_seq, QKV_COLS)
            qkv_s8 = jnp.pad(qkv_s3, ((0, 0), (0, SAMPLE_ROWS - dec_seq), (0, 0)))
            ck = cache_k[j].reshape(dec_batch, WINDOW, KV_COLS)
            cv = cache_v[j].reshape(dec_batch, WINDOW, KV_COLS)
            o_s = _attn_sample(qkv_s8, ck, cv, sinks, dec_seq)[:, :dec_seq].reshape(n_s, Q_COLS)
            kv_p = qkv_p.reshape(batch, seq, QKV_COLS)[:, -WINDOW:, Q_COLS:]
            new_kp.append(kv_p[..., :KV_COLS].reshape(batch, WINDOW, N_KV_HEADS, HEAD_DIM))
            new_vp.append(kv_p[..., KV_COLS:].reshape(batch, WINDOW, N_KV_HEADS, HEAD_DIM))
            k_s = jnp.concatenate([ck, qkv_s3[..., Q_COLS:Q_COLS + KV_COLS]], axis=1)[:, -WINDOW:]
            v_s = jnp.concatenate([cv, qkv_s3[..., Q_COLS + KV_COLS:]], axis=1)[:, -WINDOW:]
            new_ks.append(k_s.reshape(dec_batch, WINDOW, N_KV_HEADS, HEAD_DIM))
            new_vs.append(v_s.reshape(dec_batch, WINDOW, N_KV_HEADS, HEAD_DIM))
            xp = _proj_ln(o_p, wo, b_o[j], xp, g0, b0, 512)
            xs = _proj_ln(o_s, wo, b_o[j], xs, g0, b0, n_s)
        else:
            w_in = w_conv_in[j].astype(BF16)
            w_out = w_conv_out[j].astype(BF16)
            gb_p, u_p = _conv_in(xp, w_in, 512)
            gb_s, u_s = _conv_in(xs, w_in, n_s)
            zero_state = jnp.zeros((batch, CONV_WIDTH - 1, D_MODEL), F32)
            um1_p, um2_p, cp = _shifted(zero_state, u_p.reshape(batch, seq, D_MODEL))
            um1_s, um2_s, cs = _shifted(state_conv[j], u_s.reshape(dec_batch, dec_seq, D_MODEL))
            new_cp.append(cp)
            new_cs.append(cs)
            xp = _conv_out_ln(gb_p, u_p, um1_p.reshape(n_p, D_MODEL), um2_p.reshape(n_p, D_MODEL),
                              conv_w[j], w_out, xp, g0, b0, 256)
            xs = _conv_out_ln(gb_s, u_s, um1_s.reshape(n_s, D_MODEL), um2_s.reshape(n_s, D_MODEL),
                              conv_w[j], w_out, xs, g0, b0, n_s)
        wq = w_peer_q[layer].astype(BF16)
        keys = peer_sub_keys[layer].astype(BF16)
        g1, b1 = ln_g[layer, 1], ln_b[layer, 1]
        xp = _peer_layer(xp, wq, keys, peer_u, peer_v, layer, g1, b1, 256, 1024, 512)
        xs = _peer_layer(xs, wq, keys, peer_u, peer_v, layer, g1, b1, n_s, n_s, n_s)

    return (xp.reshape(batch, seq, D_MODEL), xs.reshape(dec_batch, dec_seq, D_MODEL),
            jnp.stack(new_kp), jnp.stack(new_vp), jnp.stack(new_cp),
            jnp.stack(new_ks), jnp.stack(new_vs), jnp.stack(new_cs))
```

```python
import functools

import jax
import jax.numpy as jnp
from jax import lax
from jax.experimental import pallas as pl
from jax.experimental.pallas import tpu as pltpu

F32 = jnp.float32
BF16 = jnp.bfloat16

D_MODEL = 2048
DEPTH = 4
HEAD_DIM = 64
N_Q_HEADS = 32
N_KV_HEADS = 8
GQA_GROUP = 4
WINDOW = 128
ROPE_THETA = 10000.0
PAST_LEN = 16384
CONV_WIDTH = 3
PEER_HEADS = 8
PEER_KEYS = 128
PEER_EXPERTS = PEER_KEYS * PEER_KEYS
PEER_HALF = 128
PEER_TOPK = 16
LN_EPS = 1e-5
NEG_INF = -1e30
DEEPNORM_ALPHA = (2 * DEPTH) ** 0.25
LOG2E = 1.4426950408889634

Q_COLS = N_Q_HEADS * HEAD_DIM
KV_COLS = N_KV_HEADS * HEAD_DIM
QKV_COLS = Q_COLS + 2 * KV_COLS

LANES = 128
PEER_TE = 256
I1_PER_TILE = PEER_TE // PEER_KEYS
VMEM_LIMIT_BYTES = 56 * 1024 * 1024


def _cparams(*semantics):
    return pltpu.CompilerParams(dimension_semantics=semantics, vmem_limit_bytes=VMEM_LIMIT_BYTES)


def _layer_norm(z, g, b):
    mu = jnp.mean(z, axis=-1, keepdims=True)
    zc = z - mu
    var = jnp.mean(zc * zc, axis=-1, keepdims=True)
    return zc * lax.rsqrt(var + LN_EPS) * g + b


QKV_TN = 512
N_ROPE_TILES = (Q_COLS + KV_COLS) // QKV_TN


def _qkv_kernel(x_ref, w_ref, b_ref, cos_ref, sin_ref, o_ref):
    j = pl.program_id(1)
    acc = jnp.dot(x_ref[...].astype(BF16), w_ref[...], preferred_element_type=F32) + b_ref[...]

    @pl.when(j < N_ROPE_TILES)
    def _():
        cos = cos_ref[...]
        sin = sin_ref[...]
        lane = lax.broadcasted_iota(jnp.int32, cos.shape, 1)
        first_half = (lane % HEAD_DIM) < (HEAD_DIM // 2)
        for c in range(QKV_TN // LANES):
            seg = acc[:, c * LANES:(c + 1) * LANES]
            partner = jnp.where(first_half,
                                pltpu.roll(seg, LANES - HEAD_DIM // 2, 1),
                                pltpu.roll(seg, HEAD_DIM // 2, 1))
            o_ref[:, c * LANES:(c + 1) * LANES] = seg * cos + partner * sin

    @pl.when(j >= N_ROPE_TILES)
    def _():
        o_ref[...] = acc


def _qkv_rope(x, w_bf, b, cos, sin, tm):
    n = x.shape[0]
    return pl.pallas_call(
        _qkv_kernel,
        grid=(n // tm, QKV_COLS // QKV_TN),
        in_specs=[
            pl.BlockSpec((tm, D_MODEL), lambda i, j: (i, 0)),
            pl.BlockSpec((D_MODEL, QKV_TN), lambda i, j: (0, j)),
            pl.BlockSpec((1, QKV_TN), lambda i, j: (0, j)),
            pl.BlockSpec((tm, LANES), lambda i, j: (i, 0)),
            pl.BlockSpec((tm, LANES), lambda i, j: (i, 0)),
        ],
        out_specs=pl.BlockSpec((tm, QKV_TN), lambda i, j: (i, j)),
        out_shape=jax.ShapeDtypeStruct((n, QKV_COLS), F32),
        compiler_params=_cparams("parallel", "arbitrary"),
        name="qkv_rope",
    )(x, w_bf, b.reshape(1, QKV_COLS), cos, sin)


def _sink_column(sink_ref, g, rows):
    return jnp.concatenate(
        [jnp.full((rows, 1), sink_ref[g * GQA_GROUP + j], F32) for j in range(GQA_GROUP)], axis=0)


def _attn_prompt_kernel(sink_ref, q_ref, kp_ref, kc_ref, vp_ref, vc_ref, o_ref):
    i = pl.program_id(1)
    cols = GQA_GROUP * WINDOW
    kj = lax.broadcasted_iota(jnp.int32, (2 * WINDOW, cols), 0)
    qi = lax.broadcasted_iota(jnp.int32, (2 * WINDOW, cols), 1) % WINDOW
    dist = qi + WINDOW - kj
    prev_ok = jnp.where(i > 0, 0, WINDOW)
    mask = (dist >= 0) & (dist <= WINDOW) & (kj >= prev_ok)
    nt = (((1,), (1,)), ((), ()))
    tn = (((0,), (0,)), ((), ()))
    for g in range(N_KV_HEADS):
        ks = slice(g * HEAD_DIM, (g + 1) * HEAD_DIM)
        k = jnp.concatenate([kp_ref[:, ks], kc_ref[:, ks]], axis=0).astype(BF16)
        v = jnp.concatenate([vp_ref[:, ks], vc_ref[:, ks]], axis=0).astype(BF16)
        q = jnp.concatenate(
            [q_ref[:, (g * GQA_GROUP + j) * HEAD_DIM:(g * GQA_GROUP + j + 1) * HEAD_DIM]
             for j in range(GQA_GROUP)], axis=0).astype(BF16)
        t = lax.dot_general(k, q, nt, preferred_element_type=F32)
        t = jnp.where(mask, t * (HEAD_DIM ** -0.5 * LOG2E), NEG_INF)
        sink = jnp.concatenate(
            [jnp.full((1, WINDOW), sink_ref[g * GQA_GROUP + j] * LOG2E, F32) for j in range(GQA_GROUP)], axis=1)
        m = jnp.maximum(jnp.max(t, axis=0, keepdims=True), sink)
        p = jnp.exp2(t - m)
        denom = jnp.sum(p, axis=0, keepdims=True) + jnp.exp2(sink - m)
        o_t = lax.dot_general(v, p.astype(BF16), tn, preferred_element_type=F32) / denom
        o = o_t.T
        o_ref[:, g * GQA_GROUP * HEAD_DIM:(g + 1) * GQA_GROUP * HEAD_DIM] = jnp.concatenate(
            [o[j * WINDOW:(j + 1) * WINDOW] for j in range(GQA_GROUP)], axis=1).astype(o_ref.dtype)


def _attn_prompt(qkv, sinks, batch, seq):
    nb = seq // WINDOW
    kcol = Q_COLS // KV_COLS
    vcol = kcol + 1

    def cur(b, i):
        return b * nb + i

    def prev(b, i):
        return b * nb + jnp.maximum(i - 1, 0)

    return pl.pallas_call(
        _attn_prompt_kernel,
        grid=(batch, nb),
        in_specs=[
            pl.BlockSpec(memory_space=pltpu.SMEM),
            pl.BlockSpec((WINDOW, Q_COLS), lambda b, i: (cur(b, i), 0)),
            pl.BlockSpec((WINDOW, KV_COLS), lambda b, i: (prev(b, i), kcol)),
            pl.BlockSpec((WINDOW, KV_COLS), lambda b, i: (cur(b, i), kcol)),
            pl.BlockSpec((WINDOW, KV_COLS), lambda b, i: (prev(b, i), vcol)),
            pl.BlockSpec((WINDOW, KV_COLS), lambda b, i: (cur(b, i), vcol)),
        ],
        out_specs=pl.BlockSpec((WINDOW, Q_COLS), lambda b, i: (cur(b, i), 0)),
        out_shape=jax.ShapeDtypeStruct((batch * seq, Q_COLS), BF16),
        compiler_params=_cparams("parallel", "arbitrary"),
        name="attn_prompt",
    )(sinks, qkv, qkv, qkv, qkv, qkv)


SAMPLE_ROWS = 8


def _attn_sample_kernel(sink_ref, qkv_ref, ck_ref, cv_ref, o_ref, *, n_new):
    qkv = qkv_ref[0]
    rows = GQA_GROUP * SAMPLE_ROWS
    tok = lax.broadcasted_iota(jnp.int32, (rows, WINDOW), 0) % SAMPLE_ROWS
    kj = lax.broadcasted_iota(jnp.int32, (rows, WINDOW), 1)
    mask_c = kj >= tok
    tok1 = lax.broadcasted_iota(jnp.int32, (rows, 1), 0) % SAMPLE_ROWS
    for g in range(N_KV_HEADS):
        ks = slice(g * HEAD_DIM, (g + 1) * HEAD_DIM)
        kc = ck_ref[0, :, ks].astype(BF16)
        vc = cv_ref[0, :, ks].astype(BF16)
        kn = qkv[:, Q_COLS + g * HEAD_DIM:Q_COLS + (g + 1) * HEAD_DIM]
        vn = qkv[:, Q_COLS + KV_COLS + g * HEAD_DIM:Q_COLS + KV_COLS + (g + 1) * HEAD_DIM]
        q = jnp.concatenate(
            [qkv[:, (g * GQA_GROUP + j) * HEAD_DIM:(g * GQA_GROUP + j + 1) * HEAD_DIM]
             for j in range(GQA_GROUP)], axis=0)
        s_c = lax.dot_general(q.astype(BF16), kc, (((1,), (1,)), ((), ())), preferred_element_type=F32)
        s_c = jnp.where(mask_c, s_c * (HEAD_DIM ** -0.5), NEG_INF)
        s_n = []
        for r in range(n_new):
            s_r = jnp.sum(q * kn[r:r + 1, :], axis=1, keepdims=True) * (HEAD_DIM ** -0.5)
            s_n.append(jnp.where(tok1 >= r, s_r, NEG_INF))
        sink = _sink_column(sink_ref, g, SAMPLE_ROWS)
        m = jnp.maximum(jnp.max(s_c, axis=1, keepdims=True), sink)
        for s_r in s_n:
            m = jnp.maximum(m, s_r)
        p_c = jnp.exp(s_c - m)
        denom = jnp.sum(p_c, axis=1, keepdims=True) + jnp.exp(sink - m)
        o = jnp.dot(p_c.astype(BF16), vc, preferred_element_type=F32)
        for r, s_r in enumerate(s_n):
            p_r = jnp.exp(s_r - m)
            denom = denom + p_r
            o = o + p_r * vn[r:r + 1, :]
        o = o / denom
        o_ref[0, :, g * GQA_GROUP * HEAD_DIM:(g + 1) * GQA_GROUP * HEAD_DIM] = jnp.concatenate(
            [o[j * SAMPLE_ROWS:(j + 1) * SAMPLE_ROWS] for j in range(GQA_GROUP)], axis=1)


def _attn_sample(qkv8, ck, cv, sinks, n_new):
    nb = qkv8.shape[0]
    return pl.pallas_call(
        functools.partial(_attn_sample_kernel, n_new=n_new),
        grid=(nb,),
        in_specs=[
            pl.BlockSpec(memory_space=pltpu.SMEM),
            pl.BlockSpec((1, SAMPLE_ROWS, QKV_COLS), lambda b: (b, 0, 0)),
            pl.BlockSpec((1, WINDOW, KV_COLS), lambda b: (b, 0, 0)),
            pl.BlockSpec((1, WINDOW, KV_COLS), lambda b: (b, 0, 0)),
        ],
        out_specs=pl.BlockSpec((1, SAMPLE_ROWS, Q_COLS), lambda b: (b, 0, 0)),
        out_shape=jax.ShapeDtypeStruct((nb, SAMPLE_ROWS, Q_COLS), F32),
        compiler_params=_cparams("parallel"),
        name="attn_sample",
    )(sinks, qkv8, ck, cv)


def _proj_ln_kernel(a_ref, w_ref, b_ref, x_ref, g_ref, beta_ref, o_ref):
    y = jnp.dot(a_ref[...].astype(BF16), w_ref[...], preferred_element_type=F32) + b_ref[...]
    o_ref[...] = _layer_norm(DEEPNORM_ALPHA * x_ref[...] + y, g_ref[...], beta_ref[...])


def _proj_ln(a, w_bf, bias, x, g, beta, tm):
    n, k = a.shape
    row = lambda i: (i, 0)
    fixed = lambda i: (0, 0)
    return pl.pallas_call(
        _proj_ln_kernel,
        grid=(n // tm,),
        in_specs=[
            pl.BlockSpec((tm, k), row),
            pl.BlockSpec((k, D_MODEL), fixed),
            pl.BlockSpec((1, D_MODEL), fixed),
            pl.BlockSpec((tm, D_MODEL), row),
            pl.BlockSpec((1, D_MODEL), fixed),
            pl.BlockSpec((1, D_MODEL), fixed),
        ],
        out_specs=pl.BlockSpec((tm, D_MODEL), row),
        out_shape=jax.ShapeDtypeStruct((n, D_MODEL), F32),
        compiler_params=_cparams("parallel"),
        name="proj_ln",
    )(a, w_bf, bias.reshape(1, D_MODEL), x, g.reshape(1, D_MODEL), beta.reshape(1, D_MODEL))


CONV_TN = 512


def _conv_in_kernel(x_ref, wb_ref, wc_ref, wh_ref, gb_ref, u_ref):
    xb = x_ref[...].astype(BF16)
    gb_ref[...] = jnp.dot(xb, wb_ref[...], preferred_element_type=F32)
    gate_c = jnp.dot(xb, wc_ref[...], preferred_element_type=F32)
    h = jnp.dot(xb, wh_ref[...], preferred_element_type=F32)
    u_ref[...] = gate_c * h


def _conv_in(x, w_in_bf, tm):
    n = x.shape[0]
    nt = D_MODEL // CONV_TN
    return pl.pallas_call(
        _conv_in_kernel,
        grid=(n // tm, nt),
        in_specs=[
            pl.BlockSpec((tm, D_MODEL), lambda i, j: (i, 0)),
            pl.BlockSpec((D_MODEL, CONV_TN), lambda i, j: (0, j)),
            pl.BlockSpec((D_MODEL, CONV_TN), lambda i, j: (0, nt + j)),
            pl.BlockSpec((D_MODEL, CONV_TN), lambda i, j: (0, 2 * nt + j)),
        ],
        out_specs=[
            pl.BlockSpec((tm, CONV_TN), lambda i, j: (i, j)),
            pl.BlockSpec((tm, CONV_TN), lambda i, j: (i, j)),
        ],
        out_shape=[jax.ShapeDtypeStruct((n, D_MODEL), F32)] * 2,
        compiler_params=_cparams("parallel", "arbitrary"),
        name="conv_in",
    )(x, w_in_bf, w_in_bf, w_in_bf)


def _conv_out_ln_kernel(gb_ref, u_ref, um1_ref, um2_ref, cw_ref, w_ref, x_ref, g_ref, beta_ref, o_ref):
    conv = um2_ref[...] * cw_ref[0:1, :]
    conv = conv + um1_ref[...] * cw_ref[1:2, :]
    conv = conv + u_ref[...] * cw_ref[2:3, :]
    a = (gb_ref[...] * conv).astype(BF16)
    y = jnp.dot(a, w_ref[...], preferred_element_type=F32)
    o_ref[...] = _layer_norm(DEEPNORM_ALPHA * x_ref[...] + y, g_ref[...], beta_ref[...])


def _conv_out_ln(gate_b, u, um1, um2, conv_w, w_out_bf, x, g, beta, tm):
    n = x.shape[0]
    row = lambda i: (i, 0)
    fixed = lambda i: (0, 0)
    tile = pl.BlockSpec((tm, D_MODEL), row)
    vec = pl.BlockSpec((1, D_MODEL), fixed)
    return pl.pallas_call(
        _conv_out_ln_kernel,
        grid=(n // tm,),
        in_specs=[tile, tile, tile, tile,
                  pl.BlockSpec((CONV_WIDTH, D_MODEL), fixed),
                  pl.BlockSpec((D_MODEL, D_MODEL), fixed),
                  tile, vec, vec],
        out_specs=tile,
        out_shape=jax.ShapeDtypeStruct((n, D_MODEL), F32),
        compiler_params=_cparams("parallel"),
        name="conv_out_ln",
    )(gate_b, u, um1, um2, conv_w, w_out_bf, x, g.reshape(1, D_MODEL), beta.reshape(1, D_MODEL))


def _oddeven_merge_sort_pairs(n):
    pairs = []
    p = 1
    while p < n:
        k = p
        while k >= 1:
            for j in range(k % p, n - k, 2 * k):
                for i in range(min(k, n - j - k)):
                    if (i + j) // (2 * p) == (i + j + k) // (2 * p):
                        pairs.append((i + j, i + j + k))
            k //= 2
        p *= 2
    return pairs


def _top_values(pieces, k):
    n = len(pieces)
    size = 1
    while size < n:
        size *= 2
    v = list(pieces) + [jnp.full_like(pieces[0], -jnp.inf)] * (size - n)
    for i, j in _oddeven_merge_sort_pairs(size):
        v[i], v[j] = jnp.maximum(v[i], v[j]), jnp.minimum(v[i], v[j])
    v = v[:n]
    vals = []
    for it in range(k):
        m = jnp.max(v[0], axis=0, keepdims=True)
        vals.append(m)
        need = k - it - 1
        if need:
            hit = v[0] == m
            nxt = v[1:] + [jnp.full_like(v[0], -jnp.inf)]
            v = [jnp.where(hit, nxt[j], v[j]) for j in range(min(need, len(v)))]
    return vals


def _peer_score_kernel(x_ref, wq_ref, keys_ref, xt_ref, s1l_ref, s2l_ref, tau_ref):
    pad = jnp.zeros((I1_PER_TILE,) + s1l_ref.shape[1:], F32)
    s1l_ref[:I1_PER_TILE] = pad
    s1l_ref[I1_PER_TILE + PEER_KEYS:] = pad
    x = x_ref[...]
    xt_ref[...] = x.T.astype(BF16)
    q = jnp.dot(x.astype(BF16), wq_ref[...], preferred_element_type=F32).astype(BF16)
    nt = (((1,), (1,)), ((), ()))
    k = PEER_TOPK
    for h in range(PEER_HEADS):
        c0 = h * 2 * PEER_HALF
        s1 = lax.dot_general(keys_ref[h, 0], q[:, c0:c0 + PEER_HALF], nt, preferred_element_type=F32)
        s2 = lax.dot_general(keys_ref[h, 1], q[:, c0 + PEER_HALF:c0 + 2 * PEER_HALF], nt,
                             preferred_element_type=F32)
        a = _top_values([s1[r:r + 8] for r in range(0, PEER_KEYS, 8)], k + 1)
        b = _top_values([s2[r:r + 8] for r in range(0, PEER_KEYS, 8)], k + 1)
        neg = jnp.full_like(a[0], -jnp.inf)
        b24 = jnp.concatenate(b + [neg] * 7, axis=0)
        b8 = jnp.concatenate(b[:8], axis=0)
        a16 = jnp.concatenate(a[8:] + [neg] * 7, axis=0)
        row = lax.broadcasted_iota(jnp.int32, b8.shape, 0)
        eb24 = jnp.exp(b24 - b[0])
        eb8 = jnp.exp(b8 - b[0])
        cand = [a[0] + b24[r:r + 8] for r in (0, 8, 16)]
        wgt = [eb24[r:r + 8] for r in (0, 8, 16)]
        for i in range(1, 8):
            cand.append(jnp.where(row < (k + 1) // (i + 1), a[i] + b8, -jnp.inf))
            wgt.append(jnp.exp(a[i] - a[0]) * eb8)
        cand += [a16[r:r + 8] + b[0] for r in (0, 8)]
        wgt += [jnp.exp(a16[r:r + 8] - a[0]) for r in (0, 8)]
        tops = _top_values(cand, k + 1)
        thr = 0.5 * (tops[k - 1] + tops[k])
        z8 = functools.reduce(jnp.add, [jnp.where(c >= thr, w, 0.0) for c, w in zip(cand, wgt)])
        z = jnp.sum(z8, axis=0, keepdims=True)
        shift = a[0] + b[0] + jnp.log(z)
        s1l_ref[I1_PER_TILE:I1_PER_TILE + PEER_KEYS, h, :] = (s1 - shift) * LOG2E
        s2l_ref[h] = s2 * LOG2E
        tau_ref[h:h + 1, :] = (thr - shift) * LOG2E


def _peer_scores(x, wq_bf, keys_bf, tm):
    n = x.shape[0]
    hk = (PEER_HEADS, PEER_KEYS, tm)
    hk_map = lambda i: (0, 0, i)
    hk_shape = jax.ShapeDtypeStruct((PEER_HEADS, PEER_KEYS, n), F32)
    return pl.pallas_call(
        _peer_score_kernel,
        grid=(n // tm,),
        in_specs=[
            pl.BlockSpec((tm, D_MODEL), lambda i: (i, 0)),
            pl.BlockSpec((D_MODEL, PEER_HEADS * 2 * PEER_HALF), lambda i: (0, 0)),
            pl.BlockSpec((PEER_HEADS, 2, PEER_KEYS, PEER_HALF), lambda i: (0, 0, 0, 0)),
        ],
        out_specs=[
            pl.BlockSpec((D_MODEL, tm), lambda i: (0, i)),
            pl.BlockSpec((PEER_KEYS + 2 * I1_PER_TILE, PEER_HEADS, tm), hk_map), pl.BlockSpec(hk, hk_map),
            pl.BlockSpec((PEER_HEADS, tm), lambda i: (0, i)),
        ],
        out_shape=[
            jax.ShapeDtypeStruct((D_MODEL, n), BF16),
            jax.ShapeDtypeStruct((PEER_KEYS + 2 * I1_PER_TILE, PEER_HEADS, n), F32), hk_shape,
            jax.ShapeDtypeStruct((PEER_HEADS, n), F32),
        ],
        compiler_params=_cparams("parallel"),
        name="peer_scores",
    )(x, wq_bf, keys_bf)


def _gelu_exact(x):
    return 0.5 * x * (1.0 + lax.erf(x * (2.0 ** -0.5)))


N_PAIRS = PEER_EXPERTS // (2 * PEER_TE)


def _dense_pipeline_step(p, u_ref, v_ref, xt_ref, s1l_ref, s2l_ref, tau_ref, o_ref, hid, at):
    tm = xt_ref.shape[1]

    @pl.when(p == 0)
    def _():
        o_ref[...] = jnp.zeros_like(o_ref)
        for buf in hid + at:
            buf[...] = jnp.zeros_like(buf)

    for half in range(2):
        rows = slice(half * PEER_TE, (half + 1) * PEER_TE)
        hid[half][...] = jnp.dot(u_ref[rows, :].astype(BF16), xt_ref[...], preferred_element_type=F32)

        src, dst = hid[1 - half], at[1 - half]
        for il in range(I1_PER_TILE):
            r = half * I1_PER_TILE + il
            rs = slice(il * PEER_KEYS, (il + 1) * PEER_KEYS)
            for c in range(tm // LANES):
                cs = slice(c * LANES, (c + 1) * LANES)
                gate = jnp.zeros((PEER_KEYS, LANES), F32)
                for h in range(PEER_HEADS):
                    pair = s2l_ref[h, :, cs] + s1l_ref[r, h:h + 1, cs]
                    gate = gate + jnp.where(pair >= tau_ref[h:h + 1, cs], jnp.exp2(pair), 0.0)
                dst[rs, cs] = (_gelu_exact(src[rs, cs]) * gate).astype(BF16)

        o_ref[...] += lax.dot_general(at[half][...], v_ref[rows, :].astype(BF16), (((0,), (0,)), ((), ())),
                                      preferred_element_type=F32)


def _peer_dense_kernel(u_ref, v_ref, xt_ref, s1l_ref, s2l_ref, tau_ref, xts_ref, s1ls_ref, s2ls_ref, taus_ref,
                       o_ref, os_ref, hid0, hid1, at0, at1, hids0, hids1, ats0, ats1):
    t = pl.program_id(0)
    p = pl.program_id(1)
    _dense_pipeline_step(p, u_ref, v_ref, xt_ref, s1l_ref, s2l_ref, tau_ref, o_ref, (hid0, hid1), (at0, at1))

    @pl.when(t == 0)
    def _():
        _dense_pipeline_step(p, u_ref, v_ref, xts_ref, s1ls_ref, s2ls_ref, taus_ref, os_ref,
                             (hids0, hids1), (ats0, ats1))


def _peer_dense(u_all, v_all, layer, prompt, sample, tm):
    n = prompt[0].shape[1]
    n_s = sample[0].shape[1]
    u_spec = pl.BlockSpec((None, 2 * PEER_TE, D_MODEL), lambda t, p: (layer, jnp.minimum(p, N_PAIRS - 1), 0))
    v_spec = pl.BlockSpec((None, 2 * PEER_TE, D_MODEL), lambda t, p: (layer, jnp.maximum(p - 1, 0), 0))

    def token_specs(width, tile_of):
        return [
            pl.BlockSpec((D_MODEL, width), lambda t, p: (0, tile_of(t))),
            pl.BlockSpec((2 * I1_PER_TILE, PEER_HEADS, width), lambda t, p: (p, 0, tile_of(t))),
            pl.BlockSpec((PEER_HEADS, PEER_KEYS, width), lambda t, p: (0, 0, tile_of(t)),
                         pipeline_mode=pl.Buffered(1)),
            pl.BlockSpec((PEER_HEADS, width), lambda t, p: (0, tile_of(t))),
        ]

    def buffers(width):
        return [pltpu.VMEM((PEER_TE, width), F32), pltpu.VMEM((PEER_TE, width), F32),
                pltpu.VMEM((PEER_TE, width), BF16), pltpu.VMEM((PEER_TE, width), BF16)]

    return pl.pallas_call(
        _peer_dense_kernel,
        grid=(n // tm, N_PAIRS + 1),
        in_specs=[u_spec, v_spec] + token_specs(tm, lambda t: t) + token_specs(n_s, lambda t: 0),
        out_specs=[pl.BlockSpec((tm, D_MODEL), lambda t, p: (t, 0)),
                   pl.BlockSpec((n_s, D_MODEL), lambda t, p: (0, 0))],
        out_shape=[jax.ShapeDtypeStruct((n, D_MODEL), F32), jax.ShapeDtypeStruct((n_s, D_MODEL), F32)],
        scratch_shapes=buffers(tm) + buffers(n_s),
        compiler_params=_cparams("arbitrary", "arbitrary"),
        name="peer_dense",
    )(u_all, v_all, *prompt, *sample)


def _res_ln_kernel(x_ref, f_ref, g_ref, beta_ref, o_ref):
    o_ref[...] = _layer_norm(DEEPNORM_ALPHA * x_ref[...] + f_ref[...], g_ref[...], beta_ref[...])


def _res_ln(x, f, g, beta, tm):
    n = x.shape[0]
    tile = pl.BlockSpec((tm, D_MODEL), lambda i: (i, 0))
    vec = pl.BlockSpec((1, D_MODEL), lambda i: (0, 0))
    return pl.pallas_call(
        _res_ln_kernel,
        grid=(n // tm,),
        in_specs=[tile, tile, vec, vec],
        out_specs=tile,
        out_shape=jax.ShapeDtypeStruct((n, D_MODEL), F32),
        compiler_params=_cparams("parallel"),
        name="res_ln",
    )(x, f, g.reshape(1, D_MODEL), beta.reshape(1, D_MODEL))


def _peer_layer(xp, xs, wq_bf, keys_bf, u_all, v_all, layer, g, beta):
    prompt = _peer_scores(xp, wq_bf, keys_bf, 256)
    sample = _peer_scores(xs, wq_bf, keys_bf, xs.shape[0])
    fp, fs = _peer_dense(u_all, v_all, layer, prompt, sample, 1024)
    return _res_ln(xp, fp, g, beta, 512), _res_ln(xs, fs, g, beta, xs.shape[0])


def _rope_tables(pos):
    half = HEAD_DIM // 2
    inv = jnp.power(ROPE_THETA, -jnp.arange(half, dtype=F32) * 2.0 / HEAD_DIM)
    ang = pos.astype(F32)[:, None] * inv[None, :]
    cos = jnp.cos(ang)
    sin = jnp.sin(ang)
    reps = LANES // HEAD_DIM
    return (jnp.tile(jnp.concatenate([cos, cos], axis=1), (1, reps)),
            jnp.tile(jnp.concatenate([-sin, sin], axis=1), (1, reps)))


def _shifted(state, u3):
    t = u3.shape[1]
    up = jnp.concatenate([state, u3], axis=1)
    return up[:, 1:t + 1], up[:, 0:t], up[:, -(CONV_WIDTH - 1):]


def kernel(x_prompt, x_sample, cache_k, cache_v, state_conv, w_qkv, b_qkv, w_o, b_o, attn_sinks,
           w_conv_in, conv_w, w_conv_out, w_peer_q, peer_sub_keys, peer_u, peer_v, ln_g, ln_b):
    batch, seq, _ = x_prompt.shape
    dec_batch, dec_seq, _ = x_sample.shape
    n_p = batch * seq
    n_s = dec_batch * dec_seq
    xp = x_prompt.reshape(n_p, D_MODEL)
    xs = x_sample.reshape(n_s, D_MODEL)
    cos_p, sin_p = _rope_tables(jnp.tile(jnp.arange(seq), batch))
    cos_s, sin_s = _rope_tables(jnp.tile(PAST_LEN + jnp.arange(dec_seq), dec_batch))

    new_kp, new_vp, new_cp, new_ks, new_vs, new_cs = [], [], [], [], [], []
    for layer in range(DEPTH):
        j = layer // 2
        g0, b0 = ln_g[layer, 0], ln_b[layer, 0]
        if layer % 2 == 0:
            wqkv = w_qkv[j].astype(BF16)
            wo = w_o[j].astype(BF16)
            sinks = attn_sinks[j].reshape(N_Q_HEADS)
            qkv_p = _qkv_rope(xp, wqkv, b_qkv[j], cos_p, sin_p, 1024)
            qkv_s = _qkv_rope(xs, wqkv, b_qkv[j], cos_s, sin_s, n_s)
            o_p = _attn_prompt(qkv_p, sinks, batch, seq)
            qkv_s3 = qkv_s.reshape(dec_batch, dec_seq, QKV_COLS)
            qkv_s8 = jnp.pad(qkv_s3, ((0, 0), (0, SAMPLE_ROWS - dec_seq), (0, 0)))
            ck = cache_k[j].reshape(dec_batch, WINDOW, KV_COLS)
            cv = cache_v[j].reshape(dec_batch, WINDOW, KV_COLS)
            o_s = _attn_sample(qkv_s8, ck, cv, sinks, dec_seq)[:, :dec_seq].reshape(n_s, Q_COLS)
            kv_p = qkv_p.reshape(batch, seq, QKV_COLS)[:, -WINDOW:, Q_COLS:]
            new_kp.append(kv_p[..., :KV_COLS].reshape(batch, WINDOW, N_KV_HEADS, HEAD_DIM))
            new_vp.append(kv_p[..., KV_COLS:].reshape(batch, WINDOW, N_KV_HEADS, HEAD_DIM))
            k_s = jnp.concatenate([ck, qkv_s3[..., Q_COLS:Q_COLS + KV_COLS]], axis=1)[:, -WINDOW:]
            v_s = jnp.concatenate([cv, qkv_s3[..., Q_COLS + KV_COLS:]], axis=1)[:, -WINDOW:]
            new_ks.append(k_s.reshape(dec_batch, WINDOW, N_KV_HEADS, HEAD_DIM))
            new_vs.append(v_s.reshape(dec_batch, WINDOW, N_KV_HEADS, HEAD_DIM))
            xp = _proj_ln(o_p, wo, b_o[j], xp, g0, b0, 512)
            xs = _proj_ln(o_s, wo, b_o[j], xs, g0, b0, n_s)
        else:
            w_in = w_conv_in[j].astype(BF16)
            w_out = w_conv_out[j].astype(BF16)
            gb_p, u_p = _conv_in(xp, w_in, 512)
            gb_s, u_s = _conv_in(xs, w_in, n_s)
            zero_state = jnp.zeros((batch, CONV_WIDTH - 1, D_MODEL), F32)
            um1_p, um2_p, cp = _shifted(zero_state, u_p.reshape(batch, seq, D_MODEL))
            um1_s, um2_s, cs = _shifted(state_conv[j], u_s.reshape(dec_batch, dec_seq, D_MODEL))
            new_cp.append(cp)
            new_cs.append(cs)
            xp = _conv_out_ln(gb_p, u_p, um1_p.reshape(n_p, D_MODEL), um2_p.reshape(n_p, D_MODEL),
                              conv_w[j], w_out, xp, g0, b0, 256)
            xs = _conv_out_ln(gb_s, u_s, um1_s.reshape(n_s, D_MODEL), um2_s.reshape(n_s, D_MODEL),
                              conv_w[j], w_out, xs, g0, b0, n_s)
        wq = w_peer_q[layer].astype(BF16)
        keys = peer_sub_keys[layer].astype(BF16)
        g1, b1 = ln_g[layer, 1], ln_b[layer, 1]
        xp, xs = _peer_layer(xp, xs, wq, keys, peer_u, peer_v, layer, g1, b1)

    return (xp.reshape(batch, seq, D_MODEL), xs.reshape(dec_batch, dec_seq, D_MODEL),
            jnp.stack(new_kp), jnp.stack(new_vp), jnp.stack(new_cp),
            jnp.stack(new_ks), jnp.stack(new_vs), jnp.stack(new_cs))
```

```python
import functools

import jax
import jax.numpy as jnp
from jax import lax
from jax.experimental import pallas as pl
from jax.experimental.pallas import tpu as pltpu

F32 = jnp.float32
BF16 = jnp.bfloat16

D_MODEL = 2048
DEPTH = 4
HEAD_DIM = 64
N_Q_HEADS = 32
N_KV_HEADS = 8
GQA_GROUP = 4
WINDOW = 128
ROPE_THETA = 10000.0
PAST_LEN = 16384
CONV_WIDTH = 3
PEER_HEADS = 8
PEER_KEYS = 128
PEER_EXPERTS = PEER_KEYS * PEER_KEYS
PEER_HALF = 128
PEER_TOPK = 16
LN_EPS = 1e-5
NEG_INF = -1e30
DEEPNORM_ALPHA = (2 * DEPTH) ** 0.25
LOG2E = 1.4426950408889634

Q_COLS = N_Q_HEADS * HEAD_DIM
KV_COLS = N_KV_HEADS * HEAD_DIM
QKV_COLS = Q_COLS + 2 * KV_COLS

LANES = 128
PEER_TE = 256
I1_PER_TILE = PEER_TE // PEER_KEYS
VMEM_LIMIT_BYTES = 56 * 1024 * 1024


def _cparams(*semantics):
    return pltpu.CompilerParams(dimension_semantics=semantics, vmem_limit_bytes=VMEM_LIMIT_BYTES)


def _layer_norm(z, g, b):
    mu = jnp.mean(z, axis=-1, keepdims=True)
    zc = z - mu
    var = jnp.mean(zc * zc, axis=-1, keepdims=True)
    return zc * lax.rsqrt(var + LN_EPS) * g + b


QKV_TN = 512
N_ROPE_TILES = (Q_COLS + KV_COLS) // QKV_TN


def _qkv_kernel(x_ref, w_ref, b_ref, cos_ref, sin_ref, o_ref):
    j = pl.program_id(1)
    acc = jnp.dot(x_ref[...].astype(BF16), w_ref[...], preferred_element_type=F32) + b_ref[...]

    @pl.when(j < N_ROPE_TILES)
    def _():
        cos = cos_ref[...]
        sin = sin_ref[...]
        lane = lax.broadcasted_iota(jnp.int32, cos.shape, 1)
        first_half = (lane % HEAD_DIM) < (HEAD_DIM // 2)
        for c in range(QKV_TN // LANES):
            seg = acc[:, c * LANES:(c + 1) * LANES]
            partner = jnp.where(first_half,
                                pltpu.roll(seg, LANES - HEAD_DIM // 2, 1),
                                pltpu.roll(seg, HEAD_DIM // 2, 1))
            o_ref[:, c * LANES:(c + 1) * LANES] = seg * cos + partner * sin

    @pl.when(j >= N_ROPE_TILES)
    def _():
        o_ref[...] = acc


def _qkv_rope(x, w_bf, b, cos, sin, tm):
    n = x.shape[0]
    return pl.pallas_call(
        _qkv_kernel,
        grid=(n // tm, QKV_COLS // QKV_TN),
        in_specs=[
            pl.BlockSpec((tm, D_MODEL), lambda i, j: (i, 0)),
            pl.BlockSpec((D_MODEL, QKV_TN), lambda i, j: (0, j)),
            pl.BlockSpec((1, QKV_TN), lambda i, j: (0, j)),
            pl.BlockSpec((tm, LANES), lambda i, j: (i, 0)),
            pl.BlockSpec((tm, LANES), lambda i, j: (i, 0)),
        ],
        out_specs=pl.BlockSpec((tm, QKV_TN), lambda i, j: (i, j)),
        out_shape=jax.ShapeDtypeStruct((n, QKV_COLS), F32),
        compiler_params=_cparams("parallel", "arbitrary"),
        name="qkv_rope",
    )(x, w_bf, b.reshape(1, QKV_COLS), cos, sin)


def _sink_column(sink_ref, g, rows):
    return jnp.concatenate(
        [jnp.full((rows, 1), sink_ref[g * GQA_GROUP + j], F32) for j in range(GQA_GROUP)], axis=0)


def _attn_prompt_kernel(sink_ref, q_ref, kp_ref, kc_ref, vp_ref, vc_ref, o_ref):
    i = pl.program_id(1)
    cols = GQA_GROUP * WINDOW
    kj = lax.broadcasted_iota(jnp.int32, (2 * WINDOW, cols), 0)
    qi = lax.broadcasted_iota(jnp.int32, (2 * WINDOW, cols), 1) % WINDOW
    dist = qi + WINDOW - kj
    prev_ok = jnp.where(i > 0, 0, WINDOW)
    mask = (dist >= 0) & (dist <= WINDOW) & (kj >= prev_ok)
    nt = (((1,), (1,)), ((), ()))
    tn = (((0,), (0,)), ((), ()))
    for g in range(N_KV_HEADS):
        ks = slice(g * HEAD_DIM, (g + 1) * HEAD_DIM)
        k = jnp.concatenate([kp_ref[:, ks], kc_ref[:, ks]], axis=0).astype(BF16)
        v = jnp.concatenate([vp_ref[:, ks], vc_ref[:, ks]], axis=0).astype(BF16)
        q = jnp.concatenate(
            [q_ref[:, (g * GQA_GROUP + j) * HEAD_DIM:(g * GQA_GROUP + j + 1) * HEAD_DIM]
             for j in range(GQA_GROUP)], axis=0).astype(BF16)
        t = lax.dot_general(k, q, nt, preferred_element_type=F32)
        t = jnp.where(mask, t * (HEAD_DIM ** -0.5 * LOG2E), NEG_INF)
        sink = jnp.concatenate(
            [jnp.full((1, WINDOW), sink_ref[g * GQA_GROUP + j] * LOG2E, F32) for j in range(GQA_GROUP)], axis=1)
        m = jnp.maximum(jnp.max(t, axis=0, keepdims=True), sink)
        p = jnp.exp2(t - m)
        denom = jnp.sum(p, axis=0, keepdims=True) + jnp.exp2(sink - m)
        o_t = lax.dot_general(v, p.astype(BF16), tn, preferred_element_type=F32) / denom
        o = o_t.T
        o_ref[:, g * GQA_GROUP * HEAD_DIM:(g + 1) * GQA_GROUP * HEAD_DIM] = jnp.concatenate(
            [o[j * WINDOW:(j + 1) * WINDOW] for j in range(GQA_GROUP)], axis=1).astype(o_ref.dtype)


def _attn_prompt(qkv, sinks, batch, seq):
    nb = seq // WINDOW
    kcol = Q_COLS // KV_COLS
    vcol = kcol + 1

    def cur(b, i):
        return b * nb + i

    def prev(b, i):
        return b * nb + jnp.maximum(i - 1, 0)

    return pl.pallas_call(
        _attn_prompt_kernel,
        grid=(batch, nb),
        in_specs=[
            pl.BlockSpec(memory_space=pltpu.SMEM),
            pl.BlockSpec((WINDOW, Q_COLS), lambda b, i: (cur(b, i), 0)),
            pl.BlockSpec((WINDOW, KV_COLS), lambda b, i: (prev(b, i), kcol)),
            pl.BlockSpec((WINDOW, KV_COLS), lambda b, i: (cur(b, i), kcol)),
            pl.BlockSpec((WINDOW, KV_COLS), lambda b, i: (prev(b, i), vcol)),
            pl.BlockSpec((WINDOW, KV_COLS), lambda b, i: (cur(b, i), vcol)),
        ],
        out_specs=pl.BlockSpec((WINDOW, Q_COLS), lambda b, i: (cur(b, i), 0)),
        out_shape=jax.ShapeDtypeStruct((batch * seq, Q_COLS), BF16),
        compiler_params=_cparams("parallel", "arbitrary"),
        name="attn_prompt",
    )(sinks, qkv, qkv, qkv, qkv, qkv)


SAMPLE_ROWS = 8


def _attn_sample_kernel(sink_ref, qkv_ref, ck_ref, cv_ref, o_ref, *, n_new):
    qkv = qkv_ref[0]
    rows = GQA_GROUP * SAMPLE_ROWS
    tok = lax.broadcasted_iota(jnp.int32, (rows, WINDOW), 0) % SAMPLE_ROWS
    kj = lax.broadcasted_iota(jnp.int32, (rows, WINDOW), 1)
    mask_c = kj >= tok
    tok1 = lax.broadcasted_iota(jnp.int32, (rows, 1), 0) % SAMPLE_ROWS
    for g in range(N_KV_HEADS):
        ks = slice(g * HEAD_DIM, (g + 1) * HEAD_DIM)
        kc = ck_ref[0, :, ks].astype(BF16)
        vc = cv_ref[0, :, ks].astype(BF16)
        kn = qkv[:, Q_COLS + g * HEAD_DIM:Q_COLS + (g + 1) * HEAD_DIM]
        vn = qkv[:, Q_COLS + KV_COLS + g * HEAD_DIM:Q_COLS + KV_COLS + (g + 1) * HEAD_DIM]
        q = jnp.concatenate(
            [qkv[:, (g * GQA_GROUP + j) * HEAD_DIM:(g * GQA_GROUP + j + 1) * HEAD_DIM]
             for j in range(GQA_GROUP)], axis=0)
        s_c = lax.dot_general(q.astype(BF16), kc, (((1,), (1,)), ((), ())), preferred_element_type=F32)
        s_c = jnp.where(mask_c, s_c * (HEAD_DIM ** -0.5), NEG_INF)
        s_n = []
        for r in range(n_new):
            s_r = jnp.sum(q * kn[r:r + 1, :], axis=1, keepdims=True) * (HEAD_DIM ** -0.5)
            s_n.append(jnp.where(tok1 >= r, s_r, NEG_INF))
        sink = _sink_column(sink_ref, g, SAMPLE_ROWS)
        m = jnp.maximum(jnp.max(s_c, axis=1, keepdims=True), sink)
        for s_r in s_n:
            m = jnp.maximum(m, s_r)
        p_c = jnp.exp(s_c - m)
        denom = jnp.sum(p_c, axis=1, keepdims=True) + jnp.exp(sink - m)
        o = jnp.dot(p_c.astype(BF16), vc, preferred_element_type=F32)
        for r, s_r in enumerate(s_n):
            p_r = jnp.exp(s_r - m)
            denom = denom + p_r
            o = o + p_r * vn[r:r + 1, :]
        o = o / denom
        o_ref[0, :, g * GQA_GROUP * HEAD_DIM:(g + 1) * GQA_GROUP * HEAD_DIM] = jnp.concatenate(
            [o[j * SAMPLE_ROWS:(j + 1) * SAMPLE_ROWS] for j in range(GQA_GROUP)], axis=1)


def _attn_sample(qkv8, ck, cv, sinks, n_new):
    nb = qkv8.shape[0]
    return pl.pallas_call(
        functools.partial(_attn_sample_kernel, n_new=n_new),
        grid=(nb,),
        in_specs=[
            pl.BlockSpec(memory_space=pltpu.SMEM),
            pl.BlockSpec((1, SAMPLE_ROWS, QKV_COLS), lambda b: (b, 0, 0)),
            pl.BlockSpec((1, WINDOW, KV_COLS), lambda b: (b, 0, 0)),
            pl.BlockSpec((1, WINDOW, KV_COLS), lambda b: (b, 0, 0)),
        ],
        out_specs=pl.BlockSpec((1, SAMPLE_ROWS, Q_COLS), lambda b: (b, 0, 0)),
        out_shape=jax.ShapeDtypeStruct((nb, SAMPLE_ROWS, Q_COLS), F32),
        compiler_params=_cparams("parallel"),
        name="attn_sample",
    )(sinks, qkv8, ck, cv)


def _proj_ln_kernel(a_ref, w_ref, b_ref, x_ref, g_ref, beta_ref, o_ref):
    y = jnp.dot(a_ref[...].astype(BF16), w_ref[...], preferred_element_type=F32) + b_ref[...]
    o_ref[...] = _layer_norm(DEEPNORM_ALPHA * x_ref[...] + y, g_ref[...], beta_ref[...])


def _proj_ln(a, w_bf, bias, x, g, beta, tm):
    n, k = a.shape
    row = lambda i: (i, 0)
    fixed = lambda i: (0, 0)
    return pl.pallas_call(
        _proj_ln_kernel,
        grid=(n // tm,),
        in_specs=[
            pl.BlockSpec((tm, k), row),
            pl.BlockSpec((k, D_MODEL), fixed),
            pl.BlockSpec((1, D_MODEL), fixed),
            pl.BlockSpec((tm, D_MODEL), row),
            pl.BlockSpec((1, D_MODEL), fixed),
            pl.BlockSpec((1, D_MODEL), fixed),
        ],
        out_specs=pl.BlockSpec((tm, D_MODEL), row),
        out_shape=jax.ShapeDtypeStruct((n, D_MODEL), F32),
        compiler_params=_cparams("parallel"),
        name="proj_ln",
    )(a, w_bf, bias.reshape(1, D_MODEL), x, g.reshape(1, D_MODEL), beta.reshape(1, D_MODEL))


CONV_TN = 512


def _conv_in_kernel(x_ref, wb_ref, wc_ref, wh_ref, gb_ref, u_ref):
    xb = x_ref[...].astype(BF16)
    gb_ref[...] = jnp.dot(xb, wb_ref[...], preferred_element_type=F32)
    gate_c = jnp.dot(xb, wc_ref[...], preferred_element_type=F32)
    h = jnp.dot(xb, wh_ref[...], preferred_element_type=F32)
    u_ref[...] = gate_c * h


def _conv_in(x, w_in_bf, tm):
    n = x.shape[0]
    nt = D_MODEL // CONV_TN
    return pl.pallas_call(
        _conv_in_kernel,
        grid=(n // tm, nt),
        in_specs=[
            pl.BlockSpec((tm, D_MODEL), lambda i, j: (i, 0)),
            pl.BlockSpec((D_MODEL, CONV_TN), lambda i, j: (0, j)),
            pl.BlockSpec((D_MODEL, CONV_TN), lambda i, j: (0, nt + j)),
            pl.BlockSpec((D_MODEL, CONV_TN), lambda i, j: (0, 2 * nt + j)),
        ],
        out_specs=[
            pl.BlockSpec((tm, CONV_TN), lambda i, j: (i, j)),
            pl.BlockSpec((tm, CONV_TN), lambda i, j: (i, j)),
        ],
        out_shape=[jax.ShapeDtypeStruct((n, D_MODEL), F32)] * 2,
        compiler_params=_cparams("parallel", "arbitrary"),
        name="conv_in",
    )(x, w_in_bf, w_in_bf, w_in_bf)


def _gated_out_ln(gate_b, conv, w_ref, x_ref, g_ref, beta_ref, o_ref):
    y = jnp.dot((gate_b * conv).astype(BF16), w_ref[...], preferred_element_type=F32)
    o_ref[...] = _layer_norm(DEEPNORM_ALPHA * x_ref[...] + y, g_ref[...], beta_ref[...])


def _conv_out_ln_kernel(gb_ref, u_ref, um1_ref, um2_ref, cw_ref, w_ref, x_ref, g_ref, beta_ref, o_ref):
    conv = um2_ref[...] * cw_ref[0:1, :]
    conv = conv + um1_ref[...] * cw_ref[1:2, :]
    conv = conv + u_ref[...] * cw_ref[2:3, :]
    _gated_out_ln(gb_ref[...], conv, w_ref, x_ref, g_ref, beta_ref, o_ref)


def _conv_out_ln_halo_kernel(gb_ref, u_ref, halo_ref, cw_ref, w_ref, x_ref, g_ref, beta_ref, o_ref,
                             *, tiles_per_seq):
    i = pl.program_id(0)
    u = u_ref[...]
    halo = jnp.where(i % tiles_per_seq == 0, 0.0, halo_ref[...])
    row = lax.broadcasted_iota(jnp.int32, u.shape, 0)
    um1 = jnp.where(row == 0, halo[7:8, :], pltpu.roll(u, 1, 0))
    um2 = jnp.where(row == 0, halo[6:7, :], jnp.where(row == 1, halo[7:8, :], pltpu.roll(u, 2, 0)))
    conv = um2 * cw_ref[0:1, :] + um1 * cw_ref[1:2, :] + u * cw_ref[2:3, :]
    _gated_out_ln(gb_ref[...], conv, w_ref, x_ref, g_ref, beta_ref, o_ref)


def _conv_out_ln(gate_b, u, shifted, conv_w, w_out_bf, x, g, beta, tm, seq=None):
    n = x.shape[0]
    row = lambda i: (i, 0)
    fixed = lambda i: (0, 0)
    tile = pl.BlockSpec((tm, D_MODEL), row)
    vec = pl.BlockSpec((1, D_MODEL), fixed)
    tail_specs = [pl.BlockSpec((CONV_WIDTH, D_MODEL), fixed), pl.BlockSpec((D_MODEL, D_MODEL), fixed),
                  tile, vec, vec]
    tail_args = (conv_w, w_out_bf, x, g.reshape(1, D_MODEL), beta.reshape(1, D_MODEL))
    if shifted is None:
        body = functools.partial(_conv_out_ln_halo_kernel, tiles_per_seq=seq // tm)
        halo = pl.BlockSpec((8, D_MODEL), lambda i: (jnp.maximum(i * (tm // 8) - 1, 0), 0))
        in_specs, args = [tile, tile, halo], (gate_b, u, u)
    else:
        body = _conv_out_ln_kernel
        in_specs, args = [tile, tile, tile, tile], (gate_b, u) + tuple(shifted)
    return pl.pallas_call(
        body,
        grid=(n // tm,),
        in_specs=in_specs + tail_specs,
        out_specs=tile,
        out_shape=jax.ShapeDtypeStruct((n, D_MODEL), F32),
        compiler_params=_cparams("parallel"),
        name="conv_out_ln",
    )(*args, *tail_args)


def _oddeven_merge_sort_pairs(n):
    pairs = []
    p = 1
    while p < n:
        k = p
        while k >= 1:
            for j in range(k % p, n - k, 2 * k):
                for i in range(min(k, n - j - k)):
                    if (i + j) // (2 * p) == (i + j + k) // (2 * p):
                        pairs.append((i + j, i + j + k))
            k //= 2
        p *= 2
    return pairs


def _top_values(pieces, k):
    n = len(pieces)
    size = 1
    while size < n:
        size *= 2
    v = list(pieces) + [jnp.full_like(pieces[0], -jnp.inf)] * (size - n)
    for i, j in _oddeven_merge_sort_pairs(size):
        v[i], v[j] = jnp.maximum(v[i], v[j]), jnp.minimum(v[i], v[j])
    v = v[:n]
    vals = []
    for it in range(k):
        m = jnp.max(v[0], axis=0, keepdims=True)
        vals.append(m)
        need = k - it - 1
        if need:
            hit = v[0] == m
            nxt = v[1:] + [jnp.full_like(v[0], -jnp.inf)]
            v = [jnp.where(hit, nxt[j], v[j]) for j in range(min(need, len(v)))]
    return vals


def _peer_score_kernel(x_ref, wq_ref, keys_ref, xt_ref, s1l_ref, s2l_ref, tau_ref):
    pad = jnp.zeros((I1_PER_TILE,) + s1l_ref.shape[1:], F32)
    s1l_ref[:I1_PER_TILE] = pad
    s1l_ref[I1_PER_TILE + PEER_KEYS:] = pad
    x = x_ref[...]
    xt_ref[...] = x.T.astype(BF16)
    q = jnp.dot(x.astype(BF16), wq_ref[...], preferred_element_type=F32).astype(BF16)
    nt = (((1,), (1,)), ((), ()))
    k = PEER_TOPK
    for h in range(PEER_HEADS):
        c0 = h * 2 * PEER_HALF
        s1 = lax.dot_general(keys_ref[h, 0], q[:, c0:c0 + PEER_HALF], nt, preferred_element_type=F32)
        s2 = lax.dot_general(keys_ref[h, 1], q[:, c0 + PEER_HALF:c0 + 2 * PEER_HALF], nt,
                             preferred_element_type=F32)
        a = _top_values([s1[r:r + 8] for r in range(0, PEER_KEYS, 8)], k + 1)
        b = _top_values([s2[r:r + 8] for r in range(0, PEER_KEYS, 8)], k + 1)
        neg = jnp.full_like(a[0], -jnp.inf)
        b24 = jnp.concatenate(b + [neg] * 7, axis=0)
        b8 = jnp.concatenate(b[:8], axis=0)
        a16 = jnp.concatenate(a[8:] + [neg] * 7, axis=0)
        row = lax.broadcasted_iota(jnp.int32, b8.shape, 0)
        eb24 = jnp.exp(b24 - b[0])
        eb8 = jnp.exp(b8 - b[0])
        cand = [a[0] + b24[r:r + 8] for r in (0, 8, 16)]
        wgt = [eb24[r:r + 8] for r in (0, 8, 16)]
        for i in range(1, 8):
            cand.append(jnp.where(row < (k + 1) // (i + 1), a[i] + b8, -jnp.inf))
            wgt.append(jnp.exp(a[i] - a[0]) * eb8)
        cand += [a16[r:r + 8] + b[0] for r in (0, 8)]
        wgt += [jnp.exp(a16[r:r + 8] - a[0]) for r in (0, 8)]
        tops = _top_values(cand, k + 1)
        thr = 0.5 * (tops[k - 1] + tops[k])
        z8 = functools.reduce(jnp.add, [jnp.where(c >= thr, w, 0.0) for c, w in zip(cand, wgt)])
        z = jnp.sum(z8, axis=0, keepdims=True)
        shift = a[0] + b[0] + jnp.log(z)
        s1l_ref[I1_PER_TILE:I1_PER_TILE + PEER_KEYS, h, :] = (s1 - shift) * LOG2E
        s2l_ref[h] = s2 * LOG2E
        tau_ref[h:h + 1, :] = (thr - shift) * LOG2E


def _peer_scores(x, wq_bf, keys_bf, tm):
    n = x.shape[0]
    hk = (PEER_HEADS, PEER_KEYS, tm)
    hk_map = lambda i: (0, 0, i)
    hk_shape = jax.ShapeDtypeStruct((PEER_HEADS, PEER_KEYS, n), F32)
    return pl.pallas_call(
        _peer_score_kernel,
        grid=(n // tm,),
        in_specs=[
            pl.BlockSpec((tm, D_MODEL), lambda i: (i, 0)),
            pl.BlockSpec((D_MODEL, PEER_HEADS * 2 * PEER_HALF), lambda i: (0, 0)),
            pl.BlockSpec((PEER_HEADS, 2, PEER_KEYS, PEER_HALF), lambda i: (0, 0, 0, 0)),
        ],
        out_specs=[
            pl.BlockSpec((D_MODEL, tm), lambda i: (0, i)),
            pl.BlockSpec((PEER_KEYS + 2 * I1_PER_TILE, PEER_HEADS, tm), hk_map), pl.BlockSpec(hk, hk_map),
            pl.BlockSpec((PEER_HEADS, tm), lambda i: (0, i)),
        ],
        out_shape=[
            jax.ShapeDtypeStruct((D_MODEL, n), BF16),
            jax.ShapeDtypeStruct((PEER_KEYS + 2 * I1_PER_TILE, PEER_HEADS, n), F32), hk_shape,
            jax.ShapeDtypeStruct((PEER_HEADS, n), F32),
        ],
        compiler_params=_cparams("parallel"),
        name="peer_scores",
    )(x, wq_bf, keys_bf)


def _gelu_exact(x):
    return 0.5 * x * (1.0 + lax.erf(x * (2.0 ** -0.5)))


N_PAIRS = PEER_EXPERTS // (2 * PEER_TE)


def _dense_pipeline_step(p, u_ref, v_ref, xt_ref, s1l_ref, s2l_ref, tau_ref, o_ref, hid, at):
    tm = xt_ref.shape[1]

    @pl.when(p == 0)
    def _():
        o_ref[...] = jnp.zeros_like(o_ref)
        for buf in hid + at:
            buf[...] = jnp.zeros_like(buf)

    for half in range(2):
        rows = slice(half * PEER_TE, (half + 1) * PEER_TE)
        hid[half][...] = jnp.dot(u_ref[rows, :].astype(BF16), xt_ref[...], preferred_element_type=F32)

        src, dst = hid[1 - half], at[1 - half]
        for il in range(I1_PER_TILE):
            r = half * I1_PER_TILE + il
            rs = slice(il * PEER_KEYS, (il + 1) * PEER_KEYS)
            for c in range(tm // LANES):
                cs = slice(c * LANES, (c + 1) * LANES)
                gate = jnp.zeros((PEER_KEYS, LANES), F32)
                for h in range(PEER_HEADS):
                    pair = s2l_ref[h, :, cs] + s1l_ref[r, h:h + 1, cs]
                    gate = gate + jnp.where(pair >= tau_ref[h:h + 1, cs], jnp.exp2(pair), 0.0)
                dst[rs, cs] = (_gelu_exact(src[rs, cs]) * gate).astype(BF16)

        o_ref[...] += lax.dot_general(at[half][...], v_ref[rows, :].astype(BF16), (((0,), (0,)), ((), ())),
                                      preferred_element_type=F32)


def _peer_dense_kernel(u_ref, v_ref, xt_ref, s1l_ref, s2l_ref, tau_ref, xts_ref, s1ls_ref, s2ls_ref, taus_ref,
                       o_ref, os_ref, hid0, hid1, at0, at1, hids0, hids1, ats0, ats1):
    t = pl.program_id(0)
    p = pl.program_id(1)
    _dense_pipeline_step(p, u_ref, v_ref, xt_ref, s1l_ref, s2l_ref, tau_ref, o_ref, (hid0, hid1), (at0, at1))

    @pl.when(t == 0)
    def _():
        _dense_pipeline_step(p, u_ref, v_ref, xts_ref, s1ls_ref, s2ls_ref, taus_ref, os_ref,
                             (hids0, hids1), (ats0, ats1))


def _peer_dense(u_all, v_all, layer, prompt, sample, tm):
    n = prompt[0].shape[1]
    n_s = sample[0].shape[1]
    u_spec = pl.BlockSpec((None, 2 * PEER_TE, D_MODEL), lambda t, p: (layer, jnp.minimum(p, N_PAIRS - 1), 0))
    v_spec = pl.BlockSpec((None, 2 * PEER_TE, D_MODEL), lambda t, p: (layer, jnp.maximum(p - 1, 0), 0))

    def token_specs(width, tile_of):
        return [
            pl.BlockSpec((D_MODEL, width), lambda t, p: (0, tile_of(t))),
            pl.BlockSpec((2 * I1_PER_TILE, PEER_HEADS, width), lambda t, p: (p, 0, tile_of(t))),
            pl.BlockSpec((PEER_HEADS, PEER_KEYS, width), lambda t, p: (0, 0, tile_of(t)),
                         pipeline_mode=pl.Buffered(1)),
            pl.BlockSpec((PEER_HEADS, width), lambda t, p: (0, tile_of(t))),
        ]

    def buffers(width):
        return [pltpu.VMEM((PEER_TE, width), F32), pltpu.VMEM((PEER_TE, width), F32),
                pltpu.VMEM((PEER_TE, width), BF16), pltpu.VMEM((PEER_TE, width), BF16)]

    return pl.pallas_call(
        _peer_dense_kernel,
        grid=(n // tm, N_PAIRS + 1),
        in_specs=[u_spec, v_spec] + token_specs(tm, lambda t: t) + token_specs(n_s, lambda t: 0),
        out_specs=[pl.BlockSpec((tm, D_MODEL), lambda t, p: (t, 0)),
                   pl.BlockSpec((n_s, D_MODEL), lambda t, p: (0, 0))],
        out_shape=[jax.ShapeDtypeStruct((n, D_MODEL), F32), jax.ShapeDtypeStruct((n_s, D_MODEL), F32)],
        scratch_shapes=buffers(tm) + buffers(n_s),
        compiler_params=_cparams("arbitrary", "arbitrary"),
        name="peer_dense",
    )(u_all, v_all, *prompt, *sample)


def _res_ln_kernel(x_ref, f_ref, g_ref, beta_ref, o_ref):
    o_ref[...] = _layer_norm(DEEPNORM_ALPHA * x_ref[...] + f_ref[...], g_ref[...], beta_ref[...])


def _res_ln(x, f, g, beta, tm):
    n = x.shape[0]
    tile = pl.BlockSpec((tm, D_MODEL), lambda i: (i, 0))
    vec = pl.BlockSpec((1, D_MODEL), lambda i: (0, 0))
    return pl.pallas_call(
        _res_ln_kernel,
        grid=(n // tm,),
        in_specs=[tile, tile, vec, vec],
        out_specs=tile,
        out_shape=jax.ShapeDtypeStruct((n, D_MODEL), F32),
        compiler_params=_cparams("parallel"),
        name="res_ln",
    )(x, f, g.reshape(1, D_MODEL), beta.reshape(1, D_MODEL))


def _peer_layer(xp, xs, wq_bf, keys_bf, u_all, v_all, layer, g, beta):
    prompt = _peer_scores(xp, wq_bf, keys_bf, 256)
    sample = _peer_scores(xs, wq_bf, keys_bf, xs.shape[0])
    fp, fs = _peer_dense(u_all, v_all, layer, prompt, sample, 1024)
    return _res_ln(xp, fp, g, beta, 512), _res_ln(xs, fs, g, beta, xs.shape[0])


def _rope_tables(pos):
    half = HEAD_DIM // 2
    inv = jnp.power(ROPE_THETA, -jnp.arange(half, dtype=F32) * 2.0 / HEAD_DIM)
    ang = pos.astype(F32)[:, None] * inv[None, :]
    cos = jnp.cos(ang)
    sin = jnp.sin(ang)
    reps = LANES // HEAD_DIM
    return (jnp.tile(jnp.concatenate([cos, cos], axis=1), (1, reps)),
            jnp.tile(jnp.concatenate([-sin, sin], axis=1), (1, reps)))


def _shifted(state, u3):
    t = u3.shape[1]
    up = jnp.concatenate([state, u3], axis=1)
    return up[:, 1:t + 1], up[:, 0:t], up[:, -(CONV_WIDTH - 1):]


def kernel(x_prompt, x_sample, cache_k, cache_v, state_conv, w_qkv, b_qkv, w_o, b_o, attn_sinks,
           w_conv_in, conv_w, w_conv_out, w_peer_q, peer_sub_keys, peer_u, peer_v, ln_g, ln_b):
    batch, seq, _ = x_prompt.shape
    dec_batch, dec_seq, _ = x_sample.shape
    n_p = batch * seq
    n_s = dec_batch * dec_seq
    xp = x_prompt.reshape(n_p, D_MODEL)
    xs = x_sample.reshape(n_s, D_MODEL)
    cos_p, sin_p = _rope_tables(jnp.tile(jnp.arange(seq), batch))
    cos_s, sin_s = _rope_tables(jnp.tile(PAST_LEN + jnp.arange(dec_seq), dec_batch))

    new_kp, new_vp, new_cp, new_ks, new_vs, new_cs = [], [], [], [], [], []
    for layer in range(DEPTH):
        j = layer // 2
        g0, b0 = ln_g[layer, 0], ln_b[layer, 0]
        if layer % 2 == 0:
            wqkv = w_qkv[j].astype(BF16)
            wo = w_o[j].astype(BF16)
            sinks = attn_sinks[j].reshape(N_Q_HEADS)
            qkv_p = _qkv_rope(xp, wqkv, b_qkv[j], cos_p, sin_p, 1024)
            qkv_s = _qkv_rope(xs, wqkv, b_qkv[j], cos_s, sin_s, n_s)
            o_p = _attn_prompt(qkv_p, sinks, batch, seq)
            qkv_s3 = qkv_s.reshape(dec_batch, dec_seq, QKV_COLS)
            qkv_s8 = jnp.pad(qkv_s3, ((0, 0), (0, SAMPLE_ROWS - dec_seq), (0, 0)))
            ck = cache_k[j].reshape(dec_batch, WINDOW, KV_COLS)
            cv = cache_v[j].reshape(dec_batch, WINDOW, KV_COLS)
            o_s = _attn_sample(qkv_s8, ck, cv, sinks, dec_seq)[:, :dec_seq].reshape(n_s, Q_COLS)
            kv_p = qkv_p.reshape(batch, seq, QKV_COLS)[:, -WINDOW:, Q_COLS:]
            new_kp.append(kv_p[..., :KV_COLS].reshape(batch, WINDOW, N_KV_HEADS, HEAD_DIM))
            new_vp.append(kv_p[..., KV_COLS:].reshape(batch, WINDOW, N_KV_HEADS, HEAD_DIM))
            k_s = jnp.concatenate([ck, qkv_s3[..., Q_COLS:Q_COLS + KV_COLS]], axis=1)[:, -WINDOW:]
            v_s = jnp.concatenate([cv, qkv_s3[..., Q_COLS + KV_COLS:]], axis=1)[:, -WINDOW:]
            new_ks.append(k_s.reshape(dec_batch, WINDOW, N_KV_HEADS, HEAD_DIM))
            new_vs.append(v_s.reshape(dec_batch, WINDOW, N_KV_HEADS, HEAD_DIM))
            xp = _proj_ln(o_p, wo, b_o[j], xp, g0, b0, 512)
            xs = _proj_ln(o_s, wo, b_o[j], xs, g0, b0, n_s)
        else:
            w_in = w_conv_in[j].astype(BF16)
            w_out = w_conv_out[j].astype(BF16)
            gb_p, u_p = _conv_in(xp, w_in, 1024)
            gb_s, u_s = _conv_in(xs, w_in, n_s)
            um1_s, um2_s, cs = _shifted(state_conv[j], u_s.reshape(dec_batch, dec_seq, D_MODEL))
            new_cp.append(u_p.reshape(batch, seq, D_MODEL)[:, -(CONV_WIDTH - 1):])
            new_cs.append(cs)
            xp = _conv_out_ln(gb_p, u_p, None, conv_w[j], w_out, xp, g0, b0, 256, seq=seq)
            xs = _conv_out_ln(gb_s, u_s, (um1_s.reshape(n_s, D_MODEL), um2_s.reshape(n_s, D_MODEL)),
                              conv_w[j], w_out, xs, g0, b0, n_s)
        wq = w_peer_q[layer].astype(BF16)
        keys = peer_sub_keys[layer].astype(BF16)
        g1, b1 = ln_g[layer, 1], ln_b[layer, 1]
        xp, xs = _peer_layer(xp, xs, wq, keys, peer_u, peer_v, layer, g1, b1)

    return (xp.reshape(batch, seq, D_MODEL), xs.reshape(dec_batch, dec_seq, D_MODEL),
            jnp.stack(new_kp), jnp.stack(new_vp), jnp.stack(new_cp),
            jnp.stack(new_ks), jnp.stack(new_vs), jnp.stack(new_cs))
```

```python
import functools

import jax
import jax.numpy as jnp
from jax import lax
from jax.experimental import pallas as pl
from jax.experimental.pallas import tpu as pltpu

F32 = jnp.float32
BF16 = jnp.bfloat16

D_MODEL = 2048
DEPTH = 4
HEAD_DIM = 64
N_Q_HEADS = 32
N_KV_HEADS = 8
GQA_GROUP = 4
WINDOW = 128
ROPE_THETA = 10000.0
PAST_LEN = 16384
CONV_WIDTH = 3
PEER_HEADS = 8
PEER_KEYS = 128
PEER_EXPERTS = PEER_KEYS * PEER_KEYS
PEER_HALF = 128
PEER_TOPK = 16
LN_EPS = 1e-5
NEG_INF = -1e30
DEEPNORM_ALPHA = (2 * DEPTH) ** 0.25
LOG2E = 1.4426950408889634

Q_COLS = N_Q_HEADS * HEAD_DIM
KV_COLS = N_KV_HEADS * HEAD_DIM
QKV_COLS = Q_COLS + 2 * KV_COLS

LANES = 128
PEER_TE = 256
I1_PER_TILE = PEER_TE // PEER_KEYS
VMEM_LIMIT_BYTES = 56 * 1024 * 1024


def _cparams(*semantics):
    return pltpu.CompilerParams(dimension_semantics=semantics, vmem_limit_bytes=VMEM_LIMIT_BYTES)


def _layer_norm(z, g, b):
    mu = jnp.mean(z, axis=-1, keepdims=True)
    zc = z - mu
    var = jnp.mean(zc * zc, axis=-1, keepdims=True)
    return zc * lax.rsqrt(var + LN_EPS) * g + b


QKV_TN = 512
N_ROPE_TILES = (Q_COLS + KV_COLS) // QKV_TN


def _qkv_kernel(x_ref, w_ref, b_ref, cos_ref, sin_ref, o_ref):
    j = pl.program_id(1)
    acc = jnp.dot(x_ref[...].astype(BF16), w_ref[...], preferred_element_type=F32) + b_ref[...]

    @pl.when(j < N_ROPE_TILES)
    def _():
        cos = cos_ref[...]
        sin = sin_ref[...]
        lane = lax.broadcasted_iota(jnp.int32, cos.shape, 1)
        first_half = (lane % HEAD_DIM) < (HEAD_DIM // 2)
        for c in range(QKV_TN // LANES):
            seg = acc[:, c * LANES:(c + 1) * LANES]
            partner = jnp.where(first_half,
                                pltpu.roll(seg, LANES - HEAD_DIM // 2, 1),
                                pltpu.roll(seg, HEAD_DIM // 2, 1))
            o_ref[:, c * LANES:(c + 1) * LANES] = seg * cos + partner * sin

    @pl.when(j >= N_ROPE_TILES)
    def _():
        o_ref[...] = acc


def _qkv_rope(x, w_bf, b, cos, sin, tm):
    n = x.shape[0]
    return pl.pallas_call(
        _qkv_kernel,
        grid=(n // tm, QKV_COLS // QKV_TN),
        in_specs=[
            pl.BlockSpec((tm, D_MODEL), lambda i, j: (i, 0)),
            pl.BlockSpec((D_MODEL, QKV_TN), lambda i, j: (0, j)),
            pl.BlockSpec((1, QKV_TN), lambda i, j: (0, j)),
            pl.BlockSpec((tm, LANES), lambda i, j: (i, 0)),
            pl.BlockSpec((tm, LANES), lambda i, j: (i, 0)),
        ],
        out_specs=pl.BlockSpec((tm, QKV_TN), lambda i, j: (i, j)),
        out_shape=jax.ShapeDtypeStruct((n, QKV_COLS), F32),
        compiler_params=_cparams("parallel", "arbitrary"),
        name="qkv_rope",
    )(x, w_bf, b.reshape(1, QKV_COLS), cos, sin)


def _sink_column(sink_ref, g, rows):
    return jnp.concatenate(
        [jnp.full((rows, 1), sink_ref[g * GQA_GROUP + j], F32) for j in range(GQA_GROUP)], axis=0)


def _attn_prompt_kernel(sink_ref, q_ref, kp_ref, kc_ref, vp_ref, vc_ref, o_ref):
    i = pl.program_id(1)
    cols = GQA_GROUP * WINDOW
    kj = lax.broadcasted_iota(jnp.int32, (2 * WINDOW, cols), 0)
    qi = lax.broadcasted_iota(jnp.int32, (2 * WINDOW, cols), 1) % WINDOW
    dist = qi + WINDOW - kj
    prev_ok = jnp.where(i > 0, 0, WINDOW)
    mask = (dist >= 0) & (dist <= WINDOW) & (kj >= prev_ok)
    nt = (((1,), (1,)), ((), ()))
    tn = (((0,), (0,)), ((), ()))
    for g in range(N_KV_HEADS):
        ks = slice(g * HEAD_DIM, (g + 1) * HEAD_DIM)
        k = jnp.concatenate([kp_ref[:, ks], kc_ref[:, ks]], axis=0).astype(BF16)
        v = jnp.concatenate([vp_ref[:, ks], vc_ref[:, ks]], axis=0).astype(BF16)
        q = jnp.concatenate(
            [q_ref[:, (g * GQA_GROUP + j) * HEAD_DIM:(g * GQA_GROUP + j + 1) * HEAD_DIM]
             for j in range(GQA_GROUP)], axis=0).astype(BF16)
        t = lax.dot_general(k, q, nt, preferred_element_type=F32)
        t = jnp.where(mask, t * (HEAD_DIM ** -0.5 * LOG2E), NEG_INF)
        sink = jnp.concatenate(
            [jnp.full((1, WINDOW), sink_ref[g * GQA_GROUP + j] * LOG2E, F32) for j in range(GQA_GROUP)], axis=1)
        m = jnp.maximum(jnp.max(t, axis=0, keepdims=True), sink)
        p = jnp.exp2(t - m)
        denom = jnp.sum(p, axis=0, keepdims=True) + jnp.exp2(sink - m)
        o_t = lax.dot_general(v, p.astype(BF16), tn, preferred_element_type=F32) / denom
        o = o_t.T
        o_ref[:, g * GQA_GROUP * HEAD_DIM:(g + 1) * GQA_GROUP * HEAD_DIM] = jnp.concatenate(
            [o[j * WINDOW:(j + 1) * WINDOW] for j in range(GQA_GROUP)], axis=1).astype(o_ref.dtype)


def _attn_prompt(qkv, sinks, batch, seq):
    nb = seq // WINDOW
    kcol = Q_COLS // KV_COLS
    vcol = kcol + 1

    def cur(b, i):
        return b * nb + i

    def prev(b, i):
        return b * nb + jnp.maximum(i - 1, 0)

    return pl.pallas_call(
        _attn_prompt_kernel,
        grid=(batch, nb),
        in_specs=[
            pl.BlockSpec(memory_space=pltpu.SMEM),
            pl.BlockSpec((WINDOW, Q_COLS), lambda b, i: (cur(b, i), 0)),
            pl.BlockSpec((WINDOW, KV_COLS), lambda b, i: (prev(b, i), kcol)),
            pl.BlockSpec((WINDOW, KV_COLS), lambda b, i: (cur(b, i), kcol)),
            pl.BlockSpec((WINDOW, KV_COLS), lambda b, i: (prev(b, i), vcol)),
            pl.BlockSpec((WINDOW, KV_COLS), lambda b, i: (cur(b, i), vcol)),
        ],
        out_specs=pl.BlockSpec((WINDOW, Q_COLS), lambda b, i: (cur(b, i), 0)),
        out_shape=jax.ShapeDtypeStruct((batch * seq, Q_COLS), BF16),
        compiler_params=_cparams("parallel", "arbitrary"),
        name="attn_prompt",
    )(sinks, qkv, qkv, qkv, qkv, qkv)


SAMPLE_ROWS = 8


def _attn_sample_kernel(sink_ref, qkv_ref, ck_ref, cv_ref, o_ref, *, n_new):
    qkv = qkv_ref[0]
    rows = GQA_GROUP * SAMPLE_ROWS
    tok = lax.broadcasted_iota(jnp.int32, (rows, WINDOW), 0) % SAMPLE_ROWS
    kj = lax.broadcasted_iota(jnp.int32, (rows, WINDOW), 1)
    mask_c = kj >= tok
    tok1 = lax.broadcasted_iota(jnp.int32, (rows, 1), 0) % SAMPLE_ROWS
    for g in range(N_KV_HEADS):
        ks = slice(g * HEAD_DIM, (g + 1) * HEAD_DIM)
        kc = ck_ref[0, :, ks].astype(BF16)
        vc = cv_ref[0, :, ks].astype(BF16)
        kn = qkv[:, Q_COLS + g * HEAD_DIM:Q_COLS + (g + 1) * HEAD_DIM]
        vn = qkv[:, Q_COLS + KV_COLS + g * HEAD_DIM:Q_COLS + KV_COLS + (g + 1) * HEAD_DIM]
        q = jnp.concatenate(
            [qkv[:, (g * GQA_GROUP + j) * HEAD_DIM:(g * GQA_GROUP + j + 1) * HEAD_DIM]
             for j in range(GQA_GROUP)], axis=0)
        s_c = lax.dot_general(q.astype(BF16), kc, (((1,), (1,)), ((), ())), preferred_element_type=F32)
        s_c = jnp.where(mask_c, s_c * (HEAD_DIM ** -0.5), NEG_INF)
        s_n = []
        for r in range(n_new):
            s_r = jnp.sum(q * kn[r:r + 1, :], axis=1, keepdims=True) * (HEAD_DIM ** -0.5)
            s_n.append(jnp.where(tok1 >= r, s_r, NEG_INF))
        sink = _sink_column(sink_ref, g, SAMPLE_ROWS)
        m = jnp.maximum(jnp.max(s_c, axis=1, keepdims=True), sink)
        for s_r in s_n:
            m = jnp.maximum(m, s_r)
        p_c = jnp.exp(s_c - m)
        denom = jnp.sum(p_c, axis=1, keepdims=True) + jnp.exp(sink - m)
        o = jnp.dot(p_c.astype(BF16), vc, preferred_element_type=F32)
        for r, s_r in enumerate(s_n):
            p_r = jnp.exp(s_r - m)
            denom = denom + p_r
            o = o + p_r * vn[r:r + 1, :]
        o = o / denom
        o_ref[0, :, g * GQA_GROUP * HEAD_DIM:(g + 1) * GQA_GROUP * HEAD_DIM] = jnp.concatenate(
            [o[j * SAMPLE_ROWS:(j + 1) * SAMPLE_ROWS] for j in range(GQA_GROUP)], axis=1)


def _attn_sample(qkv8, ck, cv, sinks, n_new):
    nb = qkv8.shape[0]
    return pl.pallas_call(
        functools.partial(_attn_sample_kernel, n_new=n_new),
        grid=(nb,),
        in_specs=[
            pl.BlockSpec(memory_space=pltpu.SMEM),
            pl.BlockSpec((1, SAMPLE_ROWS, QKV_COLS), lambda b: (b, 0, 0)),
            pl.BlockSpec((1, WINDOW, KV_COLS), lambda b: (b, 0, 0)),
            pl.BlockSpec((1, WINDOW, KV_COLS), lambda b: (b, 0, 0)),
        ],
        out_specs=pl.BlockSpec((1, SAMPLE_ROWS, Q_COLS), lambda b: (b, 0, 0)),
        out_shape=jax.ShapeDtypeStruct((nb, SAMPLE_ROWS, Q_COLS), F32),
        compiler_params=_cparams("parallel"),
        name="attn_sample",
    )(sinks, qkv8, ck, cv)


def _proj_ln_kernel(a_ref, w_ref, b_ref, x_ref, g_ref, beta_ref, o_ref):
    y = jnp.dot(a_ref[...].astype(BF16), w_ref[...], preferred_element_type=F32) + b_ref[...]
    o_ref[...] = _layer_norm(DEEPNORM_ALPHA * x_ref[...] + y, g_ref[...], beta_ref[...])


def _proj_ln(a, w_bf, bias, x, g, beta, tm):
    n, k = a.shape
    row = lambda i: (i, 0)
    fixed = lambda i: (0, 0)
    return pl.pallas_call(
        _proj_ln_kernel,
        grid=(n // tm,),
        in_specs=[
            pl.BlockSpec((tm, k), row),
            pl.BlockSpec((k, D_MODEL), fixed),
            pl.BlockSpec((1, D_MODEL), fixed),
            pl.BlockSpec((tm, D_MODEL), row),
            pl.BlockSpec((1, D_MODEL), fixed),
            pl.BlockSpec((1, D_MODEL), fixed),
        ],
        out_specs=pl.BlockSpec((tm, D_MODEL), row),
        out_shape=jax.ShapeDtypeStruct((n, D_MODEL), F32),
        compiler_params=_cparams("parallel"),
        name="proj_ln",
    )(a, w_bf, bias.reshape(1, D_MODEL), x, g.reshape(1, D_MODEL), beta.reshape(1, D_MODEL))


CONV_TN = 512


def _conv_in_kernel(x_ref, wb_ref, wc_ref, wh_ref, gb_ref, u_ref):
    xb = x_ref[...].astype(BF16)
    gb_ref[...] = jnp.dot(xb, wb_ref[...], preferred_element_type=F32)
    gate_c = jnp.dot(xb, wc_ref[...], preferred_element_type=F32)
    h = jnp.dot(xb, wh_ref[...], preferred_element_type=F32)
    u_ref[...] = gate_c * h


def _conv_in(x, w_in_bf, tm):
    n = x.shape[0]
    nt = D_MODEL // CONV_TN
    return pl.pallas_call(
        _conv_in_kernel,
        grid=(n // tm, nt),
        in_specs=[
            pl.BlockSpec((tm, D_MODEL), lambda i, j: (i, 0)),
            pl.BlockSpec((D_MODEL, CONV_TN), lambda i, j: (0, j)),
            pl.BlockSpec((D_MODEL, CONV_TN), lambda i, j: (0, nt + j)),
            pl.BlockSpec((D_MODEL, CONV_TN), lambda i, j: (0, 2 * nt + j)),
        ],
        out_specs=[
            pl.BlockSpec((tm, CONV_TN), lambda i, j: (i, j)),
            pl.BlockSpec((tm, CONV_TN), lambda i, j: (i, j)),
        ],
        out_shape=[jax.ShapeDtypeStruct((n, D_MODEL), F32)] * 2,
        compiler_params=_cparams("parallel", "arbitrary"),
        name="conv_in",
    )(x, w_in_bf, w_in_bf, w_in_bf)


def _gated_out_ln(gate_b, conv, w_ref, x_ref, g_ref, beta_ref, o_ref):
    y = jnp.dot((gate_b * conv).astype(BF16), w_ref[...], preferred_element_type=F32)
    o_ref[...] = _layer_norm(DEEPNORM_ALPHA * x_ref[...] + y, g_ref[...], beta_ref[...])


def _conv_out_ln_kernel(gb_ref, u_ref, um1_ref, um2_ref, cw_ref, w_ref, x_ref, g_ref, beta_ref, o_ref):
    conv = um2_ref[...] * cw_ref[0:1, :]
    conv = conv + um1_ref[...] * cw_ref[1:2, :]
    conv = conv + u_ref[...] * cw_ref[2:3, :]
    _gated_out_ln(gb_ref[...], conv, w_ref, x_ref, g_ref, beta_ref, o_ref)


def _conv_out_ln_halo_kernel(gb_ref, u_ref, halo_ref, cw_ref, w_ref, x_ref, g_ref, beta_ref, o_ref,
                             *, tiles_per_seq):
    i = pl.program_id(0)
    u = u_ref[...]
    halo = jnp.where(i % tiles_per_seq == 0, 0.0, halo_ref[...])
    row = lax.broadcasted_iota(jnp.int32, u.shape, 0)
    um1 = jnp.where(row == 0, halo[7:8, :], pltpu.roll(u, 1, 0))
    um2 = jnp.where(row == 0, halo[6:7, :], jnp.where(row == 1, halo[7:8, :], pltpu.roll(u, 2, 0)))
    conv = um2 * cw_ref[0:1, :] + um1 * cw_ref[1:2, :] + u * cw_ref[2:3, :]
    _gated_out_ln(gb_ref[...], conv, w_ref, x_ref, g_ref, beta_ref, o_ref)


def _conv_out_ln(gate_b, u, shifted, conv_w, w_out_bf, x, g, beta, tm, seq=None):
    n = x.shape[0]
    row = lambda i: (i, 0)
    fixed = lambda i: (0, 0)
    tile = pl.BlockSpec((tm, D_MODEL), row)
    vec = pl.BlockSpec((1, D_MODEL), fixed)
    tail_specs = [pl.BlockSpec((CONV_WIDTH, D_MODEL), fixed), pl.BlockSpec((D_MODEL, D_MODEL), fixed),
                  tile, vec, vec]
    tail_args = (conv_w, w_out_bf, x, g.reshape(1, D_MODEL), beta.reshape(1, D_MODEL))
    if shifted is None:
        body = functools.partial(_conv_out_ln_halo_kernel, tiles_per_seq=seq // tm)
        halo = pl.BlockSpec((8, D_MODEL), lambda i: (jnp.maximum(i * (tm // 8) - 1, 0), 0))
        in_specs, args = [tile, tile, halo], (gate_b, u, u)
    else:
        body = _conv_out_ln_kernel
        in_specs, args = [tile, tile, tile, tile], (gate_b, u) + tuple(shifted)
    return pl.pallas_call(
        body,
        grid=(n // tm,),
        in_specs=in_specs + tail_specs,
        out_specs=tile,
        out_shape=jax.ShapeDtypeStruct((n, D_MODEL), F32),
        compiler_params=_cparams("parallel"),
        name="conv_out_ln",
    )(*args, *tail_args)


def _oddeven_merge_sort_pairs(n):
    pairs = []
    p = 1
    while p < n:
        k = p
        while k >= 1:
            for j in range(k % p, n - k, 2 * k):
                for i in range(min(k, n - j - k)):
                    if (i + j) // (2 * p) == (i + j + k) // (2 * p):
                        pairs.append((i + j, i + j + k))
            k //= 2
        p *= 2
    return pairs


def _top_values(pieces, k):
    n = len(pieces)
    size = 1
    while size < n:
        size *= 2
    v = list(pieces) + [jnp.full_like(pieces[0], -jnp.inf)] * (size - n)
    for i, j in _oddeven_merge_sort_pairs(size):
        v[i], v[j] = jnp.maximum(v[i], v[j]), jnp.minimum(v[i], v[j])
    v = v[:n]
    vals = []
    for it in range(k):
        m = jnp.max(v[0], axis=0, keepdims=True)
        vals.append(m)
        need = k - it - 1
        if need:
            hit = v[0] == m
            nxt = v[1:] + [jnp.full_like(v[0], -jnp.inf)]
            v = [jnp.where(hit, nxt[j], v[j]) for j in range(min(need, len(v)))]
    return vals


def _peer_score_kernel(x_ref, wq_ref, keys_ref, xt_ref, s1l_ref, s2l_ref, tau_ref):
    pad = jnp.zeros((I1_PER_TILE,) + s1l_ref.shape[1:], F32)
    s1l_ref[:I1_PER_TILE] = pad
    s1l_ref[I1_PER_TILE + PEER_KEYS:] = pad
    x = x_ref[...]
    xt_ref[...] = x.T.astype(BF16)
    q = jnp.dot(x.astype(BF16), wq_ref[...], preferred_element_type=F32).astype(BF16)
    nt = (((1,), (1,)), ((), ()))
    k = PEER_TOPK
    for h in range(PEER_HEADS):
        c0 = h * 2 * PEER_HALF
        s1 = lax.dot_general(keys_ref[h, 0], q[:, c0:c0 + PEER_HALF], nt, preferred_element_type=F32)
        s2 = lax.dot_general(keys_ref[h, 1], q[:, c0 + PEER_HALF:c0 + 2 * PEER_HALF], nt,
                             preferred_element_type=F32)
        a = _top_values([s1[r:r + 8] for r in range(0, PEER_KEYS, 8)], k + 1)
        b = _top_values([s2[r:r + 8] for r in range(0, PEER_KEYS, 8)], k + 1)
        neg = jnp.full_like(a[0], -jnp.inf)
        b24 = jnp.concatenate(b + [neg] * 7, axis=0)
        b8 = jnp.concatenate(b[:8], axis=0)
        a16 = jnp.concatenate(a[8:] + [neg] * 7, axis=0)
        row = lax.broadcasted_iota(jnp.int32, b8.shape, 0)
        eb24 = jnp.exp(b24 - b[0])
        eb8 = jnp.exp(b8 - b[0])
        cand = [a[0] + b24[r:r + 8] for r in (0, 8, 16)]
        wgt = [eb24[r:r + 8] for r in (0, 8, 16)]
        for i in range(1, 8):
            cand.append(jnp.where(row < (k + 1) // (i + 1), a[i] + b8, -jnp.inf))
            wgt.append(jnp.exp(a[i] - a[0]) * eb8)
        cand += [a16[r:r + 8] + b[0] for r in (0, 8)]
        wgt += [jnp.exp(a16[r:r + 8] - a[0]) for r in (0, 8)]
        tops = _top_values(cand, k + 1)
        thr = 0.5 * (tops[k - 1] + tops[k])
        z8 = functools.reduce(jnp.add, [jnp.where(c >= thr, w, 0.0) for c, w in zip(cand, wgt)])
        z = jnp.sum(z8, axis=0, keepdims=True)
        shift = a[0] + b[0] + jnp.log(z)
        s1l_ref[I1_PER_TILE:I1_PER_TILE + PEER_KEYS, h, :] = (s1 - shift) * LOG2E
        s2l_ref[h] = s2 * LOG2E
        tau_ref[h:h + 1, :] = (thr - shift) * LOG2E


def _peer_scores(x, wq_bf, keys_bf, tm):
    n = x.shape[0]
    hk = (PEER_HEADS, PEER_KEYS, tm)
    hk_map = lambda i: (0, 0, i)
    hk_shape = jax.ShapeDtypeStruct((PEER_HEADS, PEER_KEYS, n), F32)
    return pl.pallas_call(
        _peer_score_kernel,
        grid=(n // tm,),
        in_specs=[
            pl.BlockSpec((tm, D_MODEL), lambda i: (i, 0)),
            pl.BlockSpec((D_MODEL, PEER_HEADS * 2 * PEER_HALF), lambda i: (0, 0)),
            pl.BlockSpec((PEER_HEADS, 2, PEER_KEYS, PEER_HALF), lambda i: (0, 0, 0, 0)),
        ],
        out_specs=[
            pl.BlockSpec((D_MODEL, tm), lambda i: (0, i)),
            pl.BlockSpec((PEER_KEYS + 2 * I1_PER_TILE, PEER_HEADS, tm), hk_map), pl.BlockSpec(hk, hk_map),
            pl.BlockSpec((PEER_HEADS, tm), lambda i: (0, i)),
        ],
        out_shape=[
            jax.ShapeDtypeStruct((D_MODEL, n), BF16),
            jax.ShapeDtypeStruct((PEER_KEYS + 2 * I1_PER_TILE, PEER_HEADS, n), F32), hk_shape,
            jax.ShapeDtypeStruct((PEER_HEADS, n), F32),
        ],
        compiler_params=_cparams("parallel"),
        name="peer_scores",
    )(x, wq_bf, keys_bf)


def _gelu_exact(x):
    return 0.5 * x * (1.0 + lax.erf(x * (2.0 ** -0.5)))


N_PAIRS = PEER_EXPERTS // (2 * PEER_TE)


def _dense_pipeline_step(p, u_ref, v_ref, xt_ref, s1l_ref, s2l_ref, tau_ref, o_ref, hid, at):
    tm = xt_ref.shape[1]

    @pl.when(p == 0)
    def _():
        o_ref[...] = jnp.zeros_like(o_ref)
        for buf in hid + at:
            buf[...] = jnp.zeros_like(buf)

    for half in range(2):
        rows = slice(half * PEER_TE, (half + 1) * PEER_TE)
        hid[half][...] = jnp.dot(u_ref[rows, :].astype(BF16), xt_ref[...], preferred_element_type=F32)

        src, dst = hid[1 - half], at[1 - half]
        for il in range(I1_PER_TILE):
            r = half * I1_PER_TILE + il
            rs = slice(il * PEER_KEYS, (il + 1) * PEER_KEYS)
            for c in range(tm // LANES):
                cs = slice(c * LANES, (c + 1) * LANES)
                gate = jnp.zeros((PEER_KEYS, LANES), F32)
                for h in range(PEER_HEADS):
                    pair = s2l_ref[h, :, cs] + s1l_ref[r, h:h + 1, cs]
                    gate = gate + jnp.where(pair >= tau_ref[h:h + 1, cs], jnp.exp2(pair), 0.0)
                dst[rs, cs] = (_gelu_exact(src[rs, cs]) * gate).astype(BF16)

        o_ref[...] += lax.dot_general(at[half][...], v_ref[rows, :].astype(BF16), (((0,), (0,)), ((), ())),
                                      preferred_element_type=F32)


def _finish_ln(x_ref, g_ref, beta_ref, o_ref):
    rows = min(o_ref.shape[0], 256)
    for r in range(0, o_ref.shape[0], rows):
        rs = slice(r, r + rows)
        o_ref[rs, :] = _layer_norm(DEEPNORM_ALPHA * x_ref[rs, :] + o_ref[rs, :], g_ref[...], beta_ref[...])


def _peer_dense_kernel(u_ref, v_ref, g_ref, beta_ref,
                       x_ref, xt_ref, s1l_ref, s2l_ref, tau_ref,
                       xs_ref, xts_ref, s1ls_ref, s2ls_ref, taus_ref,
                       o_ref, os_ref, hid0, hid1, at0, at1, hids0, hids1, ats0, ats1):
    t = pl.program_id(0)
    p = pl.program_id(1)
    last = pl.num_programs(1) - 1
    _dense_pipeline_step(p, u_ref, v_ref, xt_ref, s1l_ref, s2l_ref, tau_ref, o_ref, (hid0, hid1), (at0, at1))

    @pl.when(p == last)
    def _():
        _finish_ln(x_ref, g_ref, beta_ref, o_ref)

    @pl.when(t == 0)
    def _():
        _dense_pipeline_step(p, u_ref, v_ref, xts_ref, s1ls_ref, s2ls_ref, taus_ref, os_ref,
                             (hids0, hids1), (ats0, ats1))

        @pl.when(p == last)
        def _():
            _finish_ln(xs_ref, g_ref, beta_ref, os_ref)


def _peer_dense(u_all, v_all, layer, g, beta, prompt, sample, tm):
    n = prompt[0].shape[0]
    n_s = sample[0].shape[0]
    vec = pl.BlockSpec((1, D_MODEL), lambda t, p: (0, 0))
    u_spec = pl.BlockSpec((None, 2 * PEER_TE, D_MODEL), lambda t, p: (layer, jnp.minimum(p, N_PAIRS - 1), 0))
    v_spec = pl.BlockSpec((None, 2 * PEER_TE, D_MODEL), lambda t, p: (layer, jnp.maximum(p - 1, 0), 0))

    def token_specs(width, tile_of):
        return [
            pl.BlockSpec((width, D_MODEL), lambda t, p: (tile_of(t), 0), pipeline_mode=pl.Buffered(1)),
            pl.BlockSpec((D_MODEL, width), lambda t, p: (0, tile_of(t)), pipeline_mode=pl.Buffered(1)),
            pl.BlockSpec((2 * I1_PER_TILE, PEER_HEADS, width), lambda t, p: (p, 0, tile_of(t))),
            pl.BlockSpec((PEER_HEADS, PEER_KEYS, width), lambda t, p: (0, 0, tile_of(t)),
                         pipeline_mode=pl.Buffered(1)),
            pl.BlockSpec((PEER_HEADS, width), lambda t, p: (0, tile_of(t))),
        ]

    def buffers(width):
        return [pltpu.VMEM((PEER_TE, width), F32), pltpu.VMEM((PEER_TE, width), F32),
                pltpu.VMEM((PEER_TE, width), BF16), pltpu.VMEM((PEER_TE, width), BF16)]

    return pl.pallas_call(
        _peer_dense_kernel,
        grid=(n // tm, N_PAIRS + 1),
        in_specs=[u_spec, v_spec, vec, vec] + token_specs(tm, lambda t: t) + token_specs(n_s, lambda t: 0),
        out_specs=[pl.BlockSpec((tm, D_MODEL), lambda t, p: (t, 0), pipeline_mode=pl.Buffered(1)),
                   pl.BlockSpec((n_s, D_MODEL), lambda t, p: (0, 0))],
        out_shape=[jax.ShapeDtypeStruct((n, D_MODEL), F32), jax.ShapeDtypeStruct((n_s, D_MODEL), F32)],
        scratch_shapes=buffers(tm) + buffers(n_s),
        compiler_params=_cparams("arbitrary", "arbitrary"),
        name="peer_dense",
    )(u_all, v_all, g.reshape(1, D_MODEL), beta.reshape(1, D_MODEL), *prompt, *sample)


def _peer_layer(xp, xs, wq_bf, keys_bf, u_all, v_all, layer, g, beta):
    prompt = (xp,) + tuple(_peer_scores(xp, wq_bf, keys_bf, 256))
    sample = (xs,) + tuple(_peer_scores(xs, wq_bf, keys_bf, xs.shape[0]))
    return _peer_dense(u_all, v_all, layer, g, beta, prompt, sample, 1024)


def _rope_tables(pos):
    half = HEAD_DIM // 2
    inv = jnp.power(ROPE_THETA, -jnp.arange(half, dtype=F32) * 2.0 / HEAD_DIM)
    ang = pos.astype(F32)[:, None] * inv[None, :]
    cos = jnp.cos(ang)
    sin = jnp.sin(ang)
    reps = LANES // HEAD_DIM
    return (jnp.tile(jnp.concatenate([cos, cos], axis=1), (1, reps)),
            jnp.tile(jnp.concatenate([-sin, sin], axis=1), (1, reps)))


def _shifted(state, u3):
    t = u3.shape[1]
    up = jnp.concatenate([state, u3], axis=1)
    return up[:, 1:t + 1], up[:, 0:t], up[:, -(CONV_WIDTH - 1):]


def kernel(x_prompt, x_sample, cache_k, cache_v, state_conv, w_qkv, b_qkv, w_o, b_o, attn_sinks,
           w_conv_in, conv_w, w_conv_out, w_peer_q, peer_sub_keys, peer_u, peer_v, ln_g, ln_b):
    batch, seq, _ = x_prompt.shape
    dec_batch, dec_seq, _ = x_sample.shape
    n_p = batch * seq
    n_s = dec_batch * dec_seq
    xp = x_prompt.reshape(n_p, D_MODEL)
    xs = x_sample.reshape(n_s, D_MODEL)
    cos_p, sin_p = _rope_tables(jnp.tile(jnp.arange(seq), batch))
    cos_s, sin_s = _rope_tables(jnp.tile(PAST_LEN + jnp.arange(dec_seq), dec_batch))

    new_kp, new_vp, new_cp, new_ks, new_vs, new_cs = [], [], [], [], [], []
    for layer in range(DEPTH):
        j = layer // 2
        g0, b0 = ln_g[layer, 0], ln_b[layer, 0]
        if layer % 2 == 0:
            wqkv = w_qkv[j].astype(BF16)
            wo = w_o[j].astype(BF16)
            sinks = attn_sinks[j].reshape(N_Q_HEADS)
            qkv_p = _qkv_rope(xp, wqkv, b_qkv[j], cos_p, sin_p, 1024)
            qkv_s = _qkv_rope(xs, wqkv, b_qkv[j], cos_s, sin_s, n_s)
            o_p = _attn_prompt(qkv_p, sinks, batch, seq)
            qkv_s3 = qkv_s.reshape(dec_batch, dec_seq, QKV_COLS)
            qkv_s8 = jnp.pad(qkv_s3, ((0, 0), (0, SAMPLE_ROWS - dec_seq), (0, 0)))
            ck = cache_k[j].reshape(dec_batch, WINDOW, KV_COLS)
            cv = cache_v[j].reshape(dec_batch, WINDOW, KV_COLS)
            o_s = _attn_sample(qkv_s8, ck, cv, sinks, dec_seq)[:, :dec_seq].reshape(n_s, Q_COLS)
            kv_p = qkv_p.reshape(batch, seq, QKV_COLS)[:, -WINDOW:, Q_COLS:]
            new_kp.append(kv_p[..., :KV_COLS].reshape(batch, WINDOW, N_KV_HEADS, HEAD_DIM))
            new_vp.append(kv_p[..., KV_COLS:].reshape(batch, WINDOW, N_KV_HEADS, HEAD_DIM))
            k_s = jnp.concatenate([ck, qkv_s3[..., Q_COLS:Q_COLS + KV_COLS]], axis=1)[:, -WINDOW:]
            v_s = jnp.concatenate([cv, qkv_s3[..., Q_COLS + KV_COLS:]], axis=1)[:, -WINDOW:]
            new_ks.append(k_s.reshape(dec_batch, WINDOW, N_KV_HEADS, HEAD_DIM))
            new_vs.append(v_s.reshape(dec_batch, WINDOW, N_KV_HEADS, HEAD_DIM))
            xp = _proj_ln(o_p, wo, b_o[j], xp, g0, b0, 512)
            xs = _proj_ln(o_s, wo, b_o[j], xs, g0, b0, n_s)
        else:
            w_in = w_conv_in[j].astype(BF16)
            w_out = w_conv_out[j].astype(BF16)
            gb_p, u_p = _conv_in(xp, w_in, 1024)
            gb_s, u_s = _conv_in(xs, w_in, n_s)
            um1_s, um2_s, cs = _shifted(state_conv[j], u_s.reshape(dec_batch, dec_seq, D_MODEL))
            new_cp.append(u_p.reshape(batch, seq, D_MODEL)[:, -(CONV_WIDTH - 1):])
            new_cs.append(cs)
            xp = _conv_out_ln(gb_p, u_p, None, conv_w[j], w_out, xp, g0, b0, 256, seq=seq)
            xs = _conv_out_ln(gb_s, u_s, (um1_s.reshape(n_s, D_MODEL), um2_s.reshape(n_s, D_MODEL)),
                              conv_w[j], w_out, xs, g0, b0, n_s)
        wq = w_peer_q[layer].astype(BF16)
        keys = peer_sub_keys[layer].astype(BF16)
        g1, b1 = ln_g[layer, 1], ln_b[layer, 1]
        xp, xs = _peer_layer(xp, xs, wq, keys, peer_u, peer_v, layer, g1, b1)

    return (xp.reshape(batch, seq, D_MODEL), xs.reshape(dec_batch, dec_seq, D_MODEL),
            jnp.stack(new_kp), jnp.stack(new_vp), jnp.stack(new_cp),
            jnp.stack(new_ks), jnp.stack(new_vs), jnp.stack(new_cs))
```

```python
import functools

import jax
import jax.numpy as jnp
from jax import lax
from jax.experimental import pallas as pl
from jax.experimental.pallas import tpu as pltpu

F32 = jnp.float32
BF16 = jnp.bfloat16

D_MODEL = 2048
DEPTH = 4
HEAD_DIM = 64
N_Q_HEADS = 32
N_KV_HEADS = 8
GQA_GROUP = 4
WINDOW = 128
ROPE_THETA = 10000.0
PAST_LEN = 16384
CONV_WIDTH = 3
PEER_HEADS = 8
PEER_KEYS = 128
PEER_EXPERTS = PEER_KEYS * PEER_KEYS
PEER_HALF = 128
PEER_TOPK = 16
LN_EPS = 1e-5
NEG_INF = -1e30
DEEPNORM_ALPHA = (2 * DEPTH) ** 0.25
LOG2E = 1.4426950408889634

Q_COLS = N_Q_HEADS * HEAD_DIM
KV_COLS = N_KV_HEADS * HEAD_DIM
QKV_COLS = Q_COLS + 2 * KV_COLS

LANES = 128
PEER_TE = 256
I1_PER_TILE = PEER_TE // PEER_KEYS
VMEM_LIMIT_BYTES = 56 * 1024 * 1024


def _cparams(*semantics):
    return pltpu.CompilerParams(dimension_semantics=semantics, vmem_limit_bytes=VMEM_LIMIT_BYTES)


def _layer_norm(z, g, b):
    mu = jnp.mean(z, axis=-1, keepdims=True)
    zc = z - mu
    var = jnp.mean(zc * zc, axis=-1, keepdims=True)
    return zc * lax.rsqrt(var + LN_EPS) * g + b


QKV_TN = 512
N_ROPE_TILES = (Q_COLS + KV_COLS) // QKV_TN


def _qkv_kernel(x_ref, w_ref, b_ref, cos_ref, sin_ref, o_ref):
    j = pl.program_id(1)
    acc = jnp.dot(x_ref[...].astype(BF16), w_ref[...], preferred_element_type=F32) + b_ref[...]

    @pl.when(j < N_ROPE_TILES)
    def _():
        cos = cos_ref[...]
        sin = sin_ref[...]
        lane = lax.broadcasted_iota(jnp.int32, cos.shape, 1)
        first_half = (lane % HEAD_DIM) < (HEAD_DIM // 2)
        for c in range(QKV_TN // LANES):
            seg = acc[:, c * LANES:(c + 1) * LANES]
            partner = jnp.where(first_half,
                                pltpu.roll(seg, LANES - HEAD_DIM // 2, 1),
                                pltpu.roll(seg, HEAD_DIM // 2, 1))
            o_ref[:, c * LANES:(c + 1) * LANES] = seg * cos + partner * sin

    @pl.when(j >= N_ROPE_TILES)
    def _():
        o_ref[...] = acc


def _qkv_rope(x, w_bf, b, cos, sin, tm):
    n = x.shape[0]
    return pl.pallas_call(
        _qkv_kernel,
        grid=(n // tm, QKV_COLS // QKV_TN),
        in_specs=[
            pl.BlockSpec((tm, D_MODEL), lambda i, j: (i, 0)),
            pl.BlockSpec((D_MODEL, QKV_TN), lambda i, j: (0, j)),
            pl.BlockSpec((1, QKV_TN), lambda i, j: (0, j)),
            pl.BlockSpec((tm, LANES), lambda i, j: (i, 0)),
            pl.BlockSpec((tm, LANES), lambda i, j: (i, 0)),
        ],
        out_specs=pl.BlockSpec((tm, QKV_TN), lambda i, j: (i, j)),
        out_shape=jax.ShapeDtypeStruct((n, QKV_COLS), F32),
        compiler_params=_cparams("parallel", "arbitrary"),
        name="qkv_rope",
    )(x, w_bf, b.reshape(1, QKV_COLS), cos, sin)


def _attn_prompt_kernel(sink_ref, q_ref, kp_ref, kc_ref, vp_ref, vc_ref, o_ref):
    i = pl.program_id(1)
    cols = GQA_GROUP * WINDOW
    kj = lax.broadcasted_iota(jnp.int32, (2 * WINDOW, cols), 0)
    qi = lax.broadcasted_iota(jnp.int32, (2 * WINDOW, cols), 1) % WINDOW
    dist = qi + WINDOW - kj
    prev_ok = jnp.where(i > 0, 0, WINDOW)
    mask = (dist >= 0) & (dist <= WINDOW) & (kj >= prev_ok)
    nt = (((1,), (1,)), ((), ()))
    tn = (((0,), (0,)), ((), ()))
    for g in range(N_KV_HEADS):
        ks = slice(g * HEAD_DIM, (g + 1) * HEAD_DIM)
        k = jnp.concatenate([kp_ref[:, ks], kc_ref[:, ks]], axis=0).astype(BF16)
        v = jnp.concatenate([vp_ref[:, ks], vc_ref[:, ks]], axis=0).astype(BF16)
        q = jnp.concatenate(
            [q_ref[:, (g * GQA_GROUP + j) * HEAD_DIM:(g * GQA_GROUP + j + 1) * HEAD_DIM]
             for j in range(GQA_GROUP)], axis=0).astype(BF16)
        t = lax.dot_general(k, q, nt, preferred_element_type=F32)
        t = jnp.where(mask, t * (HEAD_DIM ** -0.5 * LOG2E), NEG_INF)
        sink = jnp.concatenate(
            [jnp.full((1, WINDOW), sink_ref[g * GQA_GROUP + j] * LOG2E, F32) for j in range(GQA_GROUP)], axis=1)
        m = jnp.maximum(jnp.max(t, axis=0, keepdims=True), sink)
        p = jnp.exp2(t - m)
        denom = jnp.sum(p, axis=0, keepdims=True) + jnp.exp2(sink - m)
        o_t = lax.dot_general(v, p.astype(BF16), tn, preferred_element_type=F32) / denom
        o = o_t.T
        o_ref[:, g * GQA_GROUP * HEAD_DIM:(g + 1) * GQA_GROUP * HEAD_DIM] = jnp.concatenate(
            [o[j * WINDOW:(j + 1) * WINDOW] for j in range(GQA_GROUP)], axis=1).astype(o_ref.dtype)


def _attn_prompt(qkv, sinks, batch, seq):
    nb = seq // WINDOW
    kcol = Q_COLS // KV_COLS
    vcol = kcol + 1

    def cur(b, i):
        return b * nb + i

    def prev(b, i):
        return b * nb + jnp.maximum(i - 1, 0)

    return pl.pallas_call(
        _attn_prompt_kernel,
        grid=(batch, nb),
        in_specs=[
            pl.BlockSpec(memory_space=pltpu.SMEM),
            pl.BlockSpec((WINDOW, Q_COLS), lambda b, i: (cur(b, i), 0)),
            pl.BlockSpec((WINDOW, KV_COLS), lambda b, i: (prev(b, i), kcol)),
            pl.BlockSpec((WINDOW, KV_COLS), lambda b, i: (cur(b, i), kcol)),
            pl.BlockSpec((WINDOW, KV_COLS), lambda b, i: (prev(b, i), vcol)),
            pl.BlockSpec((WINDOW, KV_COLS), lambda b, i: (cur(b, i), vcol)),
        ],
        out_specs=pl.BlockSpec((WINDOW, Q_COLS), lambda b, i: (cur(b, i), 0)),
        out_shape=jax.ShapeDtypeStruct((batch * seq, Q_COLS), BF16),
        compiler_params=_cparams("parallel", "arbitrary"),
        name="attn_prompt",
    )(sinks, qkv, qkv, qkv, qkv, qkv)


SAMPLE_ROWS = 8


SAMPLE_BATCH_BLOCK = 8


def _attn_sample_kernel(sink_ref, qkv_ref, ck_ref, cv_ref, o_ref, *, n_new):
    n_keys = WINDOW + SAMPLE_ROWS
    kk = lax.broadcasted_iota(jnp.int32, (n_keys, LANES), 0)
    lane = lax.broadcasted_iota(jnp.int32, (n_keys, LANES), 1)
    qi = lane % SAMPLE_ROWS
    new = kk - WINDOW
    mask = ((kk < WINDOW) & (kk >= qi)) | ((new >= 0) & (new <= qi) & (new < n_new))
    head = lax.broadcasted_iota(jnp.int32, (1, LANES), 1) // SAMPLE_ROWS
    nt = (((1,), (1,)), ((), ()))
    tn = (((0,), (0,)), ((), ()))
    q_pad = jnp.zeros((LANES - GQA_GROUP * SAMPLE_ROWS, HEAD_DIM), BF16)
    for b in range(qkv_ref.shape[0]):
        qkv = qkv_ref[b]
        for g in range(N_KV_HEADS):
            ks = slice(g * HEAD_DIM, (g + 1) * HEAD_DIM)
            k = jnp.concatenate([ck_ref[b, :, ks], qkv[:, Q_COLS + g * HEAD_DIM:Q_COLS + (g + 1) * HEAD_DIM]],
                                axis=0).astype(BF16)
            v = jnp.concatenate([cv_ref[b, :, ks],
                                 qkv[:, Q_COLS + KV_COLS + g * HEAD_DIM:Q_COLS + KV_COLS + (g + 1) * HEAD_DIM]],
                                axis=0).astype(BF16)
            q = jnp.concatenate(
                [qkv[:, (g * GQA_GROUP + j) * HEAD_DIM:(g * GQA_GROUP + j + 1) * HEAD_DIM].astype(BF16)
                 for j in range(GQA_GROUP)] + [q_pad], axis=0)
            t = lax.dot_general(k, q, nt, preferred_element_type=F32)
            t = jnp.where(mask, t * (HEAD_DIM ** -0.5 * LOG2E), NEG_INF)
            sink = jnp.zeros((1, LANES), F32)
            for j in range(GQA_GROUP):
                sink = jnp.where(head == j, sink_ref[g * GQA_GROUP + j] * LOG2E, sink)
            m = jnp.maximum(jnp.max(t, axis=0, keepdims=True), sink)
            p = jnp.exp2(t - m)
            denom = jnp.sum(p, axis=0, keepdims=True) + jnp.exp2(sink - m)
            o_t = lax.dot_general(v, p.astype(BF16), tn, preferred_element_type=F32) / denom
            o = o_t.T
            o_ref[b, :, g * GQA_GROUP * HEAD_DIM:(g + 1) * GQA_GROUP * HEAD_DIM] = jnp.concatenate(
                [o[j * SAMPLE_ROWS:(j + 1) * SAMPLE_ROWS] for j in range(GQA_GROUP)], axis=1)


def _attn_sample(qkv8, ck, cv, sinks, n_new):
    nb = qkv8.shape[0]
    bb = SAMPLE_BATCH_BLOCK
    return pl.pallas_call(
        functools.partial(_attn_sample_kernel, n_new=n_new),
        grid=(nb // bb,),
        in_specs=[
            pl.BlockSpec(memory_space=pltpu.SMEM),
            pl.BlockSpec((bb, SAMPLE_ROWS, QKV_COLS), lambda b: (b, 0, 0)),
            pl.BlockSpec((bb, WINDOW, KV_COLS), lambda b: (b, 0, 0)),
            pl.BlockSpec((bb, WINDOW, KV_COLS), lambda b: (b, 0, 0)),
        ],
        out_specs=pl.BlockSpec((bb, SAMPLE_ROWS, Q_COLS), lambda b: (b, 0, 0)),
        out_shape=jax.ShapeDtypeStruct((nb, SAMPLE_ROWS, Q_COLS), F32),
        compiler_params=_cparams("parallel"),
        name="attn_sample",
    )(sinks, qkv8, ck, cv)


def _proj_ln_kernel(a_ref, w_ref, b_ref, x_ref, g_ref, beta_ref, o_ref):
    y = jnp.dot(a_ref[...].astype(BF16), w_ref[...], preferred_element_type=F32) + b_ref[...]
    o_ref[...] = _layer_norm(DEEPNORM_ALPHA * x_ref[...] + y, g_ref[...], beta_ref[...])


def _proj_ln(a, w_bf, bias, x, g, beta, tm):
    n, k = a.shape
    row = lambda i: (i, 0)
    fixed = lambda i: (0, 0)
    return pl.pallas_call(
        _proj_ln_kernel,
        grid=(n // tm,),
        in_specs=[
            pl.BlockSpec((tm, k), row),
            pl.BlockSpec((k, D_MODEL), fixed),
            pl.BlockSpec((1, D_MODEL), fixed),
            pl.BlockSpec((tm, D_MODEL), row),
            pl.BlockSpec((1, D_MODEL), fixed),
            pl.BlockSpec((1, D_MODEL), fixed),
        ],
        out_specs=pl.BlockSpec((tm, D_MODEL), row),
        out_shape=jax.ShapeDtypeStruct((n, D_MODEL), F32),
        compiler_params=_cparams("parallel"),
        name="proj_ln",
    )(a, w_bf, bias.reshape(1, D_MODEL), x, g.reshape(1, D_MODEL), beta.reshape(1, D_MODEL))


CONV_TN = 512


def _conv_in_kernel(x_ref, wb_ref, wc_ref, wh_ref, gb_ref, u_ref):
    xb = x_ref[...].astype(BF16)
    gb_ref[...] = jnp.dot(xb, wb_ref[...], preferred_element_type=F32)
    gate_c = jnp.dot(xb, wc_ref[...], preferred_element_type=F32)
    h = jnp.dot(xb, wh_ref[...], preferred_element_type=F32)
    u_ref[...] = gate_c * h


def _conv_in(x, w_in_bf, tm):
    n = x.shape[0]
    nt = D_MODEL // CONV_TN
    return pl.pallas_call(
        _conv_in_kernel,
        grid=(n // tm, nt),
        in_specs=[
            pl.BlockSpec((tm, D_MODEL), lambda i, j: (i, 0)),
            pl.BlockSpec((D_MODEL, CONV_TN), lambda i, j: (0, j)),
            pl.BlockSpec((D_MODEL, CONV_TN), lambda i, j: (0, nt + j)),
            pl.BlockSpec((D_MODEL, CONV_TN), lambda i, j: (0, 2 * nt + j)),
        ],
        out_specs=[
            pl.BlockSpec((tm, CONV_TN), lambda i, j: (i, j)),
            pl.BlockSpec((tm, CONV_TN), lambda i, j: (i, j)),
        ],
        out_shape=[jax.ShapeDtypeStruct((n, D_MODEL), F32)] * 2,
        compiler_params=_cparams("parallel", "arbitrary"),
        name="conv_in",
    )(x, w_in_bf, w_in_bf, w_in_bf)


def _gated_out_ln(gate_b, conv, w_ref, x_ref, g_ref, beta_ref, o_ref):
    y = jnp.dot((gate_b * conv).astype(BF16), w_ref[...], preferred_element_type=F32)
    o_ref[...] = _layer_norm(DEEPNORM_ALPHA * x_ref[...] + y, g_ref[...], beta_ref[...])


def _conv_out_ln_kernel(gb_ref, u_ref, um1_ref, um2_ref, cw_ref, w_ref, x_ref, g_ref, beta_ref, o_ref):
    conv = um2_ref[...] * cw_ref[0:1, :]
    conv = conv + um1_ref[...] * cw_ref[1:2, :]
    conv = conv + u_ref[...] * cw_ref[2:3, :]
    _gated_out_ln(gb_ref[...], conv, w_ref, x_ref, g_ref, beta_ref, o_ref)


def _conv_out_ln_halo_kernel(gb_ref, u_ref, halo_ref, cw_ref, w_ref, x_ref, g_ref, beta_ref, o_ref,
                             *, tiles_per_seq):
    i = pl.program_id(0)
    u = u_ref[...]
    halo = jnp.where(i % tiles_per_seq == 0, 0.0, halo_ref[...])
    row = lax.broadcasted_iota(jnp.int32, u.shape, 0)
    um1 = jnp.where(row == 0, halo[7:8, :], pltpu.roll(u, 1, 0))
    um2 = jnp.where(row == 0, halo[6:7, :], jnp.where(row == 1, halo[7:8, :], pltpu.roll(u, 2, 0)))
    conv = um2 * cw_ref[0:1, :] + um1 * cw_ref[1:2, :] + u * cw_ref[2:3, :]
    _gated_out_ln(gb_ref[...], conv, w_ref, x_ref, g_ref, beta_ref, o_ref)


def _conv_out_ln(gate_b, u, shifted, conv_w, w_out_bf, x, g, beta, tm, seq=None):
    n = x.shape[0]
    row = lambda i: (i, 0)
    fixed = lambda i: (0, 0)
    tile = pl.BlockSpec((tm, D_MODEL), row)
    vec = pl.BlockSpec((1, D_MODEL), fixed)
    tail_specs = [pl.BlockSpec((CONV_WIDTH, D_MODEL), fixed), pl.BlockSpec((D_MODEL, D_MODEL), fixed),
                  tile, vec, vec]
    tail_args = (conv_w, w_out_bf, x, g.reshape(1, D_MODEL), beta.reshape(1, D_MODEL))
    if shifted is None:
        body = functools.partial(_conv_out_ln_halo_kernel, tiles_per_seq=seq // tm)
        halo = pl.BlockSpec((8, D_MODEL), lambda i: (jnp.maximum(i * (tm // 8) - 1, 0), 0))
        in_specs, args = [tile, tile, halo], (gate_b, u, u)
    else:
        body = _conv_out_ln_kernel
        in_specs, args = [tile, tile, tile, tile], (gate_b, u) + tuple(shifted)
    return pl.pallas_call(
        body,
        grid=(n // tm,),
        in_specs=in_specs + tail_specs,
        out_specs=tile,
        out_shape=jax.ShapeDtypeStruct((n, D_MODEL), F32),
        compiler_params=_cparams("parallel"),
        name="conv_out_ln",
    )(*args, *tail_args)


def _oddeven_merge_sort_pairs(n):
    pairs = []
    p = 1
    while p < n:
        k = p
        while k >= 1:
            for j in range(k % p, n - k, 2 * k):
                for i in range(min(k, n - j - k)):
                    if (i + j) // (2 * p) == (i + j + k) // (2 * p):
                        pairs.append((i + j, i + j + k))
            k //= 2
        p *= 2
    return pairs


def _top_values(pieces, k):
    n = len(pieces)
    size = 1
    while size < n:
        size *= 2
    v = list(pieces) + [jnp.full_like(pieces[0], -jnp.inf)] * (size - n)
    for i, j in _oddeven_merge_sort_pairs(size):
        v[i], v[j] = jnp.maximum(v[i], v[j]), jnp.minimum(v[i], v[j])
    v = v[:n]
    vals = []
    for it in range(k):
        m = jnp.max(v[0], axis=0, keepdims=True)
        vals.append(m)
        need = k - it - 1
        if need:
            hit = v[0] == m
            nxt = v[1:] + [jnp.full_like(v[0], -jnp.inf)]
            v = [jnp.where(hit, nxt[j], v[j]) for j in range(min(need, len(v)))]
    return vals


def _peer_score_kernel(x_ref, wq_ref, keys_ref, xt_ref, s1l_ref, s2l_ref, tau_ref):
    pad = jnp.zeros((I1_PER_TILE,) + s1l_ref.shape[1:], F32)
    s1l_ref[:I1_PER_TILE] = pad
    s1l_ref[I1_PER_TILE + PEER_KEYS:] = pad
    x = x_ref[...]
    xt_ref[...] = x.T.astype(BF16)
    q = jnp.dot(x.astype(BF16), wq_ref[...], preferred_element_type=F32).astype(BF16)
    nt = (((1,), (1,)), ((), ()))
    k = PEER_TOPK
    for h in range(PEER_HEADS):
        c0 = h * 2 * PEER_HALF
        s1 = lax.dot_general(keys_ref[h, 0], q[:, c0:c0 + PEER_HALF], nt, preferred_element_type=F32)
        s2 = lax.dot_general(keys_ref[h, 1], q[:, c0 + PEER_HALF:c0 + 2 * PEER_HALF], nt,
                             preferred_element_type=F32)
        a = _top_values([s1[r:r + 8] for r in range(0, PEER_KEYS, 8)], k + 1)
        b = _top_values([s2[r:r + 8] for r in range(0, PEER_KEYS, 8)], k + 1)
        neg = jnp.full_like(a[0], -jnp.inf)
        b24 = jnp.concatenate(b + [neg] * 7, axis=0)
        b8 = jnp.concatenate(b[:8], axis=0)
        a16 = jnp.concatenate(a[8:] + [neg] * 7, axis=0)
        row = lax.broadcasted_iota(jnp.int32, b8.shape, 0)
        eb24 = jnp.exp(b24 - b[0])
        eb8 = jnp.exp(b8 - b[0])
        cand = [a[0] + b24[r:r + 8] for r in (0, 8, 16)]
        wgt = [eb24[r:r + 8] for r in (0, 8, 16)]
        for i in range(1, 8):
            cand.append(jnp.where(row < (k + 1) // (i + 1), a[i] + b8, -jnp.inf))
            wgt.append(jnp.exp(a[i] - a[0]) * eb8)
        cand += [a16[r:r + 8] + b[0] for r in (0, 8)]
        wgt += [jnp.exp(a16[r:r + 8] - a[0]) for r in (0, 8)]
        tops = _top_values(cand, k + 1)
        thr = 0.5 * (tops[k - 1] + tops[k])
        z8 = functools.reduce(jnp.add, [jnp.where(c >= thr, w, 0.0) for c, w in zip(cand, wgt)])
        z = jnp.sum(z8, axis=0, keepdims=True)
        shift = a[0] + b[0] + jnp.log(z)
        s1l_ref[I1_PER_TILE:I1_PER_TILE + PEER_KEYS, h, :] = (s1 - shift) * LOG2E - 1.0
        s2l_ref[h] = s2 * LOG2E
        tau_ref[h:h + 1, :] = (thr - shift) * LOG2E - 1.0


def _peer_scores(x, wq_bf, keys_bf, tm):
    n = x.shape[0]
    hk = (PEER_HEADS, PEER_KEYS, tm)
    hk_map = lambda i: (0, 0, i)
    hk_shape = jax.ShapeDtypeStruct((PEER_HEADS, PEER_KEYS, n), F32)
    return pl.pallas_call(
        _peer_score_kernel,
        grid=(n // tm,),
        in_specs=[
            pl.BlockSpec((tm, D_MODEL), lambda i: (i, 0)),
            pl.BlockSpec((D_MODEL, PEER_HEADS * 2 * PEER_HALF), lambda i: (0, 0)),
            pl.BlockSpec((PEER_HEADS, 2, PEER_KEYS, PEER_HALF), lambda i: (0, 0, 0, 0)),
        ],
        out_specs=[
            pl.BlockSpec((D_MODEL, tm), lambda i: (0, i)),
            pl.BlockSpec((PEER_KEYS + 2 * I1_PER_TILE, PEER_HEADS, tm), hk_map), pl.BlockSpec(hk, hk_map),
            pl.BlockSpec((PEER_HEADS, tm), lambda i: (0, i)),
        ],
        out_shape=[
            jax.ShapeDtypeStruct((D_MODEL, n), BF16),
            jax.ShapeDtypeStruct((PEER_KEYS + 2 * I1_PER_TILE, PEER_HEADS, n), F32), hk_shape,
            jax.ShapeDtypeStruct((PEER_HEADS, n), F32),
        ],
        compiler_params=_cparams("parallel"),
        name="peer_scores",
    )(x, wq_bf, keys_bf)


def _gelu_times_two(x):
    return x * (1.0 + lax.erf(x * (2.0 ** -0.5)))


N_PAIRS = PEER_EXPERTS // (2 * PEER_TE)


def _dense_pipeline_step(p, u_ref, v_ref, xt_ref, s1l_ref, s2l_ref, tau_ref, o_ref, hid, at):
    tm = xt_ref.shape[1]

    @pl.when(p == 0)
    def _():
        o_ref[...] = jnp.zeros_like(o_ref)
        for buf in hid + at:
            buf[...] = jnp.zeros_like(buf)

    for half in range(2):
        rows = slice(half * PEER_TE, (half + 1) * PEER_TE)
        hid[half][...] = jnp.dot(u_ref[rows, :].astype(BF16), xt_ref[...], preferred_element_type=F32)

        src, dst = hid[1 - half], at[1 - half]
        for il in range(I1_PER_TILE):
            r = half * I1_PER_TILE + il
            rs = slice(il * PEER_KEYS, (il + 1) * PEER_KEYS)
            for c in range(tm // LANES):
                cs = slice(c * LANES, (c + 1) * LANES)
                gate = jnp.zeros((PEER_KEYS, LANES), F32)
                for h in range(PEER_HEADS):
                    pair = s2l_ref[h, :, cs] + s1l_ref[r, h:h + 1, cs]
                    gate = gate + jnp.where(pair >= tau_ref[h:h + 1, cs], jnp.exp2(pair), 0.0)
                dst[rs, cs] = (_gelu_times_two(src[rs, cs]) * gate).astype(BF16)

        o_ref[...] += lax.dot_general(at[half][...], v_ref[rows, :].astype(BF16), (((0,), (0,)), ((), ())),
                                      preferred_element_type=F32)


def _peer_dense_kernel(u_ref, v_ref, xt_ref, s1l_ref, s2l_ref, tau_ref, xts_ref, s1ls_ref, s2ls_ref, taus_ref,
                       o_ref, os_ref, hid0, hid1, at0, at1, hids0, hids1, ats0, ats1):
    t = pl.program_id(0)
    p = pl.program_id(1)
    _dense_pipeline_step(p, u_ref, v_ref, xt_ref, s1l_ref, s2l_ref, tau_ref, o_ref, (hid0, hid1), (at0, at1))

    @pl.when(t == 0)
    def _():
        _dense_pipeline_step(p, u_ref, v_ref, xts_ref, s1ls_ref, s2ls_ref, taus_ref, os_ref,
                             (hids0, hids1), (ats0, ats1))


def _peer_dense(u_all, v_all, layer, prompt, sample, tm):
    n = prompt[0].shape[1]
    n_s = sample[0].shape[1]
    u_spec = pl.BlockSpec((None, 2 * PEER_TE, D_MODEL), lambda t, p: (layer, jnp.minimum(p, N_PAIRS - 1), 0))
    v_spec = pl.BlockSpec((None, 2 * PEER_TE, D_MODEL), lambda t, p: (layer, jnp.maximum(p - 1, 0), 0))

    def token_specs(width, tile_of):
        return [
            pl.BlockSpec((D_MODEL, width), lambda t, p: (0, tile_of(t))),
            pl.BlockSpec((2 * I1_PER_TILE, PEER_HEADS, width), lambda t, p: (p, 0, tile_of(t))),
            pl.BlockSpec((PEER_HEADS, PEER_KEYS, width), lambda t, p: (0, 0, tile_of(t)),
                         pipeline_mode=pl.Buffered(1)),
            pl.BlockSpec((PEER_HEADS, width), lambda t, p: (0, tile_of(t))),
        ]

    def buffers(width):
        return [pltpu.VMEM((PEER_TE, width), F32), pltpu.VMEM((PEER_TE, width), F32),
                pltpu.VMEM((PEER_TE, width), BF16), pltpu.VMEM((PEER_TE, width), BF16)]

    return pl.pallas_call(
        _peer_dense_kernel,
        grid=(n // tm, N_PAIRS + 1),
        in_specs=[u_spec, v_spec] + token_specs(tm, lambda t: t) + token_specs(n_s, lambda t: 0),
        out_specs=[pl.BlockSpec((tm, D_MODEL), lambda t, p: (t, 0)),
                   pl.BlockSpec((n_s, D_MODEL), lambda t, p: (0, 0))],
        out_shape=[jax.ShapeDtypeStruct((n, D_MODEL), F32), jax.ShapeDtypeStruct((n_s, D_MODEL), F32)],
        scratch_shapes=buffers(tm) + buffers(n_s),
        compiler_params=_cparams("arbitrary", "arbitrary"),
        name="peer_dense",
    )(u_all, v_all, *prompt, *sample)


def _res_ln_kernel(x_ref, f_ref, g_ref, beta_ref, o_ref):
    o_ref[...] = _layer_norm(DEEPNORM_ALPHA * x_ref[...] + f_ref[...], g_ref[...], beta_ref[...])


def _res_ln(x, f, g, beta, tm):
    n = x.shape[0]
    tile = pl.BlockSpec((tm, D_MODEL), lambda i: (i, 0))
    vec = pl.BlockSpec((1, D_MODEL), lambda i: (0, 0))
    return pl.pallas_call(
        _res_ln_kernel,
        grid=(n // tm,),
        in_specs=[tile, tile, vec, vec],
        out_specs=tile,
        out_shape=jax.ShapeDtypeStruct((n, D_MODEL), F32),
        compiler_params=_cparams("parallel"),
        name="res_ln",
    )(x, f, g.reshape(1, D_MODEL), beta.reshape(1, D_MODEL))


def _peer_layer(xp, xs, wq_bf, keys_bf, u_all, v_all, layer, g, beta):
    prompt = _peer_scores(xp, wq_bf, keys_bf, 256)
    sample = _peer_scores(xs, wq_bf, keys_bf, xs.shape[0])
    fp, fs = _peer_dense(u_all, v_all, layer, prompt, sample, 1024)
    return _res_ln(xp, fp, g, beta, 512), _res_ln(xs, fs, g, beta, xs.shape[0])


def _rope_tables(pos):
    half = HEAD_DIM // 2
    inv = jnp.power(ROPE_THETA, -jnp.arange(half, dtype=F32) * 2.0 / HEAD_DIM)
    ang = pos.astype(F32)[:, None] * inv[None, :]
    cos = jnp.cos(ang)
    sin = jnp.sin(ang)
    reps = LANES // HEAD_DIM
    return (jnp.tile(jnp.concatenate([cos, cos], axis=1), (1, reps)),
            jnp.tile(jnp.concatenate([-sin, sin], axis=1), (1, reps)))


def _shifted(state, u3):
    t = u3.shape[1]
    up = jnp.concatenate([state, u3], axis=1)
    return up[:, 1:t + 1], up[:, 0:t], up[:, -(CONV_WIDTH - 1):]


def kernel(x_prompt, x_sample, cache_k, cache_v, state_conv, w_qkv, b_qkv, w_o, b_o, attn_sinks,
           w_conv_in, conv_w, w_conv_out, w_peer_q, peer_sub_keys, peer_u, peer_v, ln_g, ln_b):
    batch, seq, _ = x_prompt.shape
    dec_batch, dec_seq, _ = x_sample.shape
    n_p = batch * seq
    n_s = dec_batch * dec_seq
    xp = x_prompt.reshape(n_p, D_MODEL)
    xs = x_sample.reshape(n_s, D_MODEL)
    cos_p, sin_p = _rope_tables(jnp.tile(jnp.arange(seq), batch))
    cos_s, sin_s = _rope_tables(jnp.tile(PAST_LEN + jnp.arange(dec_seq), dec_batch))

    new_kp, new_vp, new_cp, new_ks, new_vs, new_cs = [], [], [], [], [], []
    for layer in range(DEPTH):
        j = layer // 2
        g0, b0 = ln_g[layer, 0], ln_b[layer, 0]
        if layer % 2 == 0:
            wqkv = w_qkv[j].astype(BF16)
            wo = w_o[j].astype(BF16)
            sinks = attn_sinks[j].reshape(N_Q_HEADS)
            qkv_p = _qkv_rope(xp, wqkv, b_qkv[j], cos_p, sin_p, 1024)
            qkv_s = _qkv_rope(xs, wqkv, b_qkv[j], cos_s, sin_s, n_s)
            o_p = _attn_prompt(qkv_p, sinks, batch, seq)
            qkv_s3 = qkv_s.reshape(dec_batch, dec_seq, QKV_COLS)
            qkv_s8 = jnp.pad(qkv_s3, ((0, 0), (0, SAMPLE_ROWS - dec_seq), (0, 0)))
            ck = cache_k[j].reshape(dec_batch, WINDOW, KV_COLS)
            cv = cache_v[j].reshape(dec_batch, WINDOW, KV_COLS)
            o_s = _attn_sample(qkv_s8, ck, cv, sinks, dec_seq)[:, :dec_seq].reshape(n_s, Q_COLS)
            kv_p = qkv_p.reshape(batch, seq, QKV_COLS)[:, -WINDOW:, Q_COLS:]
            new_kp.append(kv_p[..., :KV_COLS].reshape(batch, WINDOW, N_KV_HEADS, HEAD_DIM))
            new_vp.append(kv_p[..., KV_COLS:].reshape(batch, WINDOW, N_KV_HEADS, HEAD_DIM))
            k_s = jnp.concatenate([ck, qkv_s3[..., Q_COLS:Q_COLS + KV_COLS]], axis=1)[:, -WINDOW:]
            v_s = jnp.concatenate([cv, qkv_s3[..., Q_COLS + KV_COLS:]], axis=1)[:, -WINDOW:]
            new_ks.append(k_s.reshape(dec_batch, WINDOW, N_KV_HEADS, HEAD_DIM))
            new_vs.append(v_s.reshape(dec_batch, WINDOW, N_KV_HEADS, HEAD_DIM))
            xp = _proj_ln(o_p, wo, b_o[j], xp, g0, b0, 512)
            xs = _proj_ln(o_s, wo, b_o[j], xs, g0, b0, n_s)
        else:
            w_in = w_conv_in[j].astype(BF16)
            w_out = w_conv_out[j].astype(BF16)
            gb_p, u_p = _conv_in(xp, w_in, 1024)
            gb_s, u_s = _conv_in(xs, w_in, n_s)
            um1_s, um2_s, cs = _shifted(state_conv[j], u_s.reshape(dec_batch, dec_seq, D_MODEL))
            new_cp.append(u_p.reshape(batch, seq, D_MODEL)[:, -(CONV_WIDTH - 1):])
            new_cs.append(cs)
            xp = _conv_out_ln(gb_p, u_p, None, conv_w[j], w_out, xp, g0, b0, 256, seq=seq)
            xs = _conv_out_ln(gb_s, u_s, (um1_s.reshape(n_s, D_MODEL), um2_s.reshape(n_s, D_MODEL)),
                              conv_w[j], w_out, xs, g0, b0, n_s)
        wq = w_peer_q[layer].astype(BF16)
        keys = peer_sub_keys[layer].astype(BF16)
        g1, b1 = ln_g[layer, 1], ln_b[layer, 1]
        xp, xs = _peer_layer(xp, xs, wq, keys, peer_u, peer_v, layer, g1, b1)

    return (xp.reshape(batch, seq, D_MODEL), xs.reshape(dec_batch, dec_seq, D_MODEL),
            jnp.stack(new_kp), jnp.stack(new_vp), jnp.stack(new_cp),
            jnp.stack(new_ks), jnp.stack(new_vs), jnp.stack(new_cs))
```

```python
import functools

import jax
import jax.numpy as jnp
from jax import lax
from jax.experimental import pallas as pl
from jax.experimental.pallas import tpu as pltpu

F32 = jnp.float32
BF16 = jnp.bfloat16

D_MODEL = 2048
DEPTH = 4
HEAD_DIM = 64
N_Q_HEADS = 32
N_KV_HEADS = 8
GQA_GROUP = 4
WINDOW = 128
ROPE_THETA = 10000.0
PAST_LEN = 16384
CONV_WIDTH = 3
PEER_HEADS = 8
PEER_KEYS = 128
PEER_EXPERTS = PEER_KEYS * PEER_KEYS
PEER_HALF = 128
PEER_TOPK = 16
LN_EPS = 1e-5
NEG_INF = -1e30
DEEPNORM_ALPHA = (2 * DEPTH) ** 0.25
LOG2E = 1.4426950408889634

Q_COLS = N_Q_HEADS * HEAD_DIM
KV_COLS = N_KV_HEADS * HEAD_DIM
QKV_COLS = Q_COLS + 2 * KV_COLS

LANES = 128
SUBLANES = 8
PEER_TE = 256
I1_PER_TILE = PEER_TE // PEER_KEYS
VMEM_LIMIT_BYTES = 56 * 1024 * 1024

QKV_ROWS = 1024
PROJ_ROWS = 512
CONV_IN_ROWS = 1024
CONV_OUT_ROWS = 256
SCORE_ROWS = 256
DENSE_ROWS = 1024
RES_LN_ROWS = 512


def _rows(n, preferred):
    return min(n, preferred)


def _cparams(*semantics):
    return pltpu.CompilerParams(dimension_semantics=semantics, vmem_limit_bytes=VMEM_LIMIT_BYTES)


def _layer_norm(z, g, b):
    mu = jnp.mean(z, axis=-1, keepdims=True)
    zc = z - mu
    var = jnp.mean(zc * zc, axis=-1, keepdims=True)
    return zc * lax.rsqrt(var + LN_EPS) * g + b


QKV_TN = 512
N_ROPE_TILES = (Q_COLS + KV_COLS) // QKV_TN


def _qkv_kernel(x_ref, w_ref, b_ref, cos_ref, sin_ref, o_ref):
    j = pl.program_id(1)
    acc = jnp.dot(x_ref[...].astype(BF16), w_ref[...], preferred_element_type=F32) + b_ref[...]

    @pl.when(j < N_ROPE_TILES)
    def _():
        cos = cos_ref[...]
        sin = sin_ref[...]
        lane = lax.broadcasted_iota(jnp.int32, cos.shape, 1)
        first_half = (lane % HEAD_DIM) < (HEAD_DIM // 2)
        for c in range(QKV_TN // LANES):
            seg = acc[:, c * LANES:(c + 1) * LANES]
            partner = jnp.where(first_half,
                                pltpu.roll(seg, LANES - HEAD_DIM // 2, 1),
                                pltpu.roll(seg, HEAD_DIM // 2, 1))
            o_ref[:, c * LANES:(c + 1) * LANES] = seg * cos + partner * sin

    @pl.when(j >= N_ROPE_TILES)
    def _():
        o_ref[...] = acc


def _qkv_rope(x, w_bf, b, cos, sin, tm):
    n = x.shape[0]
    return pl.pallas_call(
        _qkv_kernel,
        grid=(n // tm, QKV_COLS // QKV_TN),
        in_specs=[
            pl.BlockSpec((tm, D_MODEL), lambda i, j: (i, 0)),
            pl.BlockSpec((D_MODEL, QKV_TN), lambda i, j: (0, j)),
            pl.BlockSpec((1, QKV_TN), lambda i, j: (0, j)),
            pl.BlockSpec((tm, LANES), lambda i, j: (i, 0)),
            pl.BlockSpec((tm, LANES), lambda i, j: (i, 0)),
        ],
        out_specs=pl.BlockSpec((tm, QKV_TN), lambda i, j: (i, j)),
        out_shape=jax.ShapeDtypeStruct((n, QKV_COLS), F32),
        compiler_params=_cparams("parallel", "arbitrary"),
        name="qkv_rope",
    )(x, w_bf, b.reshape(1, QKV_COLS), cos, sin)


def _attn_prompt_kernel(sink_ref, q_ref, kp_ref, kc_ref, vp_ref, vc_ref, o_ref):
    i = pl.program_id(1)
    cols = GQA_GROUP * WINDOW
    kj = lax.broadcasted_iota(jnp.int32, (2 * WINDOW, cols), 0)
    qi = lax.broadcasted_iota(jnp.int32, (2 * WINDOW, cols), 1) % WINDOW
    dist = qi + WINDOW - kj
    prev_ok = jnp.where(i > 0, 0, WINDOW)
    mask = (dist >= 0) & (dist <= WINDOW) & (kj >= prev_ok)
    nt = (((1,), (1,)), ((), ()))
    tn = (((0,), (0,)), ((), ()))
    for g in range(N_KV_HEADS):
        ks = slice(g * HEAD_DIM, (g + 1) * HEAD_DIM)
        k = jnp.concatenate([kp_ref[:, ks], kc_ref[:, ks]], axis=0).astype(BF16)
        v = jnp.concatenate([vp_ref[:, ks], vc_ref[:, ks]], axis=0).astype(BF16)
        q = jnp.concatenate(
            [q_ref[:, (g * GQA_GROUP + j) * HEAD_DIM:(g * GQA_GROUP + j + 1) * HEAD_DIM]
             for j in range(GQA_GROUP)], axis=0).astype(BF16)
        t = lax.dot_general(k, q, nt, preferred_element_type=F32)
        t = jnp.where(mask, t * (HEAD_DIM ** -0.5 * LOG2E), NEG_INF)
        sink = jnp.concatenate(
            [jnp.full((1, WINDOW), sink_ref[g * GQA_GROUP + j] * LOG2E, F32) for j in range(GQA_GROUP)], axis=1)
        m = jnp.maximum(jnp.max(t, axis=0, keepdims=True), sink)
        p = jnp.exp2(t - m)
        denom = jnp.sum(p, axis=0, keepdims=True) + jnp.exp2(sink - m)
        o_t = lax.dot_general(v, p.astype(BF16), tn, preferred_element_type=F32) / denom
        o = o_t.T
        o_ref[:, g * GQA_GROUP * HEAD_DIM:(g + 1) * GQA_GROUP * HEAD_DIM] = jnp.concatenate(
            [o[j * WINDOW:(j + 1) * WINDOW] for j in range(GQA_GROUP)], axis=1).astype(o_ref.dtype)


def _attn_prompt(qkv, sinks, batch, seq):
    nb = seq // WINDOW
    kcol = Q_COLS // KV_COLS
    vcol = kcol + 1

    def cur(b, i):
        return b * nb + i

    def prev(b, i):
        return b * nb + jnp.maximum(i - 1, 0)

    return pl.pallas_call(
        _attn_prompt_kernel,
        grid=(batch, nb),
        in_specs=[
            pl.BlockSpec(memory_space=pltpu.SMEM),
            pl.BlockSpec((WINDOW, Q_COLS), lambda b, i: (cur(b, i), 0)),
            pl.BlockSpec((WINDOW, KV_COLS), lambda b, i: (prev(b, i), kcol)),
            pl.BlockSpec((WINDOW, KV_COLS), lambda b, i: (cur(b, i), kcol)),
            pl.BlockSpec((WINDOW, KV_COLS), lambda b, i: (prev(b, i), vcol)),
            pl.BlockSpec((WINDOW, KV_COLS), lambda b, i: (cur(b, i), vcol)),
        ],
        out_specs=pl.BlockSpec((WINDOW, Q_COLS), lambda b, i: (cur(b, i), 0)),
        out_shape=jax.ShapeDtypeStruct((batch * seq, Q_COLS), BF16),
        compiler_params=_cparams("parallel", "arbitrary"),
        name="attn_prompt",
    )(sinks, qkv, qkv, qkv, qkv, qkv)


SAMPLE_ROWS = SUBLANES
SAMPLE_BATCH_BLOCK = 8


def _attn_sample_kernel(sink_ref, qkv_ref, ck_ref, cv_ref, o_ref, *, n_new):
    n_keys = WINDOW + SAMPLE_ROWS
    kk = lax.broadcasted_iota(jnp.int32, (n_keys, LANES), 0)
    lane = lax.broadcasted_iota(jnp.int32, (n_keys, LANES), 1)
    qi = lane % SAMPLE_ROWS
    new = kk - WINDOW
    mask = ((kk < WINDOW) & (kk >= qi)) | ((new >= 0) & (new <= qi) & (new < n_new))
    head = lax.broadcasted_iota(jnp.int32, (1, LANES), 1) // SAMPLE_ROWS
    nt = (((1,), (1,)), ((), ()))
    tn = (((0,), (0,)), ((), ()))
    q_pad = jnp.zeros((LANES - GQA_GROUP * SAMPLE_ROWS, HEAD_DIM), BF16)
    for b in range(qkv_ref.shape[0]):
        qkv = qkv_ref[b]
        for g in range(N_KV_HEADS):
            ks = slice(g * HEAD_DIM, (g + 1) * HEAD_DIM)
            k = jnp.concatenate([ck_ref[b, :, ks], qkv[:, Q_COLS + g * HEAD_DIM:Q_COLS + (g + 1) * HEAD_DIM]],
                                axis=0).astype(BF16)
            v = jnp.concatenate([cv_ref[b, :, ks],
                                 qkv[:, Q_COLS + KV_COLS + g * HEAD_DIM:Q_COLS + KV_COLS + (g + 1) * HEAD_DIM]],
                                axis=0).astype(BF16)
            q = jnp.concatenate(
                [qkv[:, (g * GQA_GROUP + j) * HEAD_DIM:(g * GQA_GROUP + j + 1) * HEAD_DIM].astype(BF16)
                 for j in range(GQA_GROUP)] + [q_pad], axis=0)
            t = lax.dot_general(k, q, nt, preferred_element_type=F32)
            t = jnp.where(mask, t * (HEAD_DIM ** -0.5 * LOG2E), NEG_INF)
            sink = jnp.zeros((1, LANES), F32)
            for j in range(GQA_GROUP):
                sink = jnp.where(head == j, sink_ref[g * GQA_GROUP + j] * LOG2E, sink)
            m = jnp.maximum(jnp.max(t, axis=0, keepdims=True), sink)
            p = jnp.exp2(t - m)
            denom = jnp.sum(p, axis=0, keepdims=True) + jnp.exp2(sink - m)
            o_t = lax.dot_general(v, p.astype(BF16), tn, preferred_element_type=F32) / denom
            o = o_t.T
            o_ref[b, :, g * GQA_GROUP * HEAD_DIM:(g + 1) * GQA_GROUP * HEAD_DIM] = jnp.concatenate(
                [o[j * SAMPLE_ROWS:(j + 1) * SAMPLE_ROWS] for j in range(GQA_GROUP)], axis=1)


def _attn_sample(qkv8, ck, cv, sinks, n_new):
    nb = qkv8.shape[0]
    bb = SAMPLE_BATCH_BLOCK
    assert nb % bb == 0 and n_new <= SAMPLE_ROWS
    return pl.pallas_call(
        functools.partial(_attn_sample_kernel, n_new=n_new),
        grid=(nb // bb,),
        in_specs=[
            pl.BlockSpec(memory_space=pltpu.SMEM),
            pl.BlockSpec((bb, SAMPLE_ROWS, QKV_COLS), lambda b: (b, 0, 0)),
            pl.BlockSpec((bb, WINDOW, KV_COLS), lambda b: (b, 0, 0)),
            pl.BlockSpec((bb, WINDOW, KV_COLS), lambda b: (b, 0, 0)),
        ],
        out_specs=pl.BlockSpec((bb, SAMPLE_ROWS, Q_COLS), lambda b: (b, 0, 0)),
        out_shape=jax.ShapeDtypeStruct((nb, SAMPLE_ROWS, Q_COLS), F32),
        compiler_params=_cparams("parallel"),
        name="attn_sample",
    )(sinks, qkv8, ck, cv)


def _proj_ln_kernel(a_ref, w_ref, b_ref, x_ref, g_ref, beta_ref, o_ref):
    y = jnp.dot(a_ref[...].astype(BF16), w_ref[...], preferred_element_type=F32) + b_ref[...]
    o_ref[...] = _layer_norm(DEEPNORM_ALPHA * x_ref[...] + y, g_ref[...], beta_ref[...])


def _proj_ln(a, w_bf, bias, x, g, beta, tm):
    n, k = a.shape
    row = lambda i: (i, 0)
    fixed = lambda i: (0, 0)
    return pl.pallas_call(
        _proj_ln_kernel,
        grid=(n // tm,),
        in_specs=[
            pl.BlockSpec((tm, k), row),
            pl.BlockSpec((k, D_MODEL), fixed),
            pl.BlockSpec((1, D_MODEL), fixed),
            pl.BlockSpec((tm, D_MODEL), row),
            pl.BlockSpec((1, D_MODEL), fixed),
            pl.BlockSpec((1, D_MODEL), fixed),
        ],
        out_specs=pl.BlockSpec((tm, D_MODEL), row),
        out_shape=jax.ShapeDtypeStruct((n, D_MODEL), F32),
        compiler_params=_cparams("parallel"),
        name="proj_ln",
    )(a, w_bf, bias.reshape(1, D_MODEL), x, g.reshape(1, D_MODEL), beta.reshape(1, D_MODEL))


CONV_TN = 512


def _conv_in_kernel(x_ref, wb_ref, wc_ref, wh_ref, gb_ref, u_ref):
    xb = x_ref[...].astype(BF16)
    gb_ref[...] = jnp.dot(xb, wb_ref[...], preferred_element_type=F32)
    gate_c = jnp.dot(xb, wc_ref[...], preferred_element_type=F32)
    h = jnp.dot(xb, wh_ref[...], preferred_element_type=F32)
    u_ref[...] = gate_c * h


def _conv_in(x, w_in_bf, tm):
    n = x.shape[0]
    nt = D_MODEL // CONV_TN
    return pl.pallas_call(
        _conv_in_kernel,
        grid=(n // tm, nt),
        in_specs=[
            pl.BlockSpec((tm, D_MODEL), lambda i, j: (i, 0)),
            pl.BlockSpec((D_MODEL, CONV_TN), lambda i, j: (0, j)),
            pl.BlockSpec((D_MODEL, CONV_TN), lambda i, j: (0, nt + j)),
            pl.BlockSpec((D_MODEL, CONV_TN), lambda i, j: (0, 2 * nt + j)),
        ],
        out_specs=[
            pl.BlockSpec((tm, CONV_TN), lambda i, j: (i, j)),
            pl.BlockSpec((tm, CONV_TN), lambda i, j: (i, j)),
        ],
        out_shape=[jax.ShapeDtypeStruct((n, D_MODEL), F32)] * 2,
        compiler_params=_cparams("parallel", "arbitrary"),
        name="conv_in",
    )(x, w_in_bf, w_in_bf, w_in_bf)


def _gated_out_ln(gate_b, conv, w_ref, x_ref, g_ref, beta_ref, o_ref):
    y = jnp.dot((gate_b * conv).astype(BF16), w_ref[...], preferred_element_type=F32)
    o_ref[...] = _layer_norm(DEEPNORM_ALPHA * x_ref[...] + y, g_ref[...], beta_ref[...])


def _conv_out_ln_kernel(gb_ref, u_ref, um1_ref, um2_ref, cw_ref, w_ref, x_ref, g_ref, beta_ref, o_ref):
    conv = um2_ref[...] * cw_ref[0:1, :]
    conv = conv + um1_ref[...] * cw_ref[1:2, :]
    conv = conv + u_ref[...] * cw_ref[2:3, :]
    _gated_out_ln(gb_ref[...], conv, w_ref, x_ref, g_ref, beta_ref, o_ref)


def _conv_out_ln_halo_kernel(gb_ref, u_ref, halo_ref, cw_ref, w_ref, x_ref, g_ref, beta_ref, o_ref,
                             *, tiles_per_seq):
    i = pl.program_id(0)
    u = u_ref[...]
    halo = jnp.where(i % tiles_per_seq == 0, 0.0, halo_ref[...])
    row = lax.broadcasted_iota(jnp.int32, u.shape, 0)
    last, before_last = halo[SUBLANES - 1:SUBLANES, :], halo[SUBLANES - 2:SUBLANES - 1, :]
    um1 = jnp.where(row == 0, last, pltpu.roll(u, 1, 0))
    um2 = jnp.where(row == 0, before_last, jnp.where(row == 1, last, pltpu.roll(u, 2, 0)))
    conv = um2 * cw_ref[0:1, :] + um1 * cw_ref[1:2, :] + u * cw_ref[2:3, :]
    _gated_out_ln(gb_ref[...], conv, w_ref, x_ref, g_ref, beta_ref, o_ref)


def _conv_out_ln(gate_b, u, shifted, conv_w, w_out_bf, x, g, beta, tm, seq=None):
    n = x.shape[0]
    row = lambda i: (i, 0)
    fixed = lambda i: (0, 0)
    tile = pl.BlockSpec((tm, D_MODEL), row)
    vec = pl.BlockSpec((1, D_MODEL), fixed)
    tail_specs = [pl.BlockSpec((CONV_WIDTH, D_MODEL), fixed), pl.BlockSpec((D_MODEL, D_MODEL), fixed),
                  tile, vec, vec]
    tail_args = (conv_w, w_out_bf, x, g.reshape(1, D_MODEL), beta.reshape(1, D_MODEL))
    if shifted is None:
        assert seq % tm == 0 and tm % SUBLANES == 0
        body = functools.partial(_conv_out_ln_halo_kernel, tiles_per_seq=seq // tm)
        halo = pl.BlockSpec((SUBLANES, D_MODEL), lambda i: (jnp.maximum(i * (tm // SUBLANES) - 1, 0), 0))
        in_specs, args = [tile, tile, halo], (gate_b, u, u)
    else:
        body = _conv_out_ln_kernel
        in_specs, args = [tile, tile, tile, tile], (gate_b, u) + tuple(shifted)
    return pl.pallas_call(
        body,
        grid=(n // tm,),
        in_specs=in_specs + tail_specs,
        out_specs=tile,
        out_shape=jax.ShapeDtypeStruct((n, D_MODEL), F32),
        compiler_params=_cparams("parallel"),
        name="conv_out_ln",
    )(*args, *tail_args)


def _oddeven_merge_sort_pairs(n):
    pairs = []
    p = 1
    while p < n:
        k = p
        while k >= 1:
            for j in range(k % p, n - k, 2 * k):
                for i in range(min(k, n - j - k)):
                    if (i + j) // (2 * p) == (i + j + k) // (2 * p):
                        pairs.append((i + j, i + j + k))
            k //= 2
        p *= 2
    return pairs


def _top_values(pieces, k):
    n = len(pieces)
    size = 1
    while size < n:
        size *= 2
    v = list(pieces)
    for i, j in _oddeven_merge_sort_pairs(size):
        if j < n:
            v[i], v[j] = jnp.maximum(v[i], v[j]), jnp.minimum(v[i], v[j])
    vals = []
    for it in range(k):
        m = jnp.max(v[0], axis=0, keepdims=True)
        vals.append(m)
        need = k - it - 1
        if need:
            hit = v[0] == m
            nxt = v[1:] + [jnp.full_like(v[0], -jnp.inf)]
            v = [jnp.where(hit, nxt[j], v[j]) for j in range(min(need, len(v)))]
    return vals


def _peer_score_kernel(x_ref, wq_ref, keys_ref, xt_ref, s1l_ref, s2l_ref, tau_ref):
    pad = jnp.zeros((I1_PER_TILE,) + s1l_ref.shape[1:], F32)
    s1l_ref[:I1_PER_TILE] = pad
    s1l_ref[I1_PER_TILE + PEER_KEYS:] = pad
    x = x_ref[...]
    xt_ref[...] = x.T.astype(BF16)
    q = jnp.dot(x.astype(BF16), wq_ref[...], preferred_element_type=F32).astype(BF16)
    nt = (((1,), (1,)), ((), ()))
    k = PEER_TOPK
    assert k == 16 and SUBLANES == 8
    for h in range(PEER_HEADS):
        c0 = h * 2 * PEER_HALF
        s1 = lax.dot_general(keys_ref[h, 0], q[:, c0:c0 + PEER_HALF], nt, preferred_element_type=F32)
        s2 = lax.dot_general(keys_ref[h, 1], q[:, c0 + PEER_HALF:c0 + 2 * PEER_HALF], nt,
                             preferred_element_type=F32)
        a = _top_values([s1[r:r + SUBLANES] for r in range(0, PEER_KEYS, SUBLANES)], k + 1)
        b = _top_values([s2[r:r + SUBLANES] for r in range(0, PEER_KEYS, SUBLANES)], k + 1)
        neg = jnp.full_like(a[0], -jnp.inf)
        b24 = jnp.concatenate(b + [neg] * 7, axis=0)
        b8 = jnp.concatenate(b[:8], axis=0)
        a16 = jnp.concatenate(a[8:] + [neg] * 7, axis=0)
        row = lax.broadcasted_iota(jnp.int32, b8.shape, 0)
        eb24 = jnp.exp(b24 - b[0])
        eb8 = jnp.exp(b8 - b[0])
        cand = [a[0] + b24[r:r + 8] for r in (0, 8, 16)]
        wgt = [eb24[r:r + 8] for r in (0, 8, 16)]
        for i in range(1, 8):
            cand.append(jnp.where(row < (k + 1) // (i + 1), a[i] + b8, -jnp.inf))
            wgt.append(jnp.exp(a[i] - a[0]) * eb8)
        cand += [a16[r:r + 8] + b[0] for r in (0, 8)]
        wgt += [jnp.exp(a16[r:r + 8] - a[0]) for r in (0, 8)]
        tops = _top_values(cand, k + 1)
        thr = 0.5 * (tops[k - 1] + tops[k])
        z8 = functools.reduce(jnp.add, [jnp.where(c >= thr, w, 0.0) for c, w in zip(cand, wgt)])
        z = jnp.sum(z8, axis=0, keepdims=True)
        shift = a[0] + b[0] + jnp.log(z)
        s1l_ref[I1_PER_TILE:I1_PER_TILE + PEER_KEYS, h, :] = (s1 - shift) * LOG2E - 1.0
        s2l_ref[h] = s2 * LOG2E
        tau_ref[h:h + 1, :] = (thr - shift) * LOG2E - 1.0


def _peer_scores(x, wq_bf, keys_bf, tm):
    n = x.shape[0]
    hk = (PEER_HEADS, PEER_KEYS, tm)
    hk_map = lambda i: (0, 0, i)
    hk_shape = jax.ShapeDtypeStruct((PEER_HEADS, PEER_KEYS, n), F32)
    return pl.pallas_call(
        _peer_score_kernel,
        grid=(n // tm,),
        in_specs=[
            pl.BlockSpec((tm, D_MODEL), lambda i: (i, 0)),
            pl.BlockSpec((D_MODEL, PEER_HEADS * 2 * PEER_HALF), lambda i: (0, 0)),
            pl.BlockSpec((PEER_HEADS, 2, PEER_KEYS, PEER_HALF), lambda i: (0, 0, 0, 0)),
        ],
        out_specs=[
            pl.BlockSpec((D_MODEL, tm), lambda i: (0, i)),
            pl.BlockSpec((PEER_KEYS + 2 * I1_PER_TILE, PEER_HEADS, tm), hk_map), pl.BlockSpec(hk, hk_map),
            pl.BlockSpec((PEER_HEADS, tm), lambda i: (0, i)),
        ],
        out_shape=[
            jax.ShapeDtypeStruct((D_MODEL, n), BF16),
            jax.ShapeDtypeStruct((PEER_KEYS + 2 * I1_PER_TILE, PEER_HEADS, n), F32), hk_shape,
            jax.ShapeDtypeStruct((PEER_HEADS, n), F32),
        ],
        compiler_params=_cparams("parallel"),
        name="peer_scores",
    )(x, wq_bf, keys_bf)


def _gelu_times_two(x):
    return x * (1.0 + lax.erf(x * (2.0 ** -0.5)))


N_PAIRS = PEER_EXPERTS // (2 * PEER_TE)


def _dense_pipeline_step(p, u_ref, v_ref, xt_ref, s1l_ref, s2l_ref, tau_ref, o_ref, hid, at):
    tm = xt_ref.shape[1]

    @pl.when(p == 0)
    def _():
        o_ref[...] = jnp.zeros_like(o_ref)
        for buf in hid + at:
            buf[...] = jnp.zeros_like(buf)

    for half in range(2):
        rows = slice(half * PEER_TE, (half + 1) * PEER_TE)
        hid[half][...] = jnp.dot(u_ref[rows, :].astype(BF16), xt_ref[...], preferred_element_type=F32)

        src, dst = hid[1 - half], at[1 - half]
        for il in range(I1_PER_TILE):
            r = half * I1_PER_TILE + il
            rs = slice(il * PEER_KEYS, (il + 1) * PEER_KEYS)
            for c in range(tm // LANES):
                cs = slice(c * LANES, (c + 1) * LANES)
                gate = jnp.zeros((PEER_KEYS, LANES), F32)
                for h in range(PEER_HEADS):
                    pair = s2l_ref[h, :, cs] + s1l_ref[r, h:h + 1, cs]
                    gate = gate + jnp.where(pair >= tau_ref[h:h + 1, cs], jnp.exp2(pair), 0.0)
                dst[rs, cs] = (_gelu_times_two(src[rs, cs]) * gate).astype(BF16)

        o_ref[...] += lax.dot_general(at[half][...], v_ref[rows, :].astype(BF16), (((0,), (0,)), ((), ())),
                                      preferred_element_type=F32)


def _peer_dense_kernel(u_ref, v_ref, xt_ref, s1l_ref, s2l_ref, tau_ref, xts_ref, s1ls_ref, s2ls_ref, taus_ref,
                       o_ref, os_ref, hid0, hid1, at0, at1, hids0, hids1, ats0, ats1):
    t = pl.program_id(0)
    p = pl.program_id(1)
    _dense_pipeline_step(p, u_ref, v_ref, xt_ref, s1l_ref, s2l_ref, tau_ref, o_ref, (hid0, hid1), (at0, at1))

    @pl.when(t == 0)
    def _():
        _dense_pipeline_step(p, u_ref, v_ref, xts_ref, s1ls_ref, s2ls_ref, taus_ref, os_ref,
                             (hids0, hids1), (ats0, ats1))


def _peer_dense(u_all, v_all, layer, prompt, sample, tm):
    n = prompt[0].shape[1]
    n_s = sample[0].shape[1]
    u_spec = pl.BlockSpec((None, 2 * PEER_TE, D_MODEL), lambda t, p: (layer, jnp.minimum(p, N_PAIRS - 1), 0))
    v_spec = pl.BlockSpec((None, 2 * PEER_TE, D_MODEL), lambda t, p: (layer, jnp.maximum(p - 1, 0), 0))

    def token_specs(width, tile_of):
        return [
            pl.BlockSpec((D_MODEL, width), lambda t, p: (0, tile_of(t))),
            pl.BlockSpec((2 * I1_PER_TILE, PEER_HEADS, width), lambda t, p: (p, 0, tile_of(t))),
            pl.BlockSpec((PEER_HEADS, PEER_KEYS, width), lambda t, p: (0, 0, tile_of(t)),
                         pipeline_mode=pl.Buffered(1)),
            pl.BlockSpec((PEER_HEADS, width), lambda t, p: (0, tile_of(t))),
        ]

    def buffers(width):
        return [pltpu.VMEM((PEER_TE, width), F32), pltpu.VMEM((PEER_TE, width), F32),
                pltpu.VMEM((PEER_TE, width), BF16), pltpu.VMEM((PEER_TE, width), BF16)]

    return pl.pallas_call(
        _peer_dense_kernel,
        grid=(n // tm, N_PAIRS + 1),
        in_specs=[u_spec, v_spec] + token_specs(tm, lambda t: t) + token_specs(n_s, lambda t: 0),
        out_specs=[pl.BlockSpec((tm, D_MODEL), lambda t, p: (t, 0)),
                   pl.BlockSpec((n_s, D_MODEL), lambda t, p: (0, 0))],
        out_shape=[jax.ShapeDtypeStruct((n, D_MODEL), F32), jax.ShapeDtypeStruct((n_s, D_MODEL), F32)],
        scratch_shapes=buffers(tm) + buffers(n_s),
        compiler_params=_cparams("arbitrary", "arbitrary"),
        name="peer_dense",
    )(u_all, v_all, *prompt, *sample)


def _res_ln_kernel(x_ref, f_ref, g_ref, beta_ref, o_ref):
    o_ref[...] = _layer_norm(DEEPNORM_ALPHA * x_ref[...] + f_ref[...], g_ref[...], beta_ref[...])


def _res_ln(x, f, g, beta, tm):
    n = x.shape[0]
    tile = pl.BlockSpec((tm, D_MODEL), lambda i: (i, 0))
    vec = pl.BlockSpec((1, D_MODEL), lambda i: (0, 0))
    return pl.pallas_call(
        _res_ln_kernel,
        grid=(n // tm,),
        in_specs=[tile, tile, vec, vec],
        out_specs=tile,
        out_shape=jax.ShapeDtypeStruct((n, D_MODEL), F32),
        compiler_params=_cparams("parallel"),
        name="res_ln",
    )(x, f, g.reshape(1, D_MODEL), beta.reshape(1, D_MODEL))


def _peer_layer(xp, xs, wq_bf, keys_bf, u_all, v_all, layer, g, beta):
    n_p, n_s = xp.shape[0], xs.shape[0]
    prompt = _peer_scores(xp, wq_bf, keys_bf, _rows(n_p, SCORE_ROWS))
    sample = _peer_scores(xs, wq_bf, keys_bf, _rows(n_s, SCORE_ROWS))
    fp, fs = _peer_dense(u_all, v_all, layer, prompt, sample, _rows(n_p, DENSE_ROWS))
    return (_res_ln(xp, fp, g, beta, _rows(n_p, RES_LN_ROWS)),
            _res_ln(xs, fs, g, beta, _rows(n_s, RES_LN_ROWS)))


def _rope_tables(pos):
    half = HEAD_DIM // 2
    inv = jnp.power(ROPE_THETA, -jnp.arange(half, dtype=F32) * 2.0 / HEAD_DIM)
    ang = pos.astype(F32)[:, None] * inv[None, :]
    cos = jnp.cos(ang)
    sin = jnp.sin(ang)
    reps = LANES // HEAD_DIM
    return (jnp.tile(jnp.concatenate([cos, cos], axis=1), (1, reps)),
            jnp.tile(jnp.concatenate([-sin, sin], axis=1), (1, reps)))


def _shifted(state, u3):
    t = u3.shape[1]
    up = jnp.concatenate([state, u3], axis=1)
    return up[:, 1:t + 1], up[:, 0:t], up[:, -(CONV_WIDTH - 1):]


def kernel(x_prompt, x_sample, cache_k, cache_v, state_conv, w_qkv, b_qkv, w_o, b_o, attn_sinks,
           w_conv_in, conv_w, w_conv_out, w_peer_q, peer_sub_keys, peer_u, peer_v, ln_g, ln_b):
    batch, seq, _ = x_prompt.shape
    dec_batch, dec_seq, _ = x_sample.shape
    n_p = batch * seq
    n_s = dec_batch * dec_seq
    xp = x_prompt.reshape(n_p, D_MODEL)
    xs = x_sample.reshape(n_s, D_MODEL)
    cos_p, sin_p = _rope_tables(jnp.tile(jnp.arange(seq), batch))
    cos_s, sin_s = _rope_tables(jnp.tile(PAST_LEN + jnp.arange(dec_seq), dec_batch))

    new_kp, new_vp, new_cp, new_ks, new_vs, new_cs = [], [], [], [], [], []
    for layer in range(DEPTH):
        j = layer // 2
        g0, b0 = ln_g[layer, 0], ln_b[layer, 0]
        if layer % 2 == 0:
            wqkv = w_qkv[j].astype(BF16)
            wo = w_o[j].astype(BF16)
            sinks = attn_sinks[j].reshape(N_Q_HEADS)
            qkv_p = _qkv_rope(xp, wqkv, b_qkv[j], cos_p, sin_p, _rows(n_p, QKV_ROWS))
            qkv_s = _qkv_rope(xs, wqkv, b_qkv[j], cos_s, sin_s, _rows(n_s, QKV_ROWS))
            o_p = _attn_prompt(qkv_p, sinks, batch, seq)
            qkv_s3 = qkv_s.reshape(dec_batch, dec_seq, QKV_COLS)
            qkv_s8 = jnp.pad(qkv_s3, ((0, 0), (0, SAMPLE_ROWS - dec_seq), (0, 0)))
            ck = cache_k[j].reshape(dec_batch, WINDOW, KV_COLS)
            cv = cache_v[j].reshape(dec_batch, WINDOW, KV_COLS)
            o_s = _attn_sample(qkv_s8, ck, cv, sinks, dec_seq)[:, :dec_seq].reshape(n_s, Q_COLS)
            kv_p = qkv_p.reshape(batch, seq, QKV_COLS)[:, -WINDOW:, Q_COLS:]
            new_kp.append(kv_p[..., :KV_COLS].reshape(batch, WINDOW, N_KV_HEADS, HEAD_DIM))
            new_vp.append(kv_p[..., KV_COLS:].reshape(batch, WINDOW, N_KV_HEADS, HEAD_DIM))
            k_s = jnp.concatenate([ck, qkv_s3[..., Q_COLS:Q_COLS + KV_COLS]], axis=1)[:, -WINDOW:]
            v_s = jnp.concatenate([cv, qkv_s3[..., Q_COLS + KV_COLS:]], axis=1)[:, -WINDOW:]
            new_ks.append(k_s.reshape(dec_batch, WINDOW, N_KV_HEADS, HEAD_DIM))
            new_vs.append(v_s.reshape(dec_batch, WINDOW, N_KV_HEADS, HEAD_DIM))
            xp = _proj_ln(o_p, wo, b_o[j], xp, g0, b0, _rows(n_p, PROJ_ROWS))
            xs = _proj_ln(o_s, wo, b_o[j], xs, g0, b0, _rows(n_s, PROJ_ROWS))
        else:
            w_in = w_conv_in[j].astype(BF16)
            w_out = w_conv_out[j].astype(BF16)
            gb_p, u_p = _conv_in(xp, w_in, _rows(n_p, CONV_IN_ROWS))
            gb_s, u_s = _conv_in(xs, w_in, _rows(n_s, CONV_IN_ROWS))
            um1_s, um2_s, cs = _shifted(state_conv[j], u_s.reshape(dec_batch, dec_seq, D_MODEL))
            new_cp.append(u_p.reshape(batch, seq, D_MODEL)[:, -(CONV_WIDTH - 1):])
            new_cs.append(cs)
            xp = _conv_out_ln(gb_p, u_p, None, conv_w[j], w_out, xp, g0, b0, _rows(n_p, CONV_OUT_ROWS), seq=seq)
            xs = _conv_out_ln(gb_s, u_s, (um1_s.reshape(n_s, D_MODEL), um2_s.reshape(n_s, D_MODEL)),
                              conv_w[j], w_out, xs, g0, b0, _rows(n_s, CONV_OUT_ROWS))
        wq = w_peer_q[layer].astype(BF16)
        keys = peer_sub_keys[layer].astype(BF16)
        g1, b1 = ln_g[layer, 1], ln_b[layer, 1]
        xp, xs = _peer_layer(xp, xs, wq, keys, peer_u, peer_v, layer, g1, b1)

    return (xp.reshape(batch, seq, D_MODEL), xs.reshape(dec_batch, dec_seq, D_MODEL),
            jnp.stack(new_kp), jnp.stack(new_vp), jnp.stack(new_cp),
            jnp.stack(new_ks), jnp.stack(new_vs), jnp.stack(new_cs))
```

```python
import functools

import jax
import jax.numpy as jnp
from jax import lax
from jax.experimental import pallas as pl
from jax.experimental.pallas import tpu as pltpu

F32 = jnp.float32
BF16 = jnp.bfloat16

D_MODEL = 2048
DEPTH = 4
HEAD_DIM = 64
N_Q_HEADS = 32
N_KV_HEADS = 8
GQA_GROUP = 4
WINDOW = 128
ROPE_THETA = 10000.0
PAST_LEN = 16384
CONV_WIDTH = 3
PEER_HEADS = 8
PEER_KEYS = 128
PEER_EXPERTS = PEER_KEYS * PEER_KEYS
PEER_HALF = 128
PEER_TOPK = 16
LN_EPS = 1e-5
NEG_INF = -1e30
DEEPNORM_ALPHA = (2 * DEPTH) ** 0.25
LOG2E = 1.4426950408889634

Q_COLS = N_Q_HEADS * HEAD_DIM
KV_COLS = N_KV_HEADS * HEAD_DIM
QKV_COLS = Q_COLS + 2 * KV_COLS

LANES = 128
SUBLANES = 8
PEER_TE = 256
I1_PER_TILE = PEER_TE // PEER_KEYS
VMEM_LIMIT_BYTES = 56 * 1024 * 1024

QKV_ROWS = 1024
PROJ_ROWS = 512
CONV_IN_ROWS = 1024
CONV_OUT_ROWS = 256
SCORE_ROWS = 256
DENSE_ROWS = 1024
RES_LN_ROWS = 512


def _rows(n, preferred):
    return min(n, preferred)


def _cparams(*semantics):
    return pltpu.CompilerParams(dimension_semantics=semantics, vmem_limit_bytes=VMEM_LIMIT_BYTES)


def _layer_norm(z, g, b):
    mu = jnp.mean(z, axis=-1, keepdims=True)
    zc = z - mu
    var = jnp.mean(zc * zc, axis=-1, keepdims=True)
    return zc * lax.rsqrt(var + LN_EPS) * g + b


QKV_TN = 512
N_ROPE_TILES = (Q_COLS + KV_COLS) // QKV_TN


def _qkv_kernel(x_ref, w_ref, b_ref, cos_ref, sin_ref, o_ref):
    j = pl.program_id(1)
    acc = jnp.dot(x_ref[...].astype(BF16), w_ref[...], preferred_element_type=F32) + b_ref[...]

    @pl.when(j < N_ROPE_TILES)
    def _():
        cos = cos_ref[...]
        sin = sin_ref[...]
        lane = lax.broadcasted_iota(jnp.int32, cos.shape, 1)
        first_half = (lane % HEAD_DIM) < (HEAD_DIM // 2)
        for c in range(QKV_TN // LANES):
            seg = acc[:, c * LANES:(c + 1) * LANES]
            partner = jnp.where(first_half,
                                pltpu.roll(seg, LANES - HEAD_DIM // 2, 1),
                                pltpu.roll(seg, HEAD_DIM // 2, 1))
            o_ref[:, c * LANES:(c + 1) * LANES] = seg * cos + partner * sin

    @pl.when(j >= N_ROPE_TILES)
    def _():
        o_ref[...] = acc


def _qkv_rope(x, w_bf, b, cos, sin, tm):
    n = x.shape[0]
    return pl.pallas_call(
        _qkv_kernel,
        grid=(n // tm, QKV_COLS // QKV_TN),
        in_specs=[
            pl.BlockSpec((tm, D_MODEL), lambda i, j: (i, 0)),
            pl.BlockSpec((D_MODEL, QKV_TN), lambda i, j: (0, j)),
            pl.BlockSpec((1, QKV_TN), lambda i, j: (0, j)),
            pl.BlockSpec((tm, LANES), lambda i, j: (i, 0)),
            pl.BlockSpec((tm, LANES), lambda i, j: (i, 0)),
        ],
        out_specs=pl.BlockSpec((tm, QKV_TN), lambda i, j: (i, j)),
        out_shape=jax.ShapeDtypeStruct((n, QKV_COLS), F32),
        compiler_params=_cparams("parallel", "arbitrary"),
        name="qkv_rope",
    )(x, w_bf, b.reshape(1, QKV_COLS), cos, sin)


def _attn_prompt_kernel(sink_ref, q_ref, kp_ref, kc_ref, vp_ref, vc_ref, o_ref):
    i = pl.program_id(1)
    cols = GQA_GROUP * WINDOW
    kj = lax.broadcasted_iota(jnp.int32, (2 * WINDOW, cols), 0)
    qi = lax.broadcasted_iota(jnp.int32, (2 * WINDOW, cols), 1) % WINDOW
    dist = qi + WINDOW - kj
    prev_ok = jnp.where(i > 0, 0, WINDOW)
    mask = (dist >= 0) & (dist <= WINDOW) & (kj >= prev_ok)
    nt = (((1,), (1,)), ((), ()))
    tn = (((0,), (0,)), ((), ()))
    for g in range(N_KV_HEADS):
        ks = slice(g * HEAD_DIM, (g + 1) * HEAD_DIM)
        k = jnp.concatenate([kp_ref[:, ks], kc_ref[:, ks]], axis=0).astype(BF16)
        v = jnp.concatenate([vp_ref[:, ks], vc_ref[:, ks]], axis=0).astype(BF16)
        q = jnp.concatenate(
            [q_ref[:, (g * GQA_GROUP + j) * HEAD_DIM:(g * GQA_GROUP + j + 1) * HEAD_DIM]
             for j in range(GQA_GROUP)], axis=0).astype(BF16)
        t = lax.dot_general(k, q, nt, preferred_element_type=F32)
        t = jnp.where(mask, t * (HEAD_DIM ** -0.5 * LOG2E), NEG_INF)
        sink = jnp.concatenate(
            [jnp.full((1, WINDOW), sink_ref[g * GQA_GROUP + j] * LOG2E, F32) for j in range(GQA_GROUP)], axis=1)
        m = jnp.maximum(jnp.max(t, axis=0, keepdims=True), sink)
        p = jnp.exp2(t - m)
        denom = jnp.sum(p, axis=0, keepdims=True) + jnp.exp2(sink - m)
        o_t = lax.dot_general(v, p.astype(BF16), tn, preferred_element_type=F32) / denom
        o = o_t.T
        o_ref[:, g * GQA_GROUP * HEAD_DIM:(g + 1) * GQA_GROUP * HEAD_DIM] = jnp.concatenate(
            [o[j * WINDOW:(j + 1) * WINDOW] for j in range(GQA_GROUP)], axis=1).astype(o_ref.dtype)


def _attn_prompt(qkv, sinks, batch, seq):
    nb = seq // WINDOW
    kcol = Q_COLS // KV_COLS
    vcol = kcol + 1

    def cur(b, i):
        return b * nb + i

    def prev(b, i):
        return b * nb + jnp.maximum(i - 1, 0)

    return pl.pallas_call(
        _attn_prompt_kernel,
        grid=(batch, nb),
        in_specs=[
            pl.BlockSpec(memory_space=pltpu.SMEM),
            pl.BlockSpec((WINDOW, Q_COLS), lambda b, i: (cur(b, i), 0)),
            pl.BlockSpec((WINDOW, KV_COLS), lambda b, i: (prev(b, i), kcol)),
            pl.BlockSpec((WINDOW, KV_COLS), lambda b, i: (cur(b, i), kcol)),
            pl.BlockSpec((WINDOW, KV_COLS), lambda b, i: (prev(b, i), vcol)),
            pl.BlockSpec((WINDOW, KV_COLS), lambda b, i: (cur(b, i), vcol)),
        ],
        out_specs=pl.BlockSpec((WINDOW, Q_COLS), lambda b, i: (cur(b, i), 0)),
        out_shape=jax.ShapeDtypeStruct((batch * seq, Q_COLS), BF16),
        compiler_params=_cparams("parallel", "arbitrary"),
        name="attn_prompt",
    )(sinks, qkv, qkv, qkv, qkv, qkv)


SAMPLE_ROWS = SUBLANES
SAMPLE_BATCH_BLOCK = 8


def _attn_sample_kernel(sink_ref, qkv_ref, ck_ref, cv_ref, o_ref, *, n_new):
    n_keys = WINDOW + SAMPLE_ROWS
    kk = lax.broadcasted_iota(jnp.int32, (n_keys, LANES), 0)
    lane = lax.broadcasted_iota(jnp.int32, (n_keys, LANES), 1)
    qi = lane % SAMPLE_ROWS
    new = kk - WINDOW
    mask = ((kk < WINDOW) & (kk >= qi)) | ((new >= 0) & (new <= qi) & (new < n_new))
    head = lax.broadcasted_iota(jnp.int32, (1, LANES), 1) // SAMPLE_ROWS
    nt = (((1,), (1,)), ((), ()))
    tn = (((0,), (0,)), ((), ()))
    q_pad = jnp.zeros((LANES - GQA_GROUP * SAMPLE_ROWS, HEAD_DIM), BF16)
    for b in range(qkv_ref.shape[0]):
        qkv = qkv_ref[b]
        for g in range(N_KV_HEADS):
            ks = slice(g * HEAD_DIM, (g + 1) * HEAD_DIM)
            k = jnp.concatenate([ck_ref[b, :, ks], qkv[:, Q_COLS + g * HEAD_DIM:Q_COLS + (g + 1) * HEAD_DIM]],
                                axis=0).astype(BF16)
            v = jnp.concatenate([cv_ref[b, :, ks],
                                 qkv[:, Q_COLS + KV_COLS + g * HEAD_DIM:Q_COLS + KV_COLS + (g + 1) * HEAD_DIM]],
                                axis=0).astype(BF16)
            q = jnp.concatenate(
                [qkv[:, (g * GQA_GROUP + j) * HEAD_DIM:(g * GQA_GROUP + j + 1) * HEAD_DIM].astype(BF16)
                 for j in range(GQA_GROUP)] + [q_pad], axis=0)
            t = lax.dot_general(k, q, nt, preferred_element_type=F32)
            t = jnp.where(mask, t * (HEAD_DIM ** -0.5 * LOG2E), NEG_INF)
            sink = jnp.zeros((1, LANES), F32)
            for j in range(GQA_GROUP):
                sink = jnp.where(head == j, sink_ref[g * GQA_GROUP + j] * LOG2E, sink)
            m = jnp.maximum(jnp.max(t, axis=0, keepdims=True), sink)
            p = jnp.exp2(t - m)
            denom = jnp.sum(p, axis=0, keepdims=True) + jnp.exp2(sink - m)
            o_t = lax.dot_general(v, p.astype(BF16), tn, preferred_element_type=F32) / denom
            o = o_t.T
            o_ref[b, :, g * GQA_GROUP * HEAD_DIM:(g + 1) * GQA_GROUP * HEAD_DIM] = jnp.concatenate(
                [o[j * SAMPLE_ROWS:(j + 1) * SAMPLE_ROWS] for j in range(GQA_GROUP)], axis=1)


def _attn_sample(qkv8, ck, cv, sinks, n_new):
    nb = qkv8.shape[0]
    bb = SAMPLE_BATCH_BLOCK
    assert nb % bb == 0 and n_new <= SAMPLE_ROWS
    return pl.pallas_call(
        functools.partial(_attn_sample_kernel, n_new=n_new),
        grid=(nb // bb,),
        in_specs=[
            pl.BlockSpec(memory_space=pltpu.SMEM),
            pl.BlockSpec((bb, SAMPLE_ROWS, QKV_COLS), lambda b: (b, 0, 0)),
            pl.BlockSpec((bb, WINDOW, KV_COLS), lambda b: (b, 0, 0)),
            pl.BlockSpec((bb, WINDOW, KV_COLS), lambda b: (b, 0, 0)),
        ],
        out_specs=pl.BlockSpec((bb, SAMPLE_ROWS, Q_COLS), lambda b: (b, 0, 0)),
        out_shape=jax.ShapeDtypeStruct((nb, SAMPLE_ROWS, Q_COLS), F32),
        compiler_params=_cparams("parallel"),
        name="attn_sample",
    )(sinks, qkv8, ck, cv)


def _proj_ln_kernel(a_ref, w_ref, b_ref, x_ref, g_ref, beta_ref, o_ref):
    y = jnp.dot(a_ref[...].astype(BF16), w_ref[...], preferred_element_type=F32) + b_ref[...]
    o_ref[...] = _layer_norm(DEEPNORM_ALPHA * x_ref[...] + y, g_ref[...], beta_ref[...])


def _proj_ln(a, w_bf, bias, x, g, beta, tm):
    n, k = a.shape
    row = lambda i: (i, 0)
    fixed = lambda i: (0, 0)
    return pl.pallas_call(
        _proj_ln_kernel,
        grid=(n // tm,),
        in_specs=[
            pl.BlockSpec((tm, k), row),
            pl.BlockSpec((k, D_MODEL), fixed),
            pl.BlockSpec((1, D_MODEL), fixed),
            pl.BlockSpec((tm, D_MODEL), row),
            pl.BlockSpec((1, D_MODEL), fixed),
            pl.BlockSpec((1, D_MODEL), fixed),
        ],
        out_specs=pl.BlockSpec((tm, D_MODEL), row),
        out_shape=jax.ShapeDtypeStruct((n, D_MODEL), F32),
        compiler_params=_cparams("parallel"),
        name="proj_ln",
    )(a, w_bf, bias.reshape(1, D_MODEL), x, g.reshape(1, D_MODEL), beta.reshape(1, D_MODEL))


CONV_TN = 512


def _conv_in_kernel(x_ref, wb_ref, wc_ref, wh_ref, gb_ref, u_ref):
    xb = x_ref[...].astype(BF16)
    gb_ref[...] = jnp.dot(xb, wb_ref[...], preferred_element_type=F32)
    gate_c = jnp.dot(xb, wc_ref[...], preferred_element_type=F32)
    h = jnp.dot(xb, wh_ref[...], preferred_element_type=F32)
    u_ref[...] = gate_c * h


def _conv_in(x, w_in_bf, tm):
    n = x.shape[0]
    nt = D_MODEL // CONV_TN
    return pl.pallas_call(
        _conv_in_kernel,
        grid=(n // tm, nt),
        in_specs=[
            pl.BlockSpec((tm, D_MODEL), lambda i, j: (i, 0)),
            pl.BlockSpec((D_MODEL, CONV_TN), lambda i, j: (0, j)),
            pl.BlockSpec((D_MODEL, CONV_TN), lambda i, j: (0, nt + j)),
            pl.BlockSpec((D_MODEL, CONV_TN), lambda i, j: (0, 2 * nt + j)),
        ],
        out_specs=[
            pl.BlockSpec((tm, CONV_TN), lambda i, j: (i, j)),
            pl.BlockSpec((tm, CONV_TN), lambda i, j: (i, j)),
        ],
        out_shape=[jax.ShapeDtypeStruct((n, D_MODEL), F32)] * 2,
        compiler_params=_cparams("parallel", "arbitrary"),
        name="conv_in",
    )(x, w_in_bf, w_in_bf, w_in_bf)


def _gated_out_ln(gate_b, conv, w_ref, x_ref, g_ref, beta_ref, o_ref):
    y = jnp.dot((gate_b * conv).astype(BF16), w_ref[...], preferred_element_type=F32)
    o_ref[...] = _layer_norm(DEEPNORM_ALPHA * x_ref[...] + y, g_ref[...], beta_ref[...])


def _conv_out_ln_kernel(gb_ref, u_ref, um1_ref, um2_ref, cw_ref, w_ref, x_ref, g_ref, beta_ref, o_ref):
    conv = um2_ref[...] * cw_ref[0:1, :]
    conv = conv + um1_ref[...] * cw_ref[1:2, :]
    conv = conv + u_ref[...] * cw_ref[2:3, :]
    _gated_out_ln(gb_ref[...], conv, w_ref, x_ref, g_ref, beta_ref, o_ref)


def _conv_out_ln_halo_kernel(gb_ref, u_ref, halo_ref, cw_ref, w_ref, x_ref, g_ref, beta_ref, o_ref,
                             *, tiles_per_seq):
    i = pl.program_id(0)
    u = u_ref[...]
    halo = jnp.where(i % tiles_per_seq == 0, 0.0, halo_ref[...])
    row = lax.broadcasted_iota(jnp.int32, u.shape, 0)
    last, before_last = halo[SUBLANES - 1:SUBLANES, :], halo[SUBLANES - 2:SUBLANES - 1, :]
    um1 = jnp.where(row == 0, last, pltpu.roll(u, 1, 0))
    um2 = jnp.where(row == 0, before_last, jnp.where(row == 1, last, pltpu.roll(u, 2, 0)))
    conv = um2 * cw_ref[0:1, :] + um1 * cw_ref[1:2, :] + u * cw_ref[2:3, :]
    _gated_out_ln(gb_ref[...], conv, w_ref, x_ref, g_ref, beta_ref, o_ref)


def _conv_out_ln(gate_b, u, shifted, conv_w, w_out_bf, x, g, beta, tm, seq=None):
    n = x.shape[0]
    row = lambda i: (i, 0)
    fixed = lambda i: (0, 0)
    tile = pl.BlockSpec((tm, D_MODEL), row)
    vec = pl.BlockSpec((1, D_MODEL), fixed)
    tail_specs = [pl.BlockSpec((CONV_WIDTH, D_MODEL), fixed), pl.BlockSpec((D_MODEL, D_MODEL), fixed),
                  tile, vec, vec]
    tail_args = (conv_w, w_out_bf, x, g.reshape(1, D_MODEL), beta.reshape(1, D_MODEL))
    if shifted is None:
        assert seq % tm == 0 and tm % SUBLANES == 0
        body = functools.partial(_conv_out_ln_halo_kernel, tiles_per_seq=seq // tm)
        halo = pl.BlockSpec((SUBLANES, D_MODEL), lambda i: (jnp.maximum(i * (tm // SUBLANES) - 1, 0), 0))
        in_specs, args = [tile, tile, halo], (gate_b, u, u)
    else:
        body = _conv_out_ln_kernel
        in_specs, args = [tile, tile, tile, tile], (gate_b, u) + tuple(shifted)
    return pl.pallas_call(
        body,
        grid=(n // tm,),
        in_specs=in_specs + tail_specs,
        out_specs=tile,
        out_shape=jax.ShapeDtypeStruct((n, D_MODEL), F32),
        compiler_params=_cparams("parallel"),
        name="conv_out_ln",
    )(*args, *tail_args)


def _oddeven_merge_sort_pairs(n):
    pairs = []
    p = 1
    while p < n:
        k = p
        while k >= 1:
            for j in range(k % p, n - k, 2 * k):
                for i in range(min(k, n - j - k)):
                    if (i + j) // (2 * p) == (i + j + k) // (2 * p):
                        pairs.append((i + j, i + j + k))
            k //= 2
        p *= 2
    return pairs


def _top_values(pieces, k):
    n = len(pieces)
    size = 1
    while size < n:
        size *= 2
    v = list(pieces)
    for i, j in _oddeven_merge_sort_pairs(size):
        if j < n:
            v[i], v[j] = jnp.maximum(v[i], v[j]), jnp.minimum(v[i], v[j])
    vals = []
    for it in range(k):
        m = jnp.max(v[0], axis=0, keepdims=True)
        vals.append(m)
        need = k - it - 1
        if need:
            hit = v[0] == m
            nxt = v[1:] + [jnp.full_like(v[0], -jnp.inf)]
            v = [jnp.where(hit, nxt[j], v[j]) for j in range(min(need, len(v)))]
    return vals


def _peer_score_kernel(x_ref, wq_ref, keys_ref, xb_ref, s1l_ref, s2l_ref, tau_ref):
    pad = jnp.zeros((I1_PER_TILE,) + s1l_ref.shape[1:], F32)
    s1l_ref[:I1_PER_TILE] = pad
    s1l_ref[I1_PER_TILE + PEER_KEYS:] = pad
    xb = x_ref[...].astype(BF16)
    xb_ref[...] = xb
    q = jnp.dot(xb, wq_ref[...], preferred_element_type=F32).astype(BF16)
    nt = (((1,), (1,)), ((), ()))
    k = PEER_TOPK
    assert k == 16 and SUBLANES == 8
    for h in range(PEER_HEADS):
        c0 = h * 2 * PEER_HALF
        s1 = lax.dot_general(keys_ref[h, 0], q[:, c0:c0 + PEER_HALF], nt, preferred_element_type=F32)
        s2 = lax.dot_general(keys_ref[h, 1], q[:, c0 + PEER_HALF:c0 + 2 * PEER_HALF], nt,
                             preferred_element_type=F32)
        a = _top_values([s1[r:r + SUBLANES] for r in range(0, PEER_KEYS, SUBLANES)], k + 1)
        b = _top_values([s2[r:r + SUBLANES] for r in range(0, PEER_KEYS, SUBLANES)], k + 1)
        neg = jnp.full_like(a[0], -jnp.inf)
        b24 = jnp.concatenate(b + [neg] * 7, axis=0)
        b8 = jnp.concatenate(b[:8], axis=0)
        a16 = jnp.concatenate(a[8:] + [neg] * 7, axis=0)
        row = lax.broadcasted_iota(jnp.int32, b8.shape, 0)
        eb24 = jnp.exp(b24 - b[0])
        eb8 = jnp.exp(b8 - b[0])
        cand = [a[0] + b24[r:r + 8] for r in (0, 8, 16)]
        wgt = [eb24[r:r + 8] for r in (0, 8, 16)]
        for i in range(1, 8):
            cand.append(jnp.where(row < (k + 1) // (i + 1), a[i] + b8, -jnp.inf))
            wgt.append(jnp.exp(a[i] - a[0]) * eb8)
        cand += [a16[r:r + 8] + b[0] for r in (0, 8)]
        wgt += [jnp.exp(a16[r:r + 8] - a[0]) for r in (0, 8)]
        tops = _top_values(cand, k + 1)
        thr = 0.5 * (tops[k - 1] + tops[k])
        z8 = functools.reduce(jnp.add, [jnp.where(c >= thr, w, 0.0) for c, w in zip(cand, wgt)])
        z = jnp.sum(z8, axis=0, keepdims=True)
        shift = a[0] + b[0] + jnp.log(z)
        s1l_ref[I1_PER_TILE:I1_PER_TILE + PEER_KEYS, h, :] = (s1 - shift) * LOG2E - 1.0
        s2l_ref[h] = s2 * LOG2E
        tau_ref[h:h + 1, :] = (thr - shift) * LOG2E - 1.0


def _peer_scores(x, wq_bf, keys_bf, tm):
    n = x.shape[0]
    hk = (PEER_HEADS, PEER_KEYS, tm)
    hk_map = lambda i: (0, 0, i)
    hk_shape = jax.ShapeDtypeStruct((PEER_HEADS, PEER_KEYS, n), F32)
    return pl.pallas_call(
        _peer_score_kernel,
        grid=(n // tm,),
        in_specs=[
            pl.BlockSpec((tm, D_MODEL), lambda i: (i, 0)),
            pl.BlockSpec((D_MODEL, PEER_HEADS * 2 * PEER_HALF), lambda i: (0, 0)),
            pl.BlockSpec((PEER_HEADS, 2, PEER_KEYS, PEER_HALF), lambda i: (0, 0, 0, 0)),
        ],
        out_specs=[
            pl.BlockSpec((tm, D_MODEL), lambda i: (i, 0)),
            pl.BlockSpec((PEER_KEYS + 2 * I1_PER_TILE, PEER_HEADS, tm), hk_map), pl.BlockSpec(hk, hk_map),
            pl.BlockSpec((PEER_HEADS, tm), lambda i: (0, i)),
        ],
        out_shape=[
            jax.ShapeDtypeStruct((n, D_MODEL), BF16),
            jax.ShapeDtypeStruct((PEER_KEYS + 2 * I1_PER_TILE, PEER_HEADS, n), F32), hk_shape,
            jax.ShapeDtypeStruct((PEER_HEADS, n), F32),
        ],
        compiler_params=_cparams("parallel"),
        name="peer_scores",
    )(x, wq_bf, keys_bf)


def _gelu_times_two(x):
    return x * (1.0 + lax.erf(x * (2.0 ** -0.5)))


N_PAIRS = PEER_EXPERTS // (2 * PEER_TE)


def _dense_pipeline_step(p, u_ref, v_ref, xb_ref, s1l_ref, s2l_ref, tau_ref, o_ref, hid, at):
    tm = xb_ref.shape[0]

    @pl.when(p == 0)
    def _():
        o_ref[...] = jnp.zeros_like(o_ref)
        for buf in hid + at:
            buf[...] = jnp.zeros_like(buf)

    for half in range(2):
        rows = slice(half * PEER_TE, (half + 1) * PEER_TE)
        hid[half][...] = lax.dot_general(u_ref[rows, :].astype(BF16), xb_ref[...], (((1,), (1,)), ((), ())),
                                         preferred_element_type=F32)

        src, dst = hid[1 - half], at[1 - half]
        for il in range(I1_PER_TILE):
            r = half * I1_PER_TILE + il
            rs = slice(il * PEER_KEYS, (il + 1) * PEER_KEYS)
            for c in range(tm // LANES):
                cs = slice(c * LANES, (c + 1) * LANES)
                gate = jnp.zeros((PEER_KEYS, LANES), F32)
                for h in range(PEER_HEADS):
                    pair = s2l_ref[h, :, cs] + s1l_ref[r, h:h + 1, cs]
                    gate = gate + jnp.where(pair >= tau_ref[h:h + 1, cs], jnp.exp2(pair), 0.0)
                dst[rs, cs] = (_gelu_times_two(src[rs, cs]) * gate).astype(BF16)

        o_ref[...] += lax.dot_general(at[half][...], v_ref[rows, :].astype(BF16), (((0,), (0,)), ((), ())),
                                      preferred_element_type=F32)


def _peer_dense_kernel(u_ref, v_ref, xb_ref, s1l_ref, s2l_ref, tau_ref, xbs_ref, s1ls_ref, s2ls_ref, taus_ref,
                       o_ref, os_ref, hid0, hid1, at0, at1, hids0, hids1, ats0, ats1):
    t = pl.program_id(0)
    p = pl.program_id(1)
    _dense_pipeline_step(p, u_ref, v_ref, xb_ref, s1l_ref, s2l_ref, tau_ref, o_ref, (hid0, hid1), (at0, at1))

    @pl.when(t == 0)
    def _():
        _dense_pipeline_step(p, u_ref, v_ref, xbs_ref, s1ls_ref, s2ls_ref, taus_ref, os_ref,
                             (hids0, hids1), (ats0, ats1))


def _peer_dense(u_all, v_all, layer, prompt, sample, tm):
    n = prompt[0].shape[0]
    n_s = sample[0].shape[0]
    u_spec = pl.BlockSpec((None, 2 * PEER_TE, D_MODEL), lambda t, p: (layer, jnp.minimum(p, N_PAIRS - 1), 0))
    v_spec = pl.BlockSpec((None, 2 * PEER_TE, D_MODEL), lambda t, p: (layer, jnp.maximum(p - 1, 0), 0))

    def token_specs(width, tile_of):
        return [
            pl.BlockSpec((width, D_MODEL), lambda t, p: (tile_of(t), 0)),
            pl.BlockSpec((2 * I1_PER_TILE, PEER_HEADS, width), lambda t, p: (p, 0, tile_of(t))),
            pl.BlockSpec((PEER_HEADS, PEER_KEYS, width), lambda t, p: (0, 0, tile_of(t)),
                         pipeline_mode=pl.Buffered(1)),
            pl.BlockSpec((PEER_HEADS, width), lambda t, p: (0, tile_of(t))),
        ]

    def buffers(width):
        return [pltpu.VMEM((PEER_TE, width), F32), pltpu.VMEM((PEER_TE, width), F32),
                pltpu.VMEM((PEER_TE, width), BF16), pltpu.VMEM((PEER_TE, width), BF16)]

    return pl.pallas_call(
        _peer_dense_kernel,
        grid=(n // tm, N_PAIRS + 1),
        in_specs=[u_spec, v_spec] + token_specs(tm, lambda t: t) + token_specs(n_s, lambda t: 0),
        out_specs=[pl.BlockSpec((tm, D_MODEL), lambda t, p: (t, 0)),
                   pl.BlockSpec((n_s, D_MODEL), lambda t, p: (0, 0))],
        out_shape=[jax.ShapeDtypeStruct((n, D_MODEL), F32), jax.ShapeDtypeStruct((n_s, D_MODEL), F32)],
        scratch_shapes=buffers(tm) + buffers(n_s),
        compiler_params=_cparams("arbitrary", "arbitrary"),
        name="peer_dense",
    )(u_all, v_all, *prompt, *sample)


def _res_ln_kernel(x_ref, f_ref, g_ref, beta_ref, o_ref):
    o_ref[...] = _layer_norm(DEEPNORM_ALPHA * x_ref[...] + f_ref[...], g_ref[...], beta_ref[...])


def _res_ln(x, f, g, beta, tm):
    n = x.shape[0]
    tile = pl.BlockSpec((tm, D_MODEL), lambda i: (i, 0))
    vec = pl.BlockSpec((1, D_MODEL), lambda i: (0, 0))
    return pl.pallas_call(
        _res_ln_kernel,
        grid=(n // tm,),
        in_specs=[tile, tile, vec, vec],
        out_specs=tile,
        out_shape=jax.ShapeDtypeStruct((n, D_MODEL), F32),
        compiler_params=_cparams("parallel"),
        name="res_ln",
    )(x, f, g.reshape(1, D_MODEL), beta.reshape(1, D_MODEL))


def _peer_layer(xp, xs, wq_bf, keys_bf, u_all, v_all, layer, g, beta):
    n_p, n_s = xp.shape[0], xs.shape[0]
    prompt = _peer_scores(xp, wq_bf, keys_bf, _rows(n_p, SCORE_ROWS))
    sample = _peer_scores(xs, wq_bf, keys_bf, _rows(n_s, SCORE_ROWS))
    fp, fs = _peer_dense(u_all, v_all, layer, prompt, sample, _rows(n_p, DENSE_ROWS))
    return (_res_ln(xp, fp, g, beta, _rows(n_p, RES_LN_ROWS)),
            _res_ln(xs, fs, g, beta, _rows(n_s, RES_LN_ROWS)))


def _rope_tables(pos):
    half = HEAD_DIM // 2
    inv = jnp.power(ROPE_THETA, -jnp.arange(half, dtype=F32) * 2.0 / HEAD_DIM)
    ang = pos.astype(F32)[:, None] * inv[None, :]
    cos = jnp.cos(ang)
    sin = jnp.sin(ang)
    reps = LANES // HEAD_DIM
    return (jnp.tile(jnp.concatenate([cos, cos], axis=1), (1, reps)),
            jnp.tile(jnp.concatenate([-sin, sin], axis=1), (1, reps)))


def _shifted(state, u3):
    t = u3.shape[1]
    up = jnp.concatenate([state, u3], axis=1)
    return up[:, 1:t + 1], up[:, 0:t], up[:, -(CONV_WIDTH - 1):]


def kernel(x_prompt, x_sample, cache_k, cache_v, state_conv, w_qkv, b_qkv, w_o, b_o, attn_sinks,
           w_conv_in, conv_w, w_conv_out, w_peer_q, peer_sub_keys, peer_u, peer_v, ln_g, ln_b):
    batch, seq, _ = x_prompt.shape
    dec_batch, dec_seq, _ = x_sample.shape
    n_p = batch * seq
    n_s = dec_batch * dec_seq
    xp = x_prompt.reshape(n_p, D_MODEL)
    xs = x_sample.reshape(n_s, D_MODEL)
    cos_p, sin_p = _rope_tables(jnp.tile(jnp.arange(seq), batch))
    cos_s, sin_s = _rope_tables(jnp.tile(PAST_LEN + jnp.arange(dec_seq), dec_batch))

    new_kp, new_vp, new_cp, new_ks, new_vs, new_cs = [], [], [], [], [], []
    for layer in range(DEPTH):
        j = layer // 2
        g0, b0 = ln_g[layer, 0], ln_b[layer, 0]
        if layer % 2 == 0:
            wqkv = w_qkv[j].astype(BF16)
            wo = w_o[j].astype(BF16)
            sinks = attn_sinks[j].reshape(N_Q_HEADS)
            qkv_p = _qkv_rope(xp, wqkv, b_qkv[j], cos_p, sin_p, _rows(n_p, QKV_ROWS))
            qkv_s = _qkv_rope(xs, wqkv, b_qkv[j], cos_s, sin_s, _rows(n_s, QKV_ROWS))
            o_p = _attn_prompt(qkv_p, sinks, batch, seq)
            qkv_s3 = qkv_s.reshape(dec_batch, dec_seq, QKV_COLS)
            qkv_s8 = jnp.pad(qkv_s3, ((0, 0), (0, SAMPLE_ROWS - dec_seq), (0, 0)))
            ck = cache_k[j].reshape(dec_batch, WINDOW, KV_COLS)
            cv = cache_v[j].reshape(dec_batch, WINDOW, KV_COLS)
            o_s = _attn_sample(qkv_s8, ck, cv, sinks, dec_seq)[:, :dec_seq].reshape(n_s, Q_COLS)
            kv_p = qkv_p.reshape(batch, seq, QKV_COLS)[:, -WINDOW:, Q_COLS:]
            new_kp.append(kv_p[..., :KV_COLS].reshape(batch, WINDOW, N_KV_HEADS, HEAD_DIM))
            new_vp.append(kv_p[..., KV_COLS:].reshape(batch, WINDOW, N_KV_HEADS, HEAD_DIM))
            k_s = jnp.concatenate([ck, qkv_s3[..., Q_COLS:Q_COLS + KV_COLS]], axis=1)[:, -WINDOW:]
            v_s = jnp.concatenate([cv, qkv_s3[..., Q_COLS + KV_COLS:]], axis=1)[:, -WINDOW:]
            new_ks.append(k_s.reshape(dec_batch, WINDOW, N_KV_HEADS, HEAD_DIM))
            new_vs.append(v_s.reshape(dec_batch, WINDOW, N_KV_HEADS, HEAD_DIM))
            xp = _proj_ln(o_p, wo, b_o[j], xp, g0, b0, _rows(n_p, PROJ_ROWS))
            xs = _proj_ln(o_s, wo, b_o[j], xs, g0, b0, _rows(n_s, PROJ_ROWS))
        else:
            w_in = w_conv_in[j].astype(BF16)
            w_out = w_conv_out[j].astype(BF16)
            gb_p, u_p = _conv_in(xp, w_in, _rows(n_p, CONV_IN_ROWS))
            gb_s, u_s = _conv_in(xs, w_in, _rows(n_s, CONV_IN_ROWS))
            um1_s, um2_s, cs = _shifted(state_conv[j], u_s.reshape(dec_batch, dec_seq, D_MODEL))
            new_cp.append(u_p.reshape(batch, seq, D_MODEL)[:, -(CONV_WIDTH - 1):])
            new_cs.append(cs)
            xp = _conv_out_ln(gb_p, u_p, None, conv_w[j], w_out, xp, g0, b0, _rows(n_p, CONV_OUT_ROWS), seq=seq)
            xs = _conv_out_ln(gb_s, u_s, (um1_s.reshape(n_s, D_MODEL), um2_s.reshape(n_s, D_MODEL)),
                              conv_w[j], w_out, xs, g0, b0, _rows(n_s, CONV_OUT_ROWS))
        wq = w_peer_q[layer].astype(BF16)
        keys = peer_sub_keys[layer].astype(BF16)
        g1, b1 = ln_g[layer, 1], ln_b[layer, 1]
        xp, xs = _peer_layer(xp, xs, wq, keys, peer_u, peer_v, layer, g1, b1)

    return (xp.reshape(batch, seq, D_MODEL), xs.reshape(dec_batch, dec_seq, D_MODEL),
            jnp.stack(new_kp), jnp.stack(new_vp), jnp.stack(new_cp),
            jnp.stack(new_ks), jnp.stack(new_vs), jnp.stack(new_cs))
```

```python
import functools

import jax
import jax.numpy as jnp
from jax import lax
from jax.experimental import pallas as pl
from jax.experimental.pallas import tpu as pltpu

F32 = jnp.float32
BF16 = jnp.bfloat16

D_MODEL = 2048
DEPTH = 4
HEAD_DIM = 64
N_Q_HEADS = 32
N_KV_HEADS = 8
GQA_GROUP = 4
WINDOW = 128
ROPE_THETA = 10000.0
PAST_LEN = 16384
CONV_WIDTH = 3
PEER_HEADS = 8
PEER_KEYS = 128
PEER_EXPERTS = PEER_KEYS * PEER_KEYS
PEER_HALF = 128
PEER_TOPK = 16
LN_EPS = 1e-5
NEG_INF = -1e30
DEEPNORM_ALPHA = (2 * DEPTH) ** 0.25
LOG2E = 1.4426950408889634

Q_COLS = N_Q_HEADS * HEAD_DIM
KV_COLS = N_KV_HEADS * HEAD_DIM
QKV_COLS = Q_COLS + 2 * KV_COLS

LANES = 128
SUBLANES = 8
PEER_TE = 256
I1_PER_TILE = PEER_TE // PEER_KEYS
TILES_PER_STEP = 4
VMEM_LIMIT_BYTES = 60 * 1024 * 1024

QKV_ROWS = 1024
PROJ_ROWS = 512
CONV_IN_ROWS = 1024
CONV_OUT_ROWS = 256
SCORE_ROWS = 256
DENSE_ROWS = 1024
RES_LN_ROWS = 512


def _rows(n, preferred):
    return min(n, preferred)


def _cparams(*semantics):
    return pltpu.CompilerParams(dimension_semantics=semantics, vmem_limit_bytes=VMEM_LIMIT_BYTES)


def _layer_norm(z, g, b):
    mu = jnp.mean(z, axis=-1, keepdims=True)
    zc = z - mu
    var = jnp.mean(zc * zc, axis=-1, keepdims=True)
    return zc * lax.rsqrt(var + LN_EPS) * g + b


QKV_TN = 512
N_ROPE_TILES = (Q_COLS + KV_COLS) // QKV_TN


def _qkv_kernel(x_ref, w_ref, b_ref, cos_ref, sin_ref, o_ref):
    j = pl.program_id(1)
    acc = jnp.dot(x_ref[...].astype(BF16), w_ref[...], preferred_element_type=F32) + b_ref[...]

    @pl.when(j < N_ROPE_TILES)
    def _():
        cos = cos_ref[...]
        sin = sin_ref[...]
        lane = lax.broadcasted_iota(jnp.int32, cos.shape, 1)
        first_half = (lane % HEAD_DIM) < (HEAD_DIM // 2)
        for c in range(QKV_TN // LANES):
            seg = acc[:, c * LANES:(c + 1) * LANES]
            partner = jnp.where(first_half,
                                pltpu.roll(seg, LANES - HEAD_DIM // 2, 1),
                                pltpu.roll(seg, HEAD_DIM // 2, 1))
            o_ref[:, c * LANES:(c + 1) * LANES] = seg * cos + partner * sin

    @pl.when(j >= N_ROPE_TILES)
    def _():
        o_ref[...] = acc


def _qkv_rope(x, w_bf, b, cos, sin, tm):
    n = x.shape[0]
    return pl.pallas_call(
        _qkv_kernel,
        grid=(n // tm, QKV_COLS // QKV_TN),
        in_specs=[
            pl.BlockSpec((tm, D_MODEL), lambda i, j: (i, 0)),
            pl.BlockSpec((D_MODEL, QKV_TN), lambda i, j: (0, j)),
            pl.BlockSpec((1, QKV_TN), lambda i, j: (0, j)),
            pl.BlockSpec((tm, LANES), lambda i, j: (i, 0)),
            pl.BlockSpec((tm, LANES), lambda i, j: (i, 0)),
        ],
        out_specs=pl.BlockSpec((tm, QKV_TN), lambda i, j: (i, j)),
        out_shape=jax.ShapeDtypeStruct((n, QKV_COLS), F32),
        compiler_params=_cparams("parallel", "arbitrary"),
        name="qkv_rope",
    )(x, w_bf, b.reshape(1, QKV_COLS), cos, sin)


def _attn_prompt_kernel(sink_ref, q_ref, kp_ref, kc_ref, vp_ref, vc_ref, o_ref):
    i = pl.program_id(1)
    cols = GQA_GROUP * WINDOW
    kj = lax.broadcasted_iota(jnp.int32, (2 * WINDOW, cols), 0)
    qi = lax.broadcasted_iota(jnp.int32, (2 * WINDOW, cols), 1) % WINDOW
    dist = qi + WINDOW - kj
    prev_ok = jnp.where(i > 0, 0, WINDOW)
    mask = (dist >= 0) & (dist <= WINDOW) & (kj >= prev_ok)
    nt = (((1,), (1,)), ((), ()))
    tn = (((0,), (0,)), ((), ()))
    for g in range(N_KV_HEADS):
        ks = slice(g * HEAD_DIM, (g + 1) * HEAD_DIM)
        k = jnp.concatenate([kp_ref[:, ks], kc_ref[:, ks]], axis=0).astype(BF16)
        v = jnp.concatenate([vp_ref[:, ks], vc_ref[:, ks]], axis=0).astype(BF16)
        q = jnp.concatenate(
            [q_ref[:, (g * GQA_GROUP + j) * HEAD_DIM:(g * GQA_GROUP + j + 1) * HEAD_DIM]
             for j in range(GQA_GROUP)], axis=0).astype(BF16)
        t = lax.dot_general(k, q, nt, preferred_element_type=F32)
        t = jnp.where(mask, t * (HEAD_DIM ** -0.5 * LOG2E), NEG_INF)
        sink = jnp.concatenate(
            [jnp.full((1, WINDOW), sink_ref[g * GQA_GROUP + j] * LOG2E, F32) for j in range(GQA_GROUP)], axis=1)
        m = jnp.maximum(jnp.max(t, axis=0, keepdims=True), sink)
        p = jnp.exp2(t - m)
        denom = jnp.sum(p, axis=0, keepdims=True) + jnp.exp2(sink - m)
        o_t = lax.dot_general(v, p.astype(BF16), tn, preferred_element_type=F32) / denom
        o = o_t.T
        o_ref[:, g * GQA_GROUP * HEAD_DIM:(g + 1) * GQA_GROUP * HEAD_DIM] = jnp.concatenate(
            [o[j * WINDOW:(j + 1) * WINDOW] for j in range(GQA_GROUP)], axis=1).astype(o_ref.dtype)


def _attn_prompt(qkv, sinks, batch, seq):
    nb = seq // WINDOW
    kcol = Q_COLS // KV_COLS
    vcol = kcol + 1

    def cur(b, i):
        return b * nb + i

    def prev(b, i):
        return b * nb + jnp.maximum(i - 1, 0)

    return pl.pallas_call(
        _attn_prompt_kernel,
        grid=(batch, nb),
        in_specs=[
            pl.BlockSpec(memory_space=pltpu.SMEM),
            pl.BlockSpec((WINDOW, Q_COLS), lambda b, i: (cur(b, i), 0)),
            pl.BlockSpec((WINDOW, KV_COLS), lambda b, i: (prev(b, i), kcol)),
            pl.BlockSpec((WINDOW, KV_COLS), lambda b, i: (cur(b, i), kcol)),
            pl.BlockSpec((WINDOW, KV_COLS), lambda b, i: (prev(b, i), vcol)),
            pl.BlockSpec((WINDOW, KV_COLS), lambda b, i: (cur(b, i), vcol)),
        ],
        out_specs=pl.BlockSpec((WINDOW, Q_COLS), lambda b, i: (cur(b, i), 0)),
        out_shape=jax.ShapeDtypeStruct((batch * seq, Q_COLS), BF16),
        compiler_params=_cparams("parallel", "arbitrary"),
        name="attn_prompt",
    )(sinks, qkv, qkv, qkv, qkv, qkv)


SAMPLE_ROWS = SUBLANES
SAMPLE_BATCH_BLOCK = 8


def _attn_sample_kernel(sink_ref, qkv_ref, ck_ref, cv_ref, o_ref, *, n_new):
    n_keys = WINDOW + SAMPLE_ROWS
    kk = lax.broadcasted_iota(jnp.int32, (n_keys, LANES), 0)
    lane = lax.broadcasted_iota(jnp.int32, (n_keys, LANES), 1)
    qi = lane % SAMPLE_ROWS
    new = kk - WINDOW
    mask = ((kk < WINDOW) & (kk >= qi)) | ((new >= 0) & (new <= qi) & (new < n_new))
    head = lax.broadcasted_iota(jnp.int32, (1, LANES), 1) // SAMPLE_ROWS
    nt = (((1,), (1,)), ((), ()))
    tn = (((0,), (0,)), ((), ()))
    q_pad = jnp.zeros((LANES - GQA_GROUP * SAMPLE_ROWS, HEAD_DIM), BF16)
    for b in range(qkv_ref.shape[0]):
        qkv = qkv_ref[b]
        for g in range(N_KV_HEADS):
            ks = slice(g * HEAD_DIM, (g + 1) * HEAD_DIM)
            k = jnp.concatenate([ck_ref[b, :, ks], qkv[:, Q_COLS + g * HEAD_DIM:Q_COLS + (g + 1) * HEAD_DIM]],
                                axis=0).astype(BF16)
            v = jnp.concatenate([cv_ref[b, :, ks],
                                 qkv[:, Q_COLS + KV_COLS + g * HEAD_DIM:Q_COLS + KV_COLS + (g + 1) * HEAD_DIM]],
                                axis=0).astype(BF16)
            q = jnp.concatenate(
                [qkv[:, (g * GQA_GROUP + j) * HEAD_DIM:(g * GQA_GROUP + j + 1) * HEAD_DIM].astype(BF16)
                 for j in range(GQA_GROUP)] + [q_pad], axis=0)
            t = lax.dot_general(k, q, nt, preferred_element_type=F32)
            t = jnp.where(mask, t * (HEAD_DIM ** -0.5 * LOG2E), NEG_INF)
            sink = jnp.zeros((1, LANES), F32)
            for j in range(GQA_GROUP):
                sink = jnp.where(head == j, sink_ref[g * GQA_GROUP + j] * LOG2E, sink)
            m = jnp.maximum(jnp.max(t, axis=0, keepdims=True), sink)
            p = jnp.exp2(t - m)
            denom = jnp.sum(p, axis=0, keepdims=True) + jnp.exp2(sink - m)
            o_t = lax.dot_general(v, p.astype(BF16), tn, preferred_element_type=F32) / denom
            o = o_t.T
            o_ref[b, :, g * GQA_GROUP * HEAD_DIM:(g + 1) * GQA_GROUP * HEAD_DIM] = jnp.concatenate(
                [o[j * SAMPLE_ROWS:(j + 1) * SAMPLE_ROWS] for j in range(GQA_GROUP)], axis=1)


def _attn_sample(qkv8, ck, cv, sinks, n_new):
    nb = qkv8.shape[0]
    bb = SAMPLE_BATCH_BLOCK
    assert nb % bb == 0 and n_new <= SAMPLE_ROWS
    return pl.pallas_call(
        functools.partial(_attn_sample_kernel, n_new=n_new),
        grid=(nb // bb,),
        in_specs=[
            pl.BlockSpec(memory_space=pltpu.SMEM),
            pl.BlockSpec((bb, SAMPLE_ROWS, QKV_COLS), lambda b: (b, 0, 0)),
            pl.BlockSpec((bb, WINDOW, KV_COLS), lambda b: (b, 0, 0)),
            pl.BlockSpec((bb, WINDOW, KV_COLS), lambda b: (b, 0, 0)),
        ],
        out_specs=pl.BlockSpec((bb, SAMPLE_ROWS, Q_COLS), lambda b: (b, 0, 0)),
        out_shape=jax.ShapeDtypeStruct((nb, SAMPLE_ROWS, Q_COLS), F32),
        compiler_params=_cparams("parallel"),
        name="attn_sample",
    )(sinks, qkv8, ck, cv)


def _proj_ln_kernel(a_ref, w_ref, b_ref, x_ref, g_ref, beta_ref, o_ref):
    y = jnp.dot(a_ref[...].astype(BF16), w_ref[...], preferred_element_type=F32) + b_ref[...]
    o_ref[...] = _layer_norm(DEEPNORM_ALPHA * x_ref[...] + y, g_ref[...], beta_ref[...])


def _proj_ln(a, w_bf, bias, x, g, beta, tm):
    n, k = a.shape
    row = lambda i: (i, 0)
    fixed = lambda i: (0, 0)
    return pl.pallas_call(
        _proj_ln_kernel,
        grid=(n // tm,),
        in_specs=[
            pl.BlockSpec((tm, k), row),
            pl.BlockSpec((k, D_MODEL), fixed),
            pl.BlockSpec((1, D_MODEL), fixed),
            pl.BlockSpec((tm, D_MODEL), row),
            pl.BlockSpec((1, D_MODEL), fixed),
            pl.BlockSpec((1, D_MODEL), fixed),
        ],
        out_specs=pl.BlockSpec((tm, D_MODEL), row),
        out_shape=jax.ShapeDtypeStruct((n, D_MODEL), F32),
        compiler_params=_cparams("parallel"),
        name="proj_ln",
    )(a, w_bf, bias.reshape(1, D_MODEL), x, g.reshape(1, D_MODEL), beta.reshape(1, D_MODEL))


CONV_TN = 512


def _conv_in_kernel(x_ref, wb_ref, wc_ref, wh_ref, gb_ref, u_ref):
    xb = x_ref[...].astype(BF16)
    gb_ref[...] = jnp.dot(xb, wb_ref[...], preferred_element_type=F32)
    gate_c = jnp.dot(xb, wc_ref[...], preferred_element_type=F32)
    h = jnp.dot(xb, wh_ref[...], preferred_element_type=F32)
    u_ref[...] = gate_c * h


def _conv_in(x, w_in_bf, tm):
    n = x.shape[0]
    nt = D_MODEL // CONV_TN
    return pl.pallas_call(
        _conv_in_kernel,
        grid=(n // tm, nt),
        in_specs=[
            pl.BlockSpec((tm, D_MODEL), lambda i, j: (i, 0)),
            pl.BlockSpec((D_MODEL, CONV_TN), lambda i, j: (0, j)),
            pl.BlockSpec((D_MODEL, CONV_TN), lambda i, j: (0, nt + j)),
            pl.BlockSpec((D_MODEL, CONV_TN), lambda i, j: (0, 2 * nt + j)),
        ],
        out_specs=[
            pl.BlockSpec((tm, CONV_TN), lambda i, j: (i, j)),
            pl.BlockSpec((tm, CONV_TN), lambda i, j: (i, j)),
        ],
        out_shape=[jax.ShapeDtypeStruct((n, D_MODEL), F32)] * 2,
        compiler_params=_cparams("parallel", "arbitrary"),
        name="conv_in",
    )(x, w_in_bf, w_in_bf, w_in_bf)


def _gated_out_ln(gate_b, conv, w_ref, x_ref, g_ref, beta_ref, o_ref):
    y = jnp.dot((gate_b * conv).astype(BF16), w_ref[...], preferred_element_type=F32)
    o_ref[...] = _layer_norm(DEEPNORM_ALPHA * x_ref[...] + y, g_ref[...], beta_ref[...])


def _conv_out_ln_kernel(gb_ref, u_ref, um1_ref, um2_ref, cw_ref, w_ref, x_ref, g_ref, beta_ref, o_ref):
    conv = um2_ref[...] * cw_ref[0:1, :]
    conv = conv + um1_ref[...] * cw_ref[1:2, :]
    conv = conv + u_ref[...] * cw_ref[2:3, :]
    _gated_out_ln(gb_ref[...], conv, w_ref, x_ref, g_ref, beta_ref, o_ref)


def _conv_out_ln_halo_kernel(gb_ref, u_ref, halo_ref, cw_ref, w_ref, x_ref, g_ref, beta_ref, o_ref,
                             *, tiles_per_seq):
    i = pl.program_id(0)
    u = u_ref[...]
    halo = jnp.where(i % tiles_per_seq == 0, 0.0, halo_ref[...])
    row = lax.broadcasted_iota(jnp.int32, u.shape, 0)
    last, before_last = halo[SUBLANES - 1:SUBLANES, :], halo[SUBLANES - 2:SUBLANES - 1, :]
    um1 = jnp.where(row == 0, last, pltpu.roll(u, 1, 0))
    um2 = jnp.where(row == 0, before_last, jnp.where(row == 1, last, pltpu.roll(u, 2, 0)))
    conv = um2 * cw_ref[0:1, :] + um1 * cw_ref[1:2, :] + u * cw_ref[2:3, :]
    _gated_out_ln(gb_ref[...], conv, w_ref, x_ref, g_ref, beta_ref, o_ref)


def _conv_out_ln(gate_b, u, shifted, conv_w, w_out_bf, x, g, beta, tm, seq=None):
    n = x.shape[0]
    row = lambda i: (i, 0)
    fixed = lambda i: (0, 0)
    tile = pl.BlockSpec((tm, D_MODEL), row)
    vec = pl.BlockSpec((1, D_MODEL), fixed)
    tail_specs = [pl.BlockSpec((CONV_WIDTH, D_MODEL), fixed), pl.BlockSpec((D_MODEL, D_MODEL), fixed),
                  tile, vec, vec]
    tail_args = (conv_w, w_out_bf, x, g.reshape(1, D_MODEL), beta.reshape(1, D_MODEL))
    if shifted is None:
        assert seq % tm == 0 and tm % SUBLANES == 0
        body = functools.partial(_conv_out_ln_halo_kernel, tiles_per_seq=seq // tm)
        halo = pl.BlockSpec((SUBLANES, D_MODEL), lambda i: (jnp.maximum(i * (tm // SUBLANES) - 1, 0), 0))
        in_specs, args = [tile, tile, halo], (gate_b, u, u)
    else:
        body = _conv_out_ln_kernel
        in_specs, args = [tile, tile, tile, tile], (gate_b, u) + tuple(shifted)
    return pl.pallas_call(
        body,
        grid=(n // tm,),
        in_specs=in_specs + tail_specs,
        out_specs=tile,
        out_shape=jax.ShapeDtypeStruct((n, D_MODEL), F32),
        compiler_params=_cparams("parallel"),
        name="conv_out_ln",
    )(*args, *tail_args)


S1L_ROWS = PEER_KEYS + TILES_PER_STEP * I1_PER_TILE


def _oddeven_merge_sort_pairs(n):
    pairs = []
    p = 1
    while p < n:
        k = p
        while k >= 1:
            for j in range(k % p, n - k, 2 * k):
                for i in range(min(k, n - j - k)):
                    if (i + j) // (2 * p) == (i + j + k) // (2 * p):
                        pairs.append((i + j, i + j + k))
            k //= 2
        p *= 2
    return pairs


def _top_values(pieces, k):
    n = len(pieces)
    size = 1
    while size < n:
        size *= 2
    v = list(pieces)
    for i, j in _oddeven_merge_sort_pairs(size):
        if j < n:
            v[i], v[j] = jnp.maximum(v[i], v[j]), jnp.minimum(v[i], v[j])
    vals = []
    for it in range(k):
        m = jnp.max(v[0], axis=0, keepdims=True)
        vals.append(m)
        need = k - it - 1
        if need:
            hit = v[0] == m
            nxt = v[1:] + [jnp.full_like(v[0], -jnp.inf)]
            v = [jnp.where(hit, nxt[j], v[j]) for j in range(min(need, len(v)))]
    return vals


def _peer_score_kernel(x_ref, wq_ref, keys_ref, xt_ref, s1l_ref, s2l_ref, tau_ref):
    s1l_ref[:I1_PER_TILE] = jnp.full((I1_PER_TILE,) + s1l_ref.shape[1:], NEG_INF, F32)
    s1l_ref[I1_PER_TILE + PEER_KEYS:] = jnp.full((S1L_ROWS - I1_PER_TILE - PEER_KEYS,) + s1l_ref.shape[1:],
                                                 NEG_INF, F32)
    x = x_ref[...]
    xt_ref[...] = x.T.astype(BF16)
    q = jnp.dot(x.astype(BF16), wq_ref[...], preferred_element_type=F32).astype(BF16)
    nt = (((1,), (1,)), ((), ()))
    k = PEER_TOPK
    assert k == 16 and SUBLANES == 8
    for h in range(PEER_HEADS):
        c0 = h * 2 * PEER_HALF
        s1 = lax.dot_general(keys_ref[h, 0], q[:, c0:c0 + PEER_HALF], nt, preferred_element_type=F32)
        s2 = lax.dot_general(keys_ref[h, 1], q[:, c0 + PEER_HALF:c0 + 2 * PEER_HALF], nt,
                             preferred_element_type=F32)
        a = _top_values([s1[r:r + SUBLANES] for r in range(0, PEER_KEYS, SUBLANES)], k + 1)
        b = _top_values([s2[r:r + SUBLANES] for r in range(0, PEER_KEYS, SUBLANES)], k + 1)
        neg = jnp.full_like(a[0], -jnp.inf)
        b24 = jnp.concatenate(b + [neg] * 7, axis=0)
        b8 = jnp.concatenate(b[:8], axis=0)
        a16 = jnp.concatenate(a[8:] + [neg] * 7, axis=0)
        row = lax.broadcasted_iota(jnp.int32, b8.shape, 0)
        eb24 = jnp.exp(b24 - b[0])
        eb8 = jnp.exp(b8 - b[0])
        cand = [a[0] + b24[r:r + 8] for r in (0, 8, 16)]
        wgt = [eb24[r:r + 8] for r in (0, 8, 16)]
        for i in range(1, 8):
            cand.append(jnp.where(row < (k + 1) // (i + 1), a[i] + b8, -jnp.inf))
            wgt.append(jnp.exp(a[i] - a[0]) * eb8)
        cand += [a16[r:r + 8] + b[0] for r in (0, 8)]
        wgt += [jnp.exp(a16[r:r + 8] - a[0]) for r in (0, 8)]
        tops = _top_values(cand, k + 1)
        thr = 0.5 * (tops[k - 1] + tops[k])
        z8 = functools.reduce(jnp.add, [jnp.where(c >= thr, w, 0.0) for c, w in zip(cand, wgt)])
        z = jnp.sum(z8, axis=0, keepdims=True)
        shift = a[0] + b[0] + jnp.log(z)
        s1l_ref[I1_PER_TILE:I1_PER_TILE + PEER_KEYS, h, :] = (s1 - shift) * LOG2E - 1.0
        s2l_ref[h] = s2 * LOG2E
        tau_ref[h:h + 1, :] = (thr - shift) * LOG2E - 1.0


def _peer_scores(x, wq_bf, keys_bf, tm):
    n = x.shape[0]
    hk = (PEER_HEADS, PEER_KEYS, tm)
    hk_map = lambda i: (0, 0, i)
    hk_shape = jax.ShapeDtypeStruct((PEER_HEADS, PEER_KEYS, n), F32)
    return pl.pallas_call(
        _peer_score_kernel,
        grid=(n // tm,),
        in_specs=[
            pl.BlockSpec((tm, D_MODEL), lambda i: (i, 0)),
            pl.BlockSpec((D_MODEL, PEER_HEADS * 2 * PEER_HALF), lambda i: (0, 0)),
            pl.BlockSpec((PEER_HEADS, 2, PEER_KEYS, PEER_HALF), lambda i: (0, 0, 0, 0)),
        ],
        out_specs=[
            pl.BlockSpec((D_MODEL, tm), lambda i: (0, i)),
            pl.BlockSpec((S1L_ROWS, PEER_HEADS, tm), hk_map), pl.BlockSpec(hk, hk_map),
            pl.BlockSpec((PEER_HEADS, tm), lambda i: (0, i)),
        ],
        out_shape=[
            jax.ShapeDtypeStruct((D_MODEL, n), BF16),
            jax.ShapeDtypeStruct((S1L_ROWS, PEER_HEADS, n), F32), hk_shape,
            jax.ShapeDtypeStruct((PEER_HEADS, n), F32),
        ],
        compiler_params=_cparams("parallel"),
        name="peer_scores",
    )(x, wq_bf, keys_bf)


def _gelu_times_two(x):
    return x * (1.0 + lax.erf(x * (2.0 ** -0.5)))


N_TILES = PEER_EXPERTS // PEER_TE
N_STEPS = N_TILES // TILES_PER_STEP + 1


def _dense_pipeline_step(p, u_ref, v_refs, xt_ref, s1l_ref, s2l_ref, tau_ref, o_ref, hid, at):
    tm = xt_ref.shape[1]

    @pl.when(p == 0)
    def _():
        o_ref[...] = jnp.zeros_like(o_ref)
        for buf in hid + at:
            buf[...] = jnp.zeros_like(buf)

    for q in range(TILES_PER_STEP):
        rows = slice(q * PEER_TE, (q + 1) * PEER_TE)
        hid[q % 2][...] = jnp.dot(u_ref[rows, :].astype(BF16), xt_ref[...], preferred_element_type=F32)

        src, dst = hid[(q - 1) % 2], at[(q - 1) % 2]
        for il in range(I1_PER_TILE):
            r = q * I1_PER_TILE + il
            rs = slice(il * PEER_KEYS, (il + 1) * PEER_KEYS)
            for c in range(tm // LANES):
                cs = slice(c * LANES, (c + 1) * LANES)
                gate = jnp.zeros((PEER_KEYS, LANES), F32)
                for h in range(PEER_HEADS):
                    pair = s2l_ref[h, :, cs] + s1l_ref[r, h:h + 1, cs]
                    gate = gate + jnp.where(pair >= tau_ref[h:h + 1, cs], jnp.exp2(pair), 0.0)
                dst[rs, cs] = (_gelu_times_two(src[rs, cs]) * gate).astype(BF16)

        v_ref = v_refs[0] if q < 2 else v_refs[1]
        v_rows = slice((q % 2) * PEER_TE, (q % 2 + 1) * PEER_TE)
        o_ref[...] += lax.dot_general(at[q % 2][...], v_ref[v_rows, :].astype(BF16), (((0,), (0,)), ((), ())),
                                      preferred_element_type=F32)


def _peer_dense_kernel(u_ref, va_ref, vb_ref, xt_ref, s1l_ref, s2l_ref, tau_ref,
                       xts_ref, s1ls_ref, s2ls_ref, taus_ref,
                       o_ref, os_ref, hid0, hid1, at0, at1, hids0, hids1, ats0, ats1):
    t = pl.program_id(0)
    p = pl.program_id(1)
    _dense_pipeline_step(p, u_ref, (va_ref, vb_ref), xt_ref, s1l_ref, s2l_ref, tau_ref, o_ref,
                         (hid0, hid1), (at0, at1))

    @pl.when(t == 0)
    def _():
        _dense_pipeline_step(p, u_ref, (va_ref, vb_ref), xts_ref, s1ls_ref, s2ls_ref, taus_ref, os_ref,
                             (hids0, hids1), (ats0, ats1))


def _peer_dense(u_all, v_all, layer, prompt, sample, tm):
    n = prompt[0].shape[1]
    n_s = sample[0].shape[1]
    assert TILES_PER_STEP == 4
    half_blocks = 2 * N_TILES // TILES_PER_STEP
    u_spec = pl.BlockSpec((None, TILES_PER_STEP * PEER_TE, D_MODEL),
                          lambda t, p: (layer, jnp.minimum(p, N_STEPS - 2), 0))
    va_spec = pl.BlockSpec((None, 2 * PEER_TE, D_MODEL), lambda t, p: (layer, jnp.maximum(2 * p - 1, 0), 0))
    vb_spec = pl.BlockSpec((None, 2 * PEER_TE, D_MODEL),
                           lambda t, p: (layer, jnp.minimum(2 * p, half_blocks - 1), 0))

    def token_specs(width, tile_of):
        return [
            pl.BlockSpec((D_MODEL, width), lambda t, p: (0, tile_of(t)), pipeline_mode=pl.Buffered(1)),
            pl.BlockSpec((TILES_PER_STEP * I1_PER_TILE, PEER_HEADS, width), lambda t, p: (p, 0, tile_of(t))),
            pl.BlockSpec((PEER_HEADS, PEER_KEYS, width), lambda t, p: (0, 0, tile_of(t)),
                         pipeline_mode=pl.Buffered(1)),
            pl.BlockSpec((PEER_HEADS, width), lambda t, p: (0, tile_of(t))),
        ]

    def buffers(width):
        return [pltpu.VMEM((PEER_TE, width), F32), pltpu.VMEM((PEER_TE, width), F32),
                pltpu.VMEM((PEER_TE, width), BF16), pltpu.VMEM((PEER_TE, width), BF16)]

    return pl.pallas_call(
        _peer_dense_kernel,
        grid=(n // tm, N_STEPS),
        in_specs=[u_spec, va_spec, vb_spec] + token_specs(tm, lambda t: t) + token_specs(n_s, lambda t: 0),
        out_specs=[pl.BlockSpec((tm, D_MODEL), lambda t, p: (t, 0), pipeline_mode=pl.Buffered(1)),
                   pl.BlockSpec((n_s, D_MODEL), lambda t, p: (0, 0))],
        out_shape=[jax.ShapeDtypeStruct((n, D_MODEL), F32), jax.ShapeDtypeStruct((n_s, D_MODEL), F32)],
        scratch_shapes=buffers(tm) + buffers(n_s),
        compiler_params=_cparams("arbitrary", "arbitrary"),
        name="peer_dense",
    )(u_all, v_all, v_all, *prompt, *sample)


def _res_ln_kernel(x_ref, f_ref, g_ref, beta_ref, o_ref):
    o_ref[...] = _layer_norm(DEEPNORM_ALPHA * x_ref[...] + f_ref[...], g_ref[...], beta_ref[...])


def _res_ln(x, f, g, beta, tm):
    n = x.shape[0]
    tile = pl.BlockSpec((tm, D_MODEL), lambda i: (i, 0))
    vec = pl.BlockSpec((1, D_MODEL), lambda i: (0, 0))
    return pl.pallas_call(
        _res_ln_kernel,
        grid=(n // tm,),
        in_specs=[tile, tile, vec, vec],
        out_specs=tile,
        out_shape=jax.ShapeDtypeStruct((n, D_MODEL), F32),
        compiler_params=_cparams("parallel"),
        name="res_ln",
    )(x, f, g.reshape(1, D_MODEL), beta.reshape(1, D_MODEL))


def _peer_layer(xp, xs, wq_bf, keys_bf, u_all, v_all, layer, g, beta):
    n_p, n_s = xp.shape[0], xs.shape[0]
    prompt = _peer_scores(xp, wq_bf, keys_bf, _rows(n_p, SCORE_ROWS))
    sample = _peer_scores(xs, wq_bf, keys_bf, _rows(n_s, SCORE_ROWS))
    fp, fs = _peer_dense(u_all, v_all, layer, prompt, sample, _rows(n_p, DENSE_ROWS))
    return (_res_ln(xp, fp, g, beta, _rows(n_p, RES_LN_ROWS)),
            _res_ln(xs, fs, g, beta, _rows(n_s, RES_LN_ROWS)))


def _rope_tables(pos):
    half = HEAD_DIM // 2
    inv = jnp.power(ROPE_THETA, -jnp.arange(half, dtype=F32) * 2.0 / HEAD_DIM)
    ang = pos.astype(F32)[:, None] * inv[None, :]
    cos = jnp.cos(ang)
    sin = jnp.sin(ang)
    reps = LANES // HEAD_DIM
    return (jnp.tile(jnp.concatenate([cos, cos], axis=1), (1, reps)),
            jnp.tile(jnp.concatenate([-sin, sin], axis=1), (1, reps)))


def _shifted(state, u3):
    t = u3.shape[1]
    up = jnp.concatenate([state, u3], axis=1)
    return up[:, 1:t + 1], up[:, 0:t], up[:, -(CONV_WIDTH - 1):]


def kernel(x_prompt, x_sample, cache_k, cache_v, state_conv, w_qkv, b_qkv, w_o, b_o, attn_sinks,
           w_conv_in, conv_w, w_conv_out, w_peer_q, peer_sub_keys, peer_u, peer_v, ln_g, ln_b):
    batch, seq, _ = x_prompt.shape
    dec_batch, dec_seq, _ = x_sample.shape
    n_p = batch * seq
    n_s = dec_batch * dec_seq
    xp = x_prompt.reshape(n_p, D_MODEL)
    xs = x_sample.reshape(n_s, D_MODEL)
    cos_p, sin_p = _rope_tables(jnp.tile(jnp.arange(seq), batch))
    cos_s, sin_s = _rope_tables(jnp.tile(PAST_LEN + jnp.arange(dec_seq), dec_batch))

    new_kp, new_vp, new_cp, new_ks, new_vs, new_cs = [], [], [], [], [], []
    for layer in range(DEPTH):
        j = layer // 2
        g0, b0 = ln_g[layer, 0], ln_b[layer, 0]
        if layer % 2 == 0:
            wqkv = w_qkv[j].astype(BF16)
            wo = w_o[j].astype(BF16)
            sinks = attn_sinks[j].reshape(N_Q_HEADS)
            qkv_p = _qkv_rope(xp, wqkv, b_qkv[j], cos_p, sin_p, _rows(n_p, QKV_ROWS))
            qkv_s = _qkv_rope(xs, wqkv, b_qkv[j], cos_s, sin_s, _rows(n_s, QKV_ROWS))
            o_p = _attn_prompt(qkv_p, sinks, batch, seq)
            qkv_s3 = qkv_s.reshape(dec_batch, dec_seq, QKV_COLS)
            qkv_s8 = jnp.pad(qkv_s3, ((0, 0), (0, SAMPLE_ROWS - dec_seq), (0, 0)))
            ck = cache_k[j].reshape(dec_batch, WINDOW, KV_COLS)
            cv = cache_v[j].reshape(dec_batch, WINDOW, KV_COLS)
            o_s = _attn_sample(qkv_s8, ck, cv, sinks, dec_seq)[:, :dec_seq].reshape(n_s, Q_COLS)
            kv_p = qkv_p.reshape(batch, seq, QKV_COLS)[:, -WINDOW:, Q_COLS:]
            new_kp.append(kv_p[..., :KV_COLS].reshape(batch, WINDOW, N_KV_HEADS, HEAD_DIM))
            new_vp.append(kv_p[..., KV_COLS:].reshape(batch, WINDOW, N_KV_HEADS, HEAD_DIM))
            k_s = jnp.concatenate([ck, qkv_s3[..., Q_COLS:Q_COLS + KV_COLS]], axis=1)[:, -WINDOW:]
            v_s = jnp.concatenate([cv, qkv_s3[..., Q_COLS + KV_COLS:]], axis=1)[:, -WINDOW:]
            new_ks.append(k_s.reshape(dec_batch, WINDOW, N_KV_HEADS, HEAD_DIM))
            new_vs.append(v_s.reshape(dec_batch, WINDOW, N_KV_HEADS, HEAD_DIM))
            xp = _proj_ln(o_p, wo, b_o[j], xp, g0, b0, _rows(n_p, PROJ_ROWS))
            xs = _proj_ln(o_s, wo, b_o[j], xs, g0, b0, _rows(n_s, PROJ_ROWS))
        else:
            w_in = w_conv_in[j].astype(BF16)
            w_out = w_conv_out[j].astype(BF16)
            gb_p, u_p = _conv_in(xp, w_in, _rows(n_p, CONV_IN_ROWS))
            gb_s, u_s = _conv_in(xs, w_in, _rows(n_s, CONV_IN_ROWS))
            um1_s, um2_s, cs = _shifted(state_conv[j], u_s.reshape(dec_batch, dec_seq, D_MODEL))
            new_cp.append(u_p.reshape(batch, seq, D_MODEL)[:, -(CONV_WIDTH - 1):])
            new_cs.append(cs)
            xp = _conv_out_ln(gb_p, u_p, None, conv_w[j], w_out, xp, g0, b0, _rows(n_p, CONV_OUT_ROWS), seq=seq)
            xs = _conv_out_ln(gb_s, u_s, (um1_s.reshape(n_s, D_MODEL), um2_s.reshape(n_s, D_MODEL)),
                              conv_w[j], w_out, xs, g0, b0, _rows(n_s, CONV_OUT_ROWS))
        wq = w_peer_q[layer].astype(BF16)
        keys = peer_sub_keys[layer].astype(BF16)
        g1, b1 = ln_g[layer, 1], ln_b[layer, 1]
        xp, xs = _peer_layer(xp, xs, wq, keys, peer_u, peer_v, layer, g1, b1)

    return (xp.reshape(batch, seq, D_MODEL), xs.reshape(dec_batch, dec_seq, D_MODEL),
            jnp.stack(new_kp), jnp.stack(new_vp), jnp.stack(new_cp),
            jnp.stack(new_ks), jnp.stack(new_vs), jnp.stack(new_cs))
```

```python
import functools

import jax
import jax.numpy as jnp
from jax import lax
from jax.experimental import pallas as pl
from jax.experimental.pallas import tpu as pltpu

F32 = jnp.float32
BF16 = jnp.bfloat16

D_MODEL = 2048
DEPTH = 4
HEAD_DIM = 64
N_Q_HEADS = 32
N_KV_HEADS = 8
GQA_GROUP = 4
WINDOW = 128
ROPE_THETA = 10000.0
PAST_LEN = 16384
CONV_WIDTH = 3
PEER_HEADS = 8
PEER_KEYS = 128
PEER_EXPERTS = PEER_KEYS * PEER_KEYS
PEER_HALF = 128
PEER_TOPK = 16
LN_EPS = 1e-5
NEG_INF = -1e30
DEEPNORM_ALPHA = (2 * DEPTH) ** 0.25
LOG2E = 1.4426950408889634

Q_COLS = N_Q_HEADS * HEAD_DIM
KV_COLS = N_KV_HEADS * HEAD_DIM
QKV_COLS = Q_COLS + 2 * KV_COLS

LANES = 128
SUBLANES = 8
PEER_TE = 256
I1_PER_TILE = PEER_TE // PEER_KEYS
VMEM_LIMIT_BYTES = 56 * 1024 * 1024

QKV_ROWS = 1024
PROJ_ROWS = 512
CONV_IN_ROWS = 1024
CONV_OUT_ROWS = 256
SCORE_ROWS = 256
DENSE_ROWS = 1024
RES_LN_ROWS = 512


def _rows(n, preferred):
    return min(n, preferred)


def _cparams(*semantics):
    return pltpu.CompilerParams(dimension_semantics=semantics, vmem_limit_bytes=VMEM_LIMIT_BYTES)


def _layer_norm(z, g, b):
    mu = jnp.mean(z, axis=-1, keepdims=True)
    zc = z - mu
    var = jnp.mean(zc * zc, axis=-1, keepdims=True)
    return zc * lax.rsqrt(var + LN_EPS) * g + b


QKV_TN = 512
N_ROPE_TILES = (Q_COLS + KV_COLS) // QKV_TN


def _qkv_kernel(x_ref, w_ref, b_ref, cos_ref, sin_ref, o_ref):
    j = pl.program_id(1)
    acc = jnp.dot(x_ref[...].astype(BF16), w_ref[...], preferred_element_type=F32) + b_ref[...]

    @pl.when(j < N_ROPE_TILES)
    def _():
        cos = cos_ref[...]
        sin = sin_ref[...]
        lane = lax.broadcasted_iota(jnp.int32, cos.shape, 1)
        first_half = (lane % HEAD_DIM) < (HEAD_DIM // 2)
        for c in range(QKV_TN // LANES):
            seg = acc[:, c * LANES:(c + 1) * LANES]
            partner = jnp.where(first_half,
                                pltpu.roll(seg, LANES - HEAD_DIM // 2, 1),
                                pltpu.roll(seg, HEAD_DIM // 2, 1))
            o_ref[:, c * LANES:(c + 1) * LANES] = seg * cos + partner * sin

    @pl.when(j >= N_ROPE_TILES)
    def _():
        o_ref[...] = acc


def _qkv_rope(x, w_bf, b, cos, sin, tm):
    n = x.shape[0]
    return pl.pallas_call(
        _qkv_kernel,
        grid=(n // tm, QKV_COLS // QKV_TN),
        in_specs=[
            pl.BlockSpec((tm, D_MODEL), lambda i, j: (i, 0)),
            pl.BlockSpec((D_MODEL, QKV_TN), lambda i, j: (0, j)),
            pl.BlockSpec((1, QKV_TN), lambda i, j: (0, j)),
            pl.BlockSpec((tm, LANES), lambda i, j: (i, 0)),
            pl.BlockSpec((tm, LANES), lambda i, j: (i, 0)),
        ],
        out_specs=pl.BlockSpec((tm, QKV_TN), lambda i, j: (i, j)),
        out_shape=jax.ShapeDtypeStruct((n, QKV_COLS), F32),
        compiler_params=_cparams("parallel", "arbitrary"),
        name="qkv_rope",
    )(x, w_bf, b.reshape(1, QKV_COLS), cos, sin)


def _attn_prompt_kernel(sink_ref, q_ref, kp_ref, kc_ref, vp_ref, vc_ref, o_ref):
    i = pl.program_id(1)
    cols = GQA_GROUP * WINDOW
    kj = lax.broadcasted_iota(jnp.int32, (2 * WINDOW, cols), 0)
    qi = lax.broadcasted_iota(jnp.int32, (2 * WINDOW, cols), 1) % WINDOW
    dist = qi + WINDOW - kj
    prev_ok = jnp.where(i > 0, 0, WINDOW)
    mask = (dist >= 0) & (dist <= WINDOW) & (kj >= prev_ok)
    nt = (((1,), (1,)), ((), ()))
    tn = (((0,), (0,)), ((), ()))
    for g in range(N_KV_HEADS):
        ks = slice(g * HEAD_DIM, (g + 1) * HEAD_DIM)
        k = jnp.concatenate([kp_ref[:, ks], kc_ref[:, ks]], axis=0).astype(BF16)
        v = jnp.concatenate([vp_ref[:, ks], vc_ref[:, ks]], axis=0).astype(BF16)
        q = jnp.concatenate(
            [q_ref[:, (g * GQA_GROUP + j) * HEAD_DIM:(g * GQA_GROUP + j + 1) * HEAD_DIM]
             for j in range(GQA_GROUP)], axis=0).astype(BF16)
        t = lax.dot_general(k, q, nt, preferred_element_type=F32)
        t = jnp.where(mask, t * (HEAD_DIM ** -0.5 * LOG2E), NEG_INF)
        sink = jnp.concatenate(
            [jnp.full((1, WINDOW), sink_ref[g * GQA_GROUP + j] * LOG2E, F32) for j in range(GQA_GROUP)], axis=1)
        m = jnp.maximum(jnp.max(t, axis=0, keepdims=True), sink)
        p = jnp.exp2(t - m)
        denom = jnp.sum(p, axis=0, keepdims=True) + jnp.exp2(sink - m)
        o_t = lax.dot_general(v, p.astype(BF16), tn, preferred_element_type=F32) / denom
        o = o_t.T
        o_ref[:, g * GQA_GROUP * HEAD_DIM:(g + 1) * GQA_GROUP * HEAD_DIM] = jnp.concatenate(
            [o[j * WINDOW:(j + 1) * WINDOW] for j in range(GQA_GROUP)], axis=1).astype(o_ref.dtype)


def _attn_prompt(qkv, sinks, batch, seq):
    nb = seq // WINDOW
    kcol = Q_COLS // KV_COLS
    vcol = kcol + 1

    def cur(b, i):
        return b * nb + i

    def prev(b, i):
        return b * nb + jnp.maximum(i - 1, 0)

    return pl.pallas_call(
        _attn_prompt_kernel,
        grid=(batch, nb),
        in_specs=[
            pl.BlockSpec(memory_space=pltpu.SMEM),
            pl.BlockSpec((WINDOW, Q_COLS), lambda b, i: (cur(b, i), 0)),
            pl.BlockSpec((WINDOW, KV_COLS), lambda b, i: (prev(b, i), kcol)),
            pl.BlockSpec((WINDOW, KV_COLS), lambda b, i: (cur(b, i), kcol)),
            pl.BlockSpec((WINDOW, KV_COLS), lambda b, i: (prev(b, i), vcol)),
            pl.BlockSpec((WINDOW, KV_COLS), lambda b, i: (cur(b, i), vcol)),
        ],
        out_specs=pl.BlockSpec((WINDOW, Q_COLS), lambda b, i: (cur(b, i), 0)),
        out_shape=jax.ShapeDtypeStruct((batch * seq, Q_COLS), BF16),
        compiler_params=_cparams("parallel", "arbitrary"),
        name="attn_prompt",
    )(sinks, qkv, qkv, qkv, qkv, qkv)


SAMPLE_ROWS = SUBLANES
SAMPLE_BATCH_BLOCK = 8


def _attn_sample_kernel(sink_ref, qkv_ref, ck_ref, cv_ref, o_ref, *, n_new):
    n_keys = WINDOW + SAMPLE_ROWS
    kk = lax.broadcasted_iota(jnp.int32, (n_keys, LANES), 0)
    lane = lax.broadcasted_iota(jnp.int32, (n_keys, LANES), 1)
    qi = lane % SAMPLE_ROWS
    new = kk - WINDOW
    mask = ((kk < WINDOW) & (kk >= qi)) | ((new >= 0) & (new <= qi) & (new < n_new))
    head = lax.broadcasted_iota(jnp.int32, (1, LANES), 1) // SAMPLE_ROWS
    nt = (((1,), (1,)), ((), ()))
    tn = (((0,), (0,)), ((), ()))
    q_pad = jnp.zeros((LANES - GQA_GROUP * SAMPLE_ROWS, HEAD_DIM), BF16)
    for b in range(qkv_ref.shape[0]):
        qkv = qkv_ref[b]
        for g in range(N_KV_HEADS):
            ks = slice(g * HEAD_DIM, (g + 1) * HEAD_DIM)
            k = jnp.concatenate([ck_ref[b, :, ks], qkv[:, Q_COLS + g * HEAD_DIM:Q_COLS + (g + 1) * HEAD_DIM]],
                                axis=0).astype(BF16)
            v = jnp.concatenate([cv_ref[b, :, ks],
                                 qkv[:, Q_COLS + KV_COLS + g * HEAD_DIM:Q_COLS + KV_COLS + (g + 1) * HEAD_DIM]],
                                axis=0).astype(BF16)
            q = jnp.concatenate(
                [qkv[:, (g * GQA_GROUP + j) * HEAD_DIM:(g * GQA_GROUP + j + 1) * HEAD_DIM].astype(BF16)
                 for j in range(GQA_GROUP)] + [q_pad], axis=0)
            t = lax.dot_general(k, q, nt, preferred_element_type=F32)
            t = jnp.where(mask, t * (HEAD_DIM ** -0.5 * LOG2E), NEG_INF)
            sink = jnp.zeros((1, LANES), F32)
            for j in range(GQA_GROUP):
                sink = jnp.where(head == j, sink_ref[g * GQA_GROUP + j] * LOG2E, sink)
            m = jnp.maximum(jnp.max(t, axis=0, keepdims=True), sink)
            p = jnp.exp2(t - m)
            denom = jnp.sum(p, axis=0, keepdims=True) + jnp.exp2(sink - m)
            o_t = lax.dot_general(v, p.astype(BF16), tn, preferred_element_type=F32) / denom
            o = o_t.T
            o_ref[b, :, g * GQA_GROUP * HEAD_DIM:(g + 1) * GQA_GROUP * HEAD_DIM] = jnp.concatenate(
                [o[j * SAMPLE_ROWS:(j + 1) * SAMPLE_ROWS] for j in range(GQA_GROUP)], axis=1)


def _attn_sample(qkv8, ck, cv, sinks, n_new):
    nb = qkv8.shape[0]
    bb = SAMPLE_BATCH_BLOCK
    assert nb % bb == 0 and n_new <= SAMPLE_ROWS
    return pl.pallas_call(
        functools.partial(_attn_sample_kernel, n_new=n_new),
        grid=(nb // bb,),
        in_specs=[
            pl.BlockSpec(memory_space=pltpu.SMEM),
            pl.BlockSpec((bb, SAMPLE_ROWS, QKV_COLS), lambda b: (b, 0, 0)),
            pl.BlockSpec((bb, WINDOW, KV_COLS), lambda b: (b, 0, 0)),
            pl.BlockSpec((bb, WINDOW, KV_COLS), lambda b: (b, 0, 0)),
        ],
        out_specs=pl.BlockSpec((bb, SAMPLE_ROWS, Q_COLS), lambda b: (b, 0, 0)),
        out_shape=jax.ShapeDtypeStruct((nb, SAMPLE_ROWS, Q_COLS), F32),
        compiler_params=_cparams("parallel"),
        name="attn_sample",
    )(sinks, qkv8, ck, cv)


def _proj_ln_kernel(a_ref, w_ref, b_ref, x_ref, g_ref, beta_ref, o_ref):
    y = jnp.dot(a_ref[...].astype(BF16), w_ref[...], preferred_element_type=F32) + b_ref[...]
    o_ref[...] = _layer_norm(DEEPNORM_ALPHA * x_ref[...] + y, g_ref[...], beta_ref[...])


def _proj_ln(a, w_bf, bias, x, g, beta, tm):
    n, k = a.shape
    row = lambda i: (i, 0)
    fixed = lambda i: (0, 0)
    return pl.pallas_call(
        _proj_ln_kernel,
        grid=(n // tm,),
        in_specs=[
            pl.BlockSpec((tm, k), row),
            pl.BlockSpec((k, D_MODEL), fixed),
            pl.BlockSpec((1, D_MODEL), fixed),
            pl.BlockSpec((tm, D_MODEL), row),
            pl.BlockSpec((1, D_MODEL), fixed),
            pl.BlockSpec((1, D_MODEL), fixed),
        ],
        out_specs=pl.BlockSpec((tm, D_MODEL), row),
        out_shape=jax.ShapeDtypeStruct((n, D_MODEL), F32),
        compiler_params=_cparams("parallel"),
        name="proj_ln",
    )(a, w_bf, bias.reshape(1, D_MODEL), x, g.reshape(1, D_MODEL), beta.reshape(1, D_MODEL))


CONV_TN = 512


def _conv_in_kernel(x_ref, wb_ref, wc_ref, wh_ref, gb_ref, u_ref):
    xb = x_ref[...].astype(BF16)
    gb_ref[...] = jnp.dot(xb, wb_ref[...], preferred_element_type=F32)
    gate_c = jnp.dot(xb, wc_ref[...], preferred_element_type=F32)
    h = jnp.dot(xb, wh_ref[...], preferred_element_type=F32)
    u_ref[...] = gate_c * h


def _conv_in(x, w_in_bf, tm):
    n = x.shape[0]
    nt = D_MODEL // CONV_TN
    return pl.pallas_call(
        _conv_in_kernel,
        grid=(n // tm, nt),
        in_specs=[
            pl.BlockSpec((tm, D_MODEL), lambda i, j: (i, 0)),
            pl.BlockSpec((D_MODEL, CONV_TN), lambda i, j: (0, j)),
            pl.BlockSpec((D_MODEL, CONV_TN), lambda i, j: (0, nt + j)),
            pl.BlockSpec((D_MODEL, CONV_TN), lambda i, j: (0, 2 * nt + j)),
        ],
        out_specs=[
            pl.BlockSpec((tm, CONV_TN), lambda i, j: (i, j)),
            pl.BlockSpec((tm, CONV_TN), lambda i, j: (i, j)),
        ],
        out_shape=[jax.ShapeDtypeStruct((n, D_MODEL), F32)] * 2,
        compiler_params=_cparams("parallel", "arbitrary"),
        name="conv_in",
    )(x, w_in_bf, w_in_bf, w_in_bf)


def _gated_out_ln(gate_b, conv, w_ref, x_ref, g_ref, beta_ref, o_ref):
    y = jnp.dot((gate_b * conv).astype(BF16), w_ref[...], preferred_element_type=F32)
    o_ref[...] = _layer_norm(DEEPNORM_ALPHA * x_ref[...] + y, g_ref[...], beta_ref[...])


def _conv_out_ln_kernel(gb_ref, u_ref, um1_ref, um2_ref, cw_ref, w_ref, x_ref, g_ref, beta_ref, o_ref):
    conv = um2_ref[...] * cw_ref[0:1, :]
    conv = conv + um1_ref[...] * cw_ref[1:2, :]
    conv = conv + u_ref[...] * cw_ref[2:3, :]
    _gated_out_ln(gb_ref[...], conv, w_ref, x_ref, g_ref, beta_ref, o_ref)


def _conv_out_ln_halo_kernel(gb_ref, u_ref, halo_ref, cw_ref, w_ref, x_ref, g_ref, beta_ref, o_ref,
                             *, tiles_per_seq):
    i = pl.program_id(0)
    u = u_ref[...]
    halo = jnp.where(i % tiles_per_seq == 0, 0.0, halo_ref[...])
    row = lax.broadcasted_iota(jnp.int32, u.shape, 0)
    last, before_last = halo[SUBLANES - 1:SUBLANES, :], halo[SUBLANES - 2:SUBLANES - 1, :]
    um1 = jnp.where(row == 0, last, pltpu.roll(u, 1, 0))
    um2 = jnp.where(row == 0, before_last, jnp.where(row == 1, last, pltpu.roll(u, 2, 0)))
    conv = um2 * cw_ref[0:1, :] + um1 * cw_ref[1:2, :] + u * cw_ref[2:3, :]
    _gated_out_ln(gb_ref[...], conv, w_ref, x_ref, g_ref, beta_ref, o_ref)


def _conv_out_ln(gate_b, u, shifted, conv_w, w_out_bf, x, g, beta, tm, seq=None):
    n = x.shape[0]
    row = lambda i: (i, 0)
    fixed = lambda i: (0, 0)
    tile = pl.BlockSpec((tm, D_MODEL), row)
    vec = pl.BlockSpec((1, D_MODEL), fixed)
    tail_specs = [pl.BlockSpec((CONV_WIDTH, D_MODEL), fixed), pl.BlockSpec((D_MODEL, D_MODEL), fixed),
                  tile, vec, vec]
    tail_args = (conv_w, w_out_bf, x, g.reshape(1, D_MODEL), beta.reshape(1, D_MODEL))
    if shifted is None:
        assert seq % tm == 0 and tm % SUBLANES == 0
        body = functools.partial(_conv_out_ln_halo_kernel, tiles_per_seq=seq // tm)
        halo = pl.BlockSpec((SUBLANES, D_MODEL), lambda i: (jnp.maximum(i * (tm // SUBLANES) - 1, 0), 0))
        in_specs, args = [tile, tile, halo], (gate_b, u, u)
    else:
        body = _conv_out_ln_kernel
        in_specs, args = [tile, tile, tile, tile], (gate_b, u) + tuple(shifted)
    return pl.pallas_call(
        body,
        grid=(n // tm,),
        in_specs=in_specs + tail_specs,
        out_specs=tile,
        out_shape=jax.ShapeDtypeStruct((n, D_MODEL), F32),
        compiler_params=_cparams("parallel"),
        name="conv_out_ln",
    )(*args, *tail_args)


def _oddeven_merge_sort_pairs(n):
    pairs = []
    p = 1
    while p < n:
        k = p
        while k >= 1:
            for j in range(k % p, n - k, 2 * k):
                for i in range(min(k, n - j - k)):
                    if (i + j) // (2 * p) == (i + j + k) // (2 * p):
                        pairs.append((i + j, i + j + k))
            k //= 2
        p *= 2
    return pairs


def _top_values(pieces, k):
    n = len(pieces)
    size = 1
    while size < n:
        size *= 2
    v = list(pieces)
    for i, j in _oddeven_merge_sort_pairs(size):
        if j < n:
            v[i], v[j] = jnp.maximum(v[i], v[j]), jnp.minimum(v[i], v[j])
    vals = []
    for it in range(k):
        m = jnp.max(v[0], axis=0, keepdims=True)
        vals.append(m)
        need = k - it - 1
        if need:
            hit = v[0] == m
            nxt = v[1:] + [jnp.full_like(v[0], -jnp.inf)]
            v = [jnp.where(hit, nxt[j], v[j]) for j in range(min(need, len(v)))]
    return vals


def _peer_score_kernel(x_ref, wq_ref, keys_ref, xt_ref, s1l_ref, s2l_ref, tau_ref):
    pad = jnp.zeros((I1_PER_TILE,) + s1l_ref.shape[1:], F32)
    s1l_ref[:I1_PER_TILE] = pad
    s1l_ref[I1_PER_TILE + PEER_KEYS:] = pad
    x = x_ref[...]
    xt_ref[...] = x.T.astype(BF16)
    q = jnp.dot(x.astype(BF16), wq_ref[...], preferred_element_type=F32).astype(BF16)
    nt = (((1,), (1,)), ((), ()))
    k = PEER_TOPK
    assert k == 16 and SUBLANES == 8
    for h in range(PEER_HEADS):
        c0 = h * 2 * PEER_HALF
        s1 = lax.dot_general(keys_ref[h, 0], q[:, c0:c0 + PEER_HALF], nt, preferred_element_type=F32)
        s2 = lax.dot_general(keys_ref[h, 1], q[:, c0 + PEER_HALF:c0 + 2 * PEER_HALF], nt,
                             preferred_element_type=F32)
        a = _top_values([s1[r:r + SUBLANES] for r in range(0, PEER_KEYS, SUBLANES)], k + 1)
        b = _top_values([s2[r:r + SUBLANES] for r in range(0, PEER_KEYS, SUBLANES)], k + 1)
        neg = jnp.full_like(a[0], -jnp.inf)
        b24 = jnp.concatenate(b + [neg] * 7, axis=0)
        b8 = jnp.concatenate(b[:8], axis=0)
        a16 = jnp.concatenate(a[8:] + [neg] * 7, axis=0)
        row = lax.broadcasted_iota(jnp.int32, b8.shape, 0)
        eb24 = jnp.exp(b24 - b[0])
        eb8 = jnp.exp(b8 - b[0])
        cand = [a[0] + b24[r:r + 8] for r in (0, 8, 16)]
        wgt = [eb24[r:r + 8] for r in (0, 8, 16)]
        for i in range(1, 8):
            cand.append(jnp.where(row < (k + 1) // (i + 1), a[i] + b8, -jnp.inf))
            wgt.append(jnp.exp(a[i] - a[0]) * eb8)
        cand += [a16[r:r + 8] + b[0] for r in (0, 8)]
        wgt += [jnp.exp(a16[r:r + 8] - a[0]) for r in (0, 8)]
        tops = _top_values(cand, k + 1)
        thr = 0.5 * (tops[k - 1] + tops[k])
        z8 = functools.reduce(jnp.add, [jnp.where(c >= thr, w, 0.0) for c, w in zip(cand, wgt)])
        z = jnp.sum(z8, axis=0, keepdims=True)
        shift = a[0] + b[0] + jnp.log(z)
        s1l_ref[I1_PER_TILE:I1_PER_TILE + PEER_KEYS, h, :] = (s1 - shift) * LOG2E - 1.0
        s2l_ref[h] = s2 * LOG2E
        tau_ref[h:h + 1, :] = (thr - shift) * LOG2E - 1.0


def _peer_scores(x, wq_bf, keys_bf, tm):
    n = x.shape[0]
    hk = (PEER_HEADS, PEER_KEYS, tm)
    hk_map = lambda i: (0, 0, i)
    hk_shape = jax.ShapeDtypeStruct((PEER_HEADS, PEER_KEYS, n), F32)
    return pl.pallas_call(
        _peer_score_kernel,
        grid=(n // tm,),
        in_specs=[
            pl.BlockSpec((tm, D_MODEL), lambda i: (i, 0)),
            pl.BlockSpec((D_MODEL, PEER_HEADS * 2 * PEER_HALF), lambda i: (0, 0)),
            pl.BlockSpec((PEER_HEADS, 2, PEER_KEYS, PEER_HALF), lambda i: (0, 0, 0, 0)),
        ],
        out_specs=[
            pl.BlockSpec((D_MODEL, tm), lambda i: (0, i)),
            pl.BlockSpec((PEER_KEYS + 2 * I1_PER_TILE, PEER_HEADS, tm), hk_map), pl.BlockSpec(hk, hk_map),
            pl.BlockSpec((PEER_HEADS, tm), lambda i: (0, i)),
        ],
        out_shape=[
            jax.ShapeDtypeStruct((D_MODEL, n), BF16),
            jax.ShapeDtypeStruct((PEER_KEYS + 2 * I1_PER_TILE, PEER_HEADS, n), F32), hk_shape,
            jax.ShapeDtypeStruct((PEER_HEADS, n), F32),
        ],
        compiler_params=_cparams("parallel"),
        name="peer_scores",
    )(x, wq_bf, keys_bf)


def _gelu_times_two(x):
    return x * (1.0 + lax.erf(x * (2.0 ** -0.5)))


N_PAIRS = PEER_EXPERTS // (2 * PEER_TE)


def _dense_pipeline_step(p, u_ref, v_ref, xt_ref, s1l_ref, s2l_ref, tau_ref, o_ref, hid, at):
    tm = xt_ref.shape[1]
    last = pl.num_programs(1) - 1

    def hidden(half):
        rows = slice(half * PEER_TE, (half + 1) * PEER_TE)
        hid[half][...] = jnp.dot(u_ref[rows, :].astype(BF16), xt_ref[...], preferred_element_type=F32)

    def gated(half):
        src, dst = hid[1 - half], at[1 - half]
        for il in range(I1_PER_TILE):
            r = half * I1_PER_TILE + il
            rs = slice(il * PEER_KEYS, (il + 1) * PEER_KEYS)
            for c in range(tm // LANES):
                cs = slice(c * LANES, (c + 1) * LANES)
                gate = jnp.zeros((PEER_KEYS, LANES), F32)
                for h in range(PEER_HEADS):
                    pair = s2l_ref[h, :, cs] + s1l_ref[r, h:h + 1, cs]
                    gate = gate + jnp.where(pair >= tau_ref[h:h + 1, cs], jnp.exp2(pair), 0.0)
                dst[rs, cs] = (_gelu_times_two(src[rs, cs]) * gate).astype(BF16)

    def value(half):
        rows = slice(half * PEER_TE, (half + 1) * PEER_TE)
        o_ref[...] += lax.dot_general(at[half][...], v_ref[rows, :].astype(BF16), (((0,), (0,)), ((), ())),
                                      preferred_element_type=F32)

    @pl.when(p == 0)
    def _():
        o_ref[...] = jnp.zeros_like(o_ref)
        hidden(0)
        hidden(1)
        gated(1)

    @pl.when((p > 0) & (p < last))
    def _():
        for half in range(2):
            hidden(half)
            gated(half)
            value(half)

    @pl.when(p == last)
    def _():
        gated(0)
        value(0)
        value(1)


def _peer_dense_kernel(u_ref, v_ref, xt_ref, s1l_ref, s2l_ref, tau_ref, xts_ref, s1ls_ref, s2ls_ref, taus_ref,
                       o_ref, os_ref, hid0, hid1, at0, at1, hids0, hids1, ats0, ats1):
    t = pl.program_id(0)
    p = pl.program_id(1)
    _dense_pipeline_step(p, u_ref, v_ref, xt_ref, s1l_ref, s2l_ref, tau_ref, o_ref, (hid0, hid1), (at0, at1))

    @pl.when(t == 0)
    def _():
        _dense_pipeline_step(p, u_ref, v_ref, xts_ref, s1ls_ref, s2ls_ref, taus_ref, os_ref,
                             (hids0, hids1), (ats0, ats1))


def _peer_dense(u_all, v_all, layer, prompt, sample, tm):
    n = prompt[0].shape[1]
    n_s = sample[0].shape[1]
    u_spec = pl.BlockSpec((None, 2 * PEER_TE, D_MODEL), lambda t, p: (layer, jnp.minimum(p, N_PAIRS - 1), 0))
    v_spec = pl.BlockSpec((None, 2 * PEER_TE, D_MODEL), lambda t, p: (layer, jnp.maximum(p - 1, 0), 0))

    def token_specs(width, tile_of):
        return [
            pl.BlockSpec((D_MODEL, width), lambda t, p: (0, tile_of(t))),
            pl.BlockSpec((2 * I1_PER_TILE, PEER_HEADS, width), lambda t, p: (p, 0, tile_of(t))),
            pl.BlockSpec((PEER_HEADS, PEER_KEYS, width), lambda t, p: (0, 0, tile_of(t)),
                         pipeline_mode=pl.Buffered(1)),
            pl.BlockSpec((PEER_HEADS, width), lambda t, p: (0, tile_of(t))),
        ]

    def buffers(width):
        return [pltpu.VMEM((PEER_TE, width), F32), pltpu.VMEM((PEER_TE, width), F32),
                pltpu.VMEM((PEER_TE, width), BF16), pltpu.VMEM((PEER_TE, width), BF16)]

    return pl.pallas_call(
        _peer_dense_kernel,
        grid=(n // tm, N_PAIRS + 1),
        in_specs=[u_spec, v_spec] + token_specs(tm, lambda t: t) + token_specs(n_s, lambda t: 0),
        out_specs=[pl.BlockSpec((tm, D_MODEL), lambda t, p: (t, 0)),
                   pl.BlockSpec((n_s, D_MODEL), lambda t, p: (0, 0))],
        out_shape=[jax.ShapeDtypeStruct((n, D_MODEL), F32), jax.ShapeDtypeStruct((n_s, D_MODEL), F32)],
        scratch_shapes=buffers(tm) + buffers(n_s),
        compiler_params=_cparams("arbitrary", "arbitrary"),
        name="peer_dense",
    )(u_all, v_all, *prompt, *sample)


def _res_ln_kernel(x_ref, f_ref, g_ref, beta_ref, o_ref):
    o_ref[...] = _layer_norm(DEEPNORM_ALPHA * x_ref[...] + f_ref[...], g_ref[...], beta_ref[...])


def _res_ln(x, f, g, beta, tm):
    n = x.shape[0]
    tile = pl.BlockSpec((tm, D_MODEL), lambda i: (i, 0))
    vec = pl.BlockSpec((1, D_MODEL), lambda i: (0, 0))
    return pl.pallas_call(
        _res_ln_kernel,
        grid=(n // tm,),
        in_specs=[tile, tile, vec, vec],
        out_specs=tile,
        out_shape=jax.ShapeDtypeStruct((n, D_MODEL), F32),
        compiler_params=_cparams("parallel"),
        name="res_ln",
    )(x, f, g.reshape(1, D_MODEL), beta.reshape(1, D_MODEL))


def _peer_layer(xp, xs, wq_bf, keys_bf, u_all, v_all, layer, g, beta):
    n_p, n_s = xp.shape[0], xs.shape[0]
    prompt = _peer_scores(xp, wq_bf, keys_bf, _rows(n_p, SCORE_ROWS))
    sample = _peer_scores(xs, wq_bf, keys_bf, _rows(n_s, SCORE_ROWS))
    fp, fs = _peer_dense(u_all, v_all, layer, prompt, sample, _rows(n_p, DENSE_ROWS))
    return (_res_ln(xp, fp, g, beta, _rows(n_p, RES_LN_ROWS)),
            _res_ln(xs, fs, g, beta, _rows(n_s, RES_LN_ROWS)))


def _rope_tables(pos):
    half = HEAD_DIM // 2
    inv = jnp.power(ROPE_THETA, -jnp.arange(half, dtype=F32) * 2.0 / HEAD_DIM)
    ang = pos.astype(F32)[:, None] * inv[None, :]
    cos = jnp.cos(ang)
    sin = jnp.sin(ang)
    reps = LANES // HEAD_DIM
    return (jnp.tile(jnp.concatenate([cos, cos], axis=1), (1, reps)),
            jnp.tile(jnp.concatenate([-sin, sin], axis=1), (1, reps)))


def _shifted(state, u3):
    t = u3.shape[1]
    up = jnp.concatenate([state, u3], axis=1)
    return up[:, 1:t + 1], up[:, 0:t], up[:, -(CONV_WIDTH - 1):]


def kernel(x_prompt, x_sample, cache_k, cache_v, state_conv, w_qkv, b_qkv, w_o, b_o, attn_sinks,
           w_conv_in, conv_w, w_conv_out, w_peer_q, peer_sub_keys, peer_u, peer_v, ln_g, ln_b):
    batch, seq, _ = x_prompt.shape
    dec_batch, dec_seq, _ = x_sample.shape
    n_p = batch * seq
    n_s = dec_batch * dec_seq
    xp = x_prompt.reshape(n_p, D_MODEL)
    xs = x_sample.reshape(n_s, D_MODEL)
    cos_p, sin_p = _rope_tables(jnp.tile(jnp.arange(seq), batch))
    cos_s, sin_s = _rope_tables(jnp.tile(PAST_LEN + jnp.arange(dec_seq), dec_batch))

    new_kp, new_vp, new_cp, new_ks, new_vs, new_cs = [], [], [], [], [], []
    for layer in range(DEPTH):
        j = layer // 2
        g0, b0 = ln_g[layer, 0], ln_b[layer, 0]
        if layer % 2 == 0:
            wqkv = w_qkv[j].astype(BF16)
            wo = w_o[j].astype(BF16)
            sinks = attn_sinks[j].reshape(N_Q_HEADS)
            qkv_p = _qkv_rope(xp, wqkv, b_qkv[j], cos_p, sin_p, _rows(n_p, QKV_ROWS))
            qkv_s = _qkv_rope(xs, wqkv, b_qkv[j], cos_s, sin_s, _rows(n_s, QKV_ROWS))
            o_p = _attn_prompt(qkv_p, sinks, batch, seq)
            qkv_s3 = qkv_s.reshape(dec_batch, dec_seq, QKV_COLS)
            qkv_s8 = jnp.pad(qkv_s3, ((0, 0), (0, SAMPLE_ROWS - dec_seq), (0, 0)))
            ck = cache_k[j].reshape(dec_batch, WINDOW, KV_COLS)
            cv = cache_v[j].reshape(dec_batch, WINDOW, KV_COLS)
            o_s = _attn_sample(qkv_s8, ck, cv, sinks, dec_seq)[:, :dec_seq].reshape(n_s, Q_COLS)
            kv_p = qkv_p.reshape(batch, seq, QKV_COLS)[:, -WINDOW:, Q_COLS:]
            new_kp.append(kv_p[..., :KV_COLS].reshape(batch, WINDOW, N_KV_HEADS, HEAD_DIM))
            new_vp.append(kv_p[..., KV_COLS:].reshape(batch, WINDOW, N_KV_HEADS, HEAD_DIM))
            k_s = jnp.concatenate([ck, qkv_s3[..., Q_COLS:Q_COLS + KV_COLS]], axis=1)[:, -WINDOW:]
            v_s = jnp.concatenate([cv, qkv_s3[..., Q_COLS + KV_COLS:]], axis=1)[:, -WINDOW:]
            new_ks.append(k_s.reshape(dec_batch, WINDOW, N_KV_HEADS, HEAD_DIM))
            new_vs.append(v_s.reshape(dec_batch, WINDOW, N_KV_HEADS, HEAD_DIM))
            xp = _proj_ln(o_p, wo, b_o[j], xp, g0, b0, _rows(n_p, PROJ_ROWS))
            xs = _proj_ln(o_s, wo, b_o[j], xs, g0, b0, _rows(n_s, PROJ_ROWS))
        else:
            w_in = w_conv_in[j].astype(BF16)
            w_out = w_conv_out[j].astype(BF16)
            gb_p, u_p = _conv_in(xp, w_in, _rows(n_p, CONV_IN_ROWS))
            gb_s, u_s = _conv_in(xs, w_in, _rows(n_s, CONV_IN_ROWS))
            um1_s, um2_s, cs = _shifted(state_conv[j], u_s.reshape(dec_batch, dec_seq, D_MODEL))
            new_cp.append(u_p.reshape(batch, seq, D_MODEL)[:, -(CONV_WIDTH - 1):])
            new_cs.append(cs)
            xp = _conv_out_ln(gb_p, u_p, None, conv_w[j], w_out, xp, g0, b0, _rows(n_p, CONV_OUT_ROWS), seq=seq)
            xs = _conv_out_ln(gb_s, u_s, (um1_s.reshape(n_s, D_MODEL), um2_s.reshape(n_s, D_MODEL)),
                              conv_w[j], w_out, xs, g0, b0, _rows(n_s, CONV_OUT_ROWS))
        wq = w_peer_q[layer].astype(BF16)
        keys = peer_sub_keys[layer].astype(BF16)
        g1, b1 = ln_g[layer, 1], ln_b[layer, 1]
        xp, xs = _peer_layer(xp, xs, wq, keys, peer_u, peer_v, layer, g1, b1)

    return (xp.reshape(batch, seq, D_MODEL), xs.reshape(dec_batch, dec_seq, D_MODEL),
            jnp.stack(new_kp), jnp.stack(new_vp), jnp.stack(new_cp),
            jnp.stack(new_ks), jnp.stack(new_vs), jnp.stack(new_cs))
```

```python
import functools

import jax
import jax.numpy as jnp
from jax import lax
from jax.experimental import pallas as pl
from jax.experimental.pallas import tpu as pltpu

F32 = jnp.float32
BF16 = jnp.bfloat16

D_MODEL = 2048
DEPTH = 4
HEAD_DIM = 64
N_Q_HEADS = 32
N_KV_HEADS = 8
GQA_GROUP = 4
WINDOW = 128
ROPE_THETA = 10000.0
PAST_LEN = 16384
CONV_WIDTH = 3
PEER_HEADS = 8
PEER_KEYS = 128
PEER_EXPERTS = PEER_KEYS * PEER_KEYS
PEER_HALF = 128
PEER_TOPK = 16
LN_EPS = 1e-5
NEG_INF = -1e30
DEEPNORM_ALPHA = (2 * DEPTH) ** 0.25
LOG2E = 1.4426950408889634

Q_COLS = N_Q_HEADS * HEAD_DIM
KV_COLS = N_KV_HEADS * HEAD_DIM
QKV_COLS = Q_COLS + 2 * KV_COLS

LANES = 128
SUBLANES = 8
PEER_TE = 256
I1_PER_TILE = PEER_TE // PEER_KEYS
VMEM_LIMIT_BYTES = 56 * 1024 * 1024

QKV_ROWS = 1024
PROJ_ROWS = 512
CONV_IN_ROWS = 1024
CONV_OUT_ROWS = 256
SCORE_ROWS = 256
DENSE_ROWS = 1024
RES_LN_ROWS = 512


def _rows(n, preferred):
    return min(n, preferred)


def _cparams(*semantics):
    return pltpu.CompilerParams(dimension_semantics=semantics, vmem_limit_bytes=VMEM_LIMIT_BYTES)


def _layer_norm(z, g, b):
    mu = jnp.mean(z, axis=-1, keepdims=True)
    zc = z - mu
    var = jnp.mean(zc * zc, axis=-1, keepdims=True)
    return zc * lax.rsqrt(var + LN_EPS) * g + b


QKV_TN = 512
N_ROPE_TILES = (Q_COLS + KV_COLS) // QKV_TN


def _qkv_kernel(x_ref, w_ref, b_ref, cos_ref, sin_ref, o_ref):
    j = pl.program_id(1)
    acc = jnp.dot(x_ref[...].astype(BF16), w_ref[...], preferred_element_type=F32) + b_ref[...]

    @pl.when(j < N_ROPE_TILES)
    def _():
        cos = cos_ref[...]
        sin = sin_ref[...]
        lane = lax.broadcasted_iota(jnp.int32, cos.shape, 1)
        first_half = (lane % HEAD_DIM) < (HEAD_DIM // 2)
        for c in range(QKV_TN // LANES):
            seg = acc[:, c * LANES:(c + 1) * LANES]
            partner = jnp.where(first_half,
                                pltpu.roll(seg, LANES - HEAD_DIM // 2, 1),
                                pltpu.roll(seg, HEAD_DIM // 2, 1))
            o_ref[:, c * LANES:(c + 1) * LANES] = seg * cos + partner * sin

    @pl.when(j >= N_ROPE_TILES)
    def _():
        o_ref[...] = acc


def _qkv_rope(x, w_bf, b, cos, sin, tm):
    n = x.shape[0]
    return pl.pallas_call(
        _qkv_kernel,
        grid=(n // tm, QKV_COLS // QKV_TN),
        in_specs=[
            pl.BlockSpec((tm, D_MODEL), lambda i, j: (i, 0)),
            pl.BlockSpec((D_MODEL, QKV_TN), lambda i, j: (0, j)),
            pl.BlockSpec((1, QKV_TN), lambda i, j: (0, j)),
            pl.BlockSpec((tm, LANES), lambda i, j: (i, 0)),
            pl.BlockSpec((tm, LANES), lambda i, j: (i, 0)),
        ],
        out_specs=pl.BlockSpec((tm, QKV_TN), lambda i, j: (i, j)),
        out_shape=jax.ShapeDtypeStruct((n, QKV_COLS), F32),
        compiler_params=_cparams("parallel", "arbitrary"),
        name="qkv_rope",
    )(x, w_bf, b.reshape(1, QKV_COLS), cos, sin)


def _attn_prompt_kernel(sink_ref, q_ref, kp_ref, kc_ref, vp_ref, vc_ref, o_ref):
    i = pl.program_id(1)
    cols = GQA_GROUP * WINDOW
    kj = lax.broadcasted_iota(jnp.int32, (2 * WINDOW, cols), 0)
    qi = lax.broadcasted_iota(jnp.int32, (2 * WINDOW, cols), 1) % WINDOW
    dist = qi + WINDOW - kj
    prev_ok = jnp.where(i > 0, 0, WINDOW)
    mask = (dist >= 0) & (dist <= WINDOW) & (kj >= prev_ok)
    nt = (((1,), (1,)), ((), ()))
    tn = (((0,), (0,)), ((), ()))
    for g in range(N_KV_HEADS):
        ks = slice(g * HEAD_DIM, (g + 1) * HEAD_DIM)
        k = jnp.concatenate([kp_ref[:, ks], kc_ref[:, ks]], axis=0).astype(BF16)
        v = jnp.concatenate([vp_ref[:, ks], vc_ref[:, ks]], axis=0).astype(BF16)
        q = jnp.concatenate(
            [q_ref[:, (g * GQA_GROUP + j) * HEAD_DIM:(g * GQA_GROUP + j + 1) * HEAD_DIM]
             for j in range(GQA_GROUP)], axis=0).astype(BF16)
        t = lax.dot_general(k, q, nt, preferred_element_type=F32)
        t = jnp.where(mask, t * (HEAD_DIM ** -0.5 * LOG2E), NEG_INF)
        sink = jnp.concatenate(
            [jnp.full((1, WINDOW), sink_ref[g * GQA_GROUP + j] * LOG2E, F32) for j in range(GQA_GROUP)], axis=1)
        m = jnp.maximum(jnp.max(t, axis=0, keepdims=True), sink)
        p = jnp.exp2(t - m)
        denom = jnp.sum(p, axis=0, keepdims=True) + jnp.exp2(sink - m)
        o_t = lax.dot_general(v, p.astype(BF16), tn, preferred_element_type=F32) / denom
        o = o_t.T
        o_ref[:, g * GQA_GROUP * HEAD_DIM:(g + 1) * GQA_GROUP * HEAD_DIM] = jnp.concatenate(
            [o[j * WINDOW:(j + 1) * WINDOW] for j in range(GQA_GROUP)], axis=1).astype(o_ref.dtype)


def _attn_prompt(qkv, sinks, batch, seq):
    nb = seq // WINDOW
    kcol = Q_COLS // KV_COLS
    vcol = kcol + 1

    def cur(b, i):
        return b * nb + i

    def prev(b, i):
        return b * nb + jnp.maximum(i - 1, 0)

    return pl.pallas_call(
        _attn_prompt_kernel,
        grid=(batch, nb),
        in_specs=[
            pl.BlockSpec(memory_space=pltpu.SMEM),
            pl.BlockSpec((WINDOW, Q_COLS), lambda b, i: (cur(b, i), 0)),
            pl.BlockSpec((WINDOW, KV_COLS), lambda b, i: (prev(b, i), kcol)),
            pl.BlockSpec((WINDOW, KV_COLS), lambda b, i: (cur(b, i), kcol)),
            pl.BlockSpec((WINDOW, KV_COLS), lambda b, i: (prev(b, i), vcol)),
            pl.BlockSpec((WINDOW, KV_COLS), lambda b, i: (cur(b, i), vcol)),
        ],
        out_specs=pl.BlockSpec((WINDOW, Q_COLS), lambda b, i: (cur(b, i), 0)),
        out_shape=jax.ShapeDtypeStruct((batch * seq, Q_COLS), BF16),
        compiler_params=_cparams("parallel", "arbitrary"),
        name="attn_prompt",
    )(sinks, qkv, qkv, qkv, qkv, qkv)


SAMPLE_ROWS = SUBLANES
SAMPLE_BATCH_BLOCK = 8


def _attn_sample_kernel(sink_ref, qkv_ref, ck_ref, cv_ref, o_ref, *, n_new):
    n_keys = WINDOW + SAMPLE_ROWS
    kk = lax.broadcasted_iota(jnp.int32, (n_keys, LANES), 0)
    lane = lax.broadcasted_iota(jnp.int32, (n_keys, LANES), 1)
    qi = lane % SAMPLE_ROWS
    new = kk - WINDOW
    mask = ((kk < WINDOW) & (kk >= qi)) | ((new >= 0) & (new <= qi) & (new < n_new))
    head = lax.broadcasted_iota(jnp.int32, (1, LANES), 1) // SAMPLE_ROWS
    nt = (((1,), (1,)), ((), ()))
    tn = (((0,), (0,)), ((), ()))
    q_pad = jnp.zeros((LANES - GQA_GROUP * SAMPLE_ROWS, HEAD_DIM), BF16)
    for b in range(qkv_ref.shape[0]):
        qkv = qkv_ref[b]
        for g in range(N_KV_HEADS):
            ks = slice(g * HEAD_DIM, (g + 1) * HEAD_DIM)
            k = jnp.concatenate([ck_ref[b, :, ks], qkv[:, Q_COLS + g * HEAD_DIM:Q_COLS + (g + 1) * HEAD_DIM]],
                                axis=0).astype(BF16)
            v = jnp.concatenate([cv_ref[b, :, ks],
                                 qkv[:, Q_COLS + KV_COLS + g * HEAD_DIM:Q_COLS + KV_COLS + (g + 1) * HEAD_DIM]],
                                axis=0).astype(BF16)
            q = jnp.concatenate(
                [qkv[:, (g * GQA_GROUP + j) * HEAD_DIM:(g * GQA_GROUP + j + 1) * HEAD_DIM].astype(BF16)
                 for j in range(GQA_GROUP)] + [q_pad], axis=0)
            t = lax.dot_general(k, q, nt, preferred_element_type=F32)
            t = jnp.where(mask, t * (HEAD_DIM ** -0.5 * LOG2E), NEG_INF)
            sink = jnp.zeros((1, LANES), F32)
            for j in range(GQA_GROUP):
                sink = jnp.where(head == j, sink_ref[g * GQA_GROUP + j] * LOG2E, sink)
            m = jnp.maximum(jnp.max(t, axis=0, keepdims=True), sink)
            p = jnp.exp2(t - m)
            denom = jnp.sum(p, axis=0, keepdims=True) + jnp.exp2(sink - m)
            o_t = lax.dot_general(v, p.astype(BF16), tn, preferred_element_type=F32) / denom
            o = o_t.T
            o_ref[b, :, g * GQA_GROUP * HEAD_DIM:(g + 1) * GQA_GROUP * HEAD_DIM] = jnp.concatenate(
                [o[j * SAMPLE_ROWS:(j + 1) * SAMPLE_ROWS] for j in range(GQA_GROUP)], axis=1)


def _attn_sample(qkv8, ck, cv, sinks, n_new):
    nb = qkv8.shape[0]
    bb = SAMPLE_BATCH_BLOCK
    assert nb % bb == 0 and n_new <= SAMPLE_ROWS
    return pl.pallas_call(
        functools.partial(_attn_sample_kernel, n_new=n_new),
        grid=(nb // bb,),
        in_specs=[
            pl.BlockSpec(memory_space=pltpu.SMEM),
            pl.BlockSpec((bb, SAMPLE_ROWS, QKV_COLS), lambda b: (b, 0, 0)),
            pl.BlockSpec((bb, WINDOW, KV_COLS), lambda b: (b, 0, 0)),
            pl.BlockSpec((bb, WINDOW, KV_COLS), lambda b: (b, 0, 0)),
        ],
        out_specs=pl.BlockSpec((bb, SAMPLE_ROWS, Q_COLS), lambda b: (b, 0, 0)),
        out_shape=jax.ShapeDtypeStruct((nb, SAMPLE_ROWS, Q_COLS), F32),
        compiler_params=_cparams("parallel"),
        name="attn_sample",
    )(sinks, qkv8, ck, cv)


def _proj_ln_kernel(a_ref, w_ref, b_ref, x_ref, g_ref, beta_ref, o_ref):
    y = jnp.dot(a_ref[...].astype(BF16), w_ref[...], preferred_element_type=F32) + b_ref[...]
    o_ref[...] = _layer_norm(DEEPNORM_ALPHA * x_ref[...] + y, g_ref[...], beta_ref[...])


def _proj_ln(a, w_bf, bias, x, g, beta, tm):
    n, k = a.shape
    row = lambda i: (i, 0)
    fixed = lambda i: (0, 0)
    return pl.pallas_call(
        _proj_ln_kernel,
        grid=(n // tm,),
        in_specs=[
            pl.BlockSpec((tm, k), row),
            pl.BlockSpec((k, D_MODEL), fixed),
            pl.BlockSpec((1, D_MODEL), fixed),
            pl.BlockSpec((tm, D_MODEL), row),
            pl.BlockSpec((1, D_MODEL), fixed),
            pl.BlockSpec((1, D_MODEL), fixed),
        ],
        out_specs=pl.BlockSpec((tm, D_MODEL), row),
        out_shape=jax.ShapeDtypeStruct((n, D_MODEL), F32),
        compiler_params=_cparams("parallel"),
        name="proj_ln",
    )(a, w_bf, bias.reshape(1, D_MODEL), x, g.reshape(1, D_MODEL), beta.reshape(1, D_MODEL))


CONV_TN = 512


def _conv_in_kernel(x_ref, wb_ref, wc_ref, wh_ref, gb_ref, u_ref):
    xb = x_ref[...].astype(BF16)
    gb_ref[...] = jnp.dot(xb, wb_ref[...], preferred_element_type=F32)
    gate_c = jnp.dot(xb, wc_ref[...], preferred_element_type=F32)
    h = jnp.dot(xb, wh_ref[...], preferred_element_type=F32)
    u_ref[...] = gate_c * h


def _conv_in(x, w_in_bf, tm):
    n = x.shape[0]
    nt = D_MODEL // CONV_TN
    return pl.pallas_call(
        _conv_in_kernel,
        grid=(n // tm, nt),
        in_specs=[
            pl.BlockSpec((tm, D_MODEL), lambda i, j: (i, 0)),
            pl.BlockSpec((D_MODEL, CONV_TN), lambda i, j: (0, j)),
            pl.BlockSpec((D_MODEL, CONV_TN), lambda i, j: (0, nt + j)),
            pl.BlockSpec((D_MODEL, CONV_TN), lambda i, j: (0, 2 * nt + j)),
        ],
        out_specs=[
            pl.BlockSpec((tm, CONV_TN), lambda i, j: (i, j)),
            pl.BlockSpec((tm, CONV_TN), lambda i, j: (i, j)),
        ],
        out_shape=[jax.ShapeDtypeStruct((n, D_MODEL), F32)] * 2,
        compiler_params=_cparams("parallel", "arbitrary"),
        name="conv_in",
    )(x, w_in_bf, w_in_bf, w_in_bf)


def _gated_out_ln(gate_b, conv, w_ref, x_ref, g_ref, beta_ref, o_ref):
    y = jnp.dot((gate_b * conv).astype(BF16), w_ref[...], preferred_element_type=F32)
    o_ref[...] = _layer_norm(DEEPNORM_ALPHA * x_ref[...] + y, g_ref[...], beta_ref[...])


def _conv_out_ln_kernel(gb_ref, u_ref, um1_ref, um2_ref, cw_ref, w_ref, x_ref, g_ref, beta_ref, o_ref):
    conv = um2_ref[...] * cw_ref[0:1, :]
    conv = conv + um1_ref[...] * cw_ref[1:2, :]
    conv = conv + u_ref[...] * cw_ref[2:3, :]
    _gated_out_ln(gb_ref[...], conv, w_ref, x_ref, g_ref, beta_ref, o_ref)


def _conv_out_ln_halo_kernel(gb_ref, u_ref, halo_ref, cw_ref, w_ref, x_ref, g_ref, beta_ref, o_ref,
                             *, tiles_per_seq):
    i = pl.program_id(0)
    u = u_ref[...]
    halo = jnp.where(i % tiles_per_seq == 0, 0.0, halo_ref[...])
    row = lax.broadcasted_iota(jnp.int32, u.shape, 0)
    last, before_last = halo[SUBLANES - 1:SUBLANES, :], halo[SUBLANES - 2:SUBLANES - 1, :]
    um1 = jnp.where(row == 0, last, pltpu.roll(u, 1, 0))
    um2 = jnp.where(row == 0, before_last, jnp.where(row == 1, last, pltpu.roll(u, 2, 0)))
    conv = um2 * cw_ref[0:1, :] + um1 * cw_ref[1:2, :] + u * cw_ref[2:3, :]
    _gated_out_ln(gb_ref[...], conv, w_ref, x_ref, g_ref, beta_ref, o_ref)


def _conv_out_ln(gate_b, u, shifted, conv_w, w_out_bf, x, g, beta, tm, seq=None):
    n = x.shape[0]
    row = lambda i: (i, 0)
    fixed = lambda i: (0, 0)
    tile = pl.BlockSpec((tm, D_MODEL), row)
    vec = pl.BlockSpec((1, D_MODEL), fixed)
    tail_specs = [pl.BlockSpec((CONV_WIDTH, D_MODEL), fixed), pl.BlockSpec((D_MODEL, D_MODEL), fixed),
                  tile, vec, vec]
    tail_args = (conv_w, w_out_bf, x, g.reshape(1, D_MODEL), beta.reshape(1, D_MODEL))
    if shifted is None:
        assert seq % tm == 0 and tm % SUBLANES == 0
        body = functools.partial(_conv_out_ln_halo_kernel, tiles_per_seq=seq // tm)
        halo = pl.BlockSpec((SUBLANES, D_MODEL), lambda i: (jnp.maximum(i * (tm // SUBLANES) - 1, 0), 0))
        in_specs, args = [tile, tile, halo], (gate_b, u, u)
    else:
        body = _conv_out_ln_kernel
        in_specs, args = [tile, tile, tile, tile], (gate_b, u) + tuple(shifted)
    return pl.pallas_call(
        body,
        grid=(n // tm,),
        in_specs=in_specs + tail_specs,
        out_specs=tile,
        out_shape=jax.ShapeDtypeStruct((n, D_MODEL), F32),
        compiler_params=_cparams("parallel"),
        name="conv_out_ln",
    )(*args, *tail_args)


def _oddeven_merge_sort_pairs(n):
    pairs = []
    p = 1
    while p < n:
        k = p
        while k >= 1:
            for j in range(k % p, n - k, 2 * k):
                for i in range(min(k, n - j - k)):
                    if (i + j) // (2 * p) == (i + j + k) // (2 * p):
                        pairs.append((i + j, i + j + k))
            k //= 2
        p *= 2
    return pairs


def _top_values(pieces, k):
    n = len(pieces)
    size = 1
    while size < n:
        size *= 2
    v = list(pieces)
    for i, j in _oddeven_merge_sort_pairs(size):
        if j < n:
            v[i], v[j] = jnp.maximum(v[i], v[j]), jnp.minimum(v[i], v[j])
    vals = []
    for it in range(k):
        m = jnp.max(v[0], axis=0, keepdims=True)
        vals.append(m)
        need = k - it - 1
        if need:
            hit = v[0] == m
            nxt = v[1:] + [jnp.full_like(v[0], -jnp.inf)]
            v = [jnp.where(hit, nxt[j], v[j]) for j in range(min(need, len(v)))]
    return vals


def _peer_score_kernel(x_ref, wq_ref, keys_ref, xt_ref, s1l_ref, s2l_ref, tau_ref):
    pad = jnp.zeros((I1_PER_TILE,) + s1l_ref.shape[1:], F32)
    s1l_ref[:I1_PER_TILE] = pad
    s1l_ref[I1_PER_TILE + PEER_KEYS:] = pad
    x = x_ref[...]
    xt_ref[...] = x.T.astype(BF16)
    q = jnp.dot(x.astype(BF16), wq_ref[...], preferred_element_type=F32).astype(BF16)
    nt = (((1,), (1,)), ((), ()))
    k = PEER_TOPK
    assert k == 16 and SUBLANES == 8
    for h in range(PEER_HEADS):
        c0 = h * 2 * PEER_HALF
        s1 = lax.dot_general(keys_ref[h, 0], q[:, c0:c0 + PEER_HALF], nt, preferred_element_type=F32)
        s2 = lax.dot_general(keys_ref[h, 1], q[:, c0 + PEER_HALF:c0 + 2 * PEER_HALF], nt,
                             preferred_element_type=F32)
        a = _top_values([s1[r:r + SUBLANES] for r in range(0, PEER_KEYS, SUBLANES)], k + 1)
        b = _top_values([s2[r:r + SUBLANES] for r in range(0, PEER_KEYS, SUBLANES)], k + 1)
        neg = jnp.full_like(a[0], -jnp.inf)
        b24 = jnp.concatenate(b + [neg] * 7, axis=0)
        b8 = jnp.concatenate(b[:8], axis=0)
        a16 = jnp.concatenate(a[8:] + [neg] * 7, axis=0)
        row = lax.broadcasted_iota(jnp.int32, b8.shape, 0)
        eb24 = jnp.exp(b24 - b[0])
        eb8 = jnp.exp(b8 - b[0])
        cand = [a[0] + b24[r:r + 8] for r in (0, 8, 16)]
        wgt = [eb24[r:r + 8] for r in (0, 8, 16)]
        for i in range(1, 8):
            cand.append(jnp.where(row < (k + 1) // (i + 1), a[i] + b8, -jnp.inf))
            wgt.append(jnp.exp(a[i] - a[0]) * eb8)
        cand += [a16[r:r + 8] + b[0] for r in (0, 8)]
        wgt += [jnp.exp(a16[r:r + 8] - a[0]) for r in (0, 8)]
        tops = _top_values(cand, k + 1)
        thr = 0.5 * (tops[k - 1] + tops[k])
        z8 = functools.reduce(jnp.add, [jnp.where(c >= thr, w, 0.0) for c, w in zip(cand, wgt)])
        z = jnp.sum(z8, axis=0, keepdims=True)
        shift = a[0] + b[0] + jnp.log(z)
        s1l_ref[I1_PER_TILE:I1_PER_TILE + PEER_KEYS, h, :] = (s1 - shift) * LOG2E - 1.0
        s2l_ref[h] = s2 * LOG2E
        tau_ref[h:h + 1, :] = (thr - shift) * LOG2E - 1.0


def _peer_scores(x, wq_bf, keys_bf, tm):
    n = x.shape[0]
    hk = (PEER_HEADS, PEER_KEYS, tm)
    hk_map = lambda i: (0, 0, i)
    hk_shape = jax.ShapeDtypeStruct((PEER_HEADS, PEER_KEYS, n), F32)
    return pl.pallas_call(
        _peer_score_kernel,
        grid=(n // tm,),
        in_specs=[
            pl.BlockSpec((tm, D_MODEL), lambda i: (i, 0)),
            pl.BlockSpec((D_MODEL, PEER_HEADS * 2 * PEER_HALF), lambda i: (0, 0)),
            pl.BlockSpec((PEER_HEADS, 2, PEER_KEYS, PEER_HALF), lambda i: (0, 0, 0, 0)),
        ],
        out_specs=[
            pl.BlockSpec((D_MODEL, tm), lambda i: (0, i)),
            pl.BlockSpec((PEER_KEYS + 2 * I1_PER_TILE, PEER_HEADS, tm), hk_map), pl.BlockSpec(hk, hk_map),
            pl.BlockSpec((PEER_HEADS, tm), lambda i: (0, i)),
        ],
        out_shape=[
            jax.ShapeDtypeStruct((D_MODEL, n), BF16),
            jax.ShapeDtypeStruct((PEER_KEYS + 2 * I1_PER_TILE, PEER_HEADS, n), F32), hk_shape,
            jax.ShapeDtypeStruct((PEER_HEADS, n), F32),
        ],
        compiler_params=_cparams("parallel"),
        name="peer_scores",
    )(x, wq_bf, keys_bf)


def _gelu_times_two(x):
    return x * (1.0 + lax.erf(x * (2.0 ** -0.5)))


N_PAIRS = PEER_EXPERTS // (2 * PEER_TE)


def _dense_pipeline_step(p, u_ref, v_ref, xt_ref, s1l_ref, s2l_ref, tau_ref, o_ref, hid, at):
    tm = xt_ref.shape[1]
    last = pl.num_programs(1) - 1

    def hidden(half):
        rows = slice(half * PEER_TE, (half + 1) * PEER_TE)
        hid[half][...] = jnp.dot(u_ref[rows, :].astype(BF16), xt_ref[...], preferred_element_type=F32)

    def gated(half):
        src, dst = hid[1 - half], at[1 - half]
        for il in range(I1_PER_TILE):
            r = half * I1_PER_TILE + il
            rs = slice(il * PEER_KEYS, (il + 1) * PEER_KEYS)
            for c in range(tm // LANES):
                cs = slice(c * LANES, (c + 1) * LANES)
                gate = jnp.zeros((PEER_KEYS, LANES), F32)
                for h in range(PEER_HEADS):
                    pair = s2l_ref[h, :, cs] + s1l_ref[r, h:h + 1, cs]
                    gate = gate + jnp.where(pair >= tau_ref[h:h + 1, cs], jnp.exp2(pair), 0.0)
                dst[rs, cs] = (_gelu_times_two(src[rs, cs]) * gate).astype(BF16)

    def value(half):
        rows = slice(half * PEER_TE, (half + 1) * PEER_TE)
        o_ref[...] += lax.dot_general(at[half][...], v_ref[rows, :].astype(BF16), (((0,), (0,)), ((), ())),
                                      preferred_element_type=F32)

    @pl.when(p == 0)
    def _():
        o_ref[...] = jnp.zeros_like(o_ref)
        hidden(0)
        hidden(1)
        gated(1)

    @pl.when((p > 0) & (p < last))
    def _():
        for half in range(2):
            hidden(half)
            gated(half)
            value(half)

    @pl.when(p == last)
    def _():
        gated(0)
        value(0)
        value(1)


TABLE_BUFFERS = 3


def _peer_dense_kernel(u_hbm, v_hbm, xt_ref, s1l_ref, s2l_ref, tau_ref, xts_ref, s1ls_ref, s2ls_ref, taus_ref,
                       o_ref, os_ref, ubuf, vbuf, usem, vsem, hid0, hid1, at0, at1, hids0, hids1, ats0, ats1,
                       *, layer):
    t = pl.program_id(0)
    p = pl.program_id(1)
    n_p = pl.num_programs(1)
    total = pl.num_programs(0) * n_p
    g = t * n_p + p
    ahead = TABLE_BUFFERS - 1

    def table_copies(step):
        pp = step % n_p
        slot = step % TABLE_BUFFERS
        u_rows = pl.ds(jnp.minimum(pp, N_PAIRS - 1) * (2 * PEER_TE), 2 * PEER_TE)
        v_rows = pl.ds(jnp.maximum(pp - 1, 0) * (2 * PEER_TE), 2 * PEER_TE)
        return (pltpu.make_async_copy(u_hbm.at[layer, u_rows, :], ubuf.at[slot], usem.at[slot]),
                pltpu.make_async_copy(v_hbm.at[layer, v_rows, :], vbuf.at[slot], vsem.at[slot]))

    @pl.when(g == 0)
    def _():
        for step in range(ahead):
            for cp in table_copies(step):
                cp.start()

    @pl.when(g + ahead < total)
    def _():
        for cp in table_copies(g + ahead):
            cp.start()

    for cp in table_copies(g):
        cp.wait()
    slot = g % TABLE_BUFFERS
    u_ref, v_ref = ubuf.at[slot], vbuf.at[slot]
    _dense_pipeline_step(p, u_ref, v_ref, xt_ref, s1l_ref, s2l_ref, tau_ref, o_ref, (hid0, hid1), (at0, at1))

    @pl.when(t == 0)
    def _():
        _dense_pipeline_step(p, u_ref, v_ref, xts_ref, s1ls_ref, s2ls_ref, taus_ref, os_ref,
                             (hids0, hids1), (ats0, ats1))


def _peer_dense(u_all, v_all, layer, prompt, sample, tm):
    n = prompt[0].shape[1]
    n_s = sample[0].shape[1]
    table = pl.BlockSpec(memory_space=pl.ANY)

    def token_specs(width, tile_of):
        return [
            pl.BlockSpec((D_MODEL, width), lambda t, p: (0, tile_of(t)), pipeline_mode=pl.Buffered(1)),
            pl.BlockSpec((2 * I1_PER_TILE, PEER_HEADS, width), lambda t, p: (p, 0, tile_of(t))),
            pl.BlockSpec((PEER_HEADS, PEER_KEYS, width), lambda t, p: (0, 0, tile_of(t)),
                         pipeline_mode=pl.Buffered(1)),
            pl.BlockSpec((PEER_HEADS, width), lambda t, p: (0, tile_of(t))),
        ]

    def buffers(width):
        return [pltpu.VMEM((PEER_TE, width), F32), pltpu.VMEM((PEER_TE, width), F32),
                pltpu.VMEM((PEER_TE, width), BF16), pltpu.VMEM((PEER_TE, width), BF16)]

    ring = [pltpu.VMEM((TABLE_BUFFERS, 2 * PEER_TE, D_MODEL), F32)] * 2 + \
           [pltpu.SemaphoreType.DMA((TABLE_BUFFERS,))] * 2
    return pl.pallas_call(
        functools.partial(_peer_dense_kernel, layer=layer),
        grid=(n // tm, N_PAIRS + 1),
        in_specs=[table, table] + token_specs(tm, lambda t: t) + token_specs(n_s, lambda t: 0),
        out_specs=[pl.BlockSpec((tm, D_MODEL), lambda t, p: (t, 0)),
                   pl.BlockSpec((n_s, D_MODEL), lambda t, p: (0, 0))],
        out_shape=[jax.ShapeDtypeStruct((n, D_MODEL), F32), jax.ShapeDtypeStruct((n_s, D_MODEL), F32)],
        scratch_shapes=ring + buffers(tm) + buffers(n_s),
        compiler_params=_cparams("arbitrary", "arbitrary"),
        name="peer_dense",
    )(u_all, v_all, *prompt, *sample)


def _res_ln_kernel(x_ref, f_ref, g_ref, beta_ref, o_ref):
    o_ref[...] = _layer_norm(DEEPNORM_ALPHA * x_ref[...] + f_ref[...], g_ref[...], beta_ref[...])


def _res_ln(x, f, g, beta, tm):
    n = x.shape[0]
    tile = pl.BlockSpec((tm, D_MODEL), lambda i: (i, 0))
    vec = pl.BlockSpec((1, D_MODEL), lambda i: (0, 0))
    return pl.pallas_call(
        _res_ln_kernel,
        grid=(n // tm,),
        in_specs=[tile, tile, vec, vec],
        out_specs=tile,
        out_shape=jax.ShapeDtypeStruct((n, D_MODEL), F32),
        compiler_params=_cparams("parallel"),
        name="res_ln",
    )(x, f, g.reshape(1, D_MODEL), beta.reshape(1, D_MODEL))


def _peer_layer(xp, xs, wq_bf, keys_bf, u_all, v_all, layer, g, beta):
    n_p, n_s = xp.shape[0], xs.shape[0]
    prompt = _peer_scores(xp, wq_bf, keys_bf, _rows(n_p, SCORE_ROWS))
    sample = _peer_scores(xs, wq_bf, keys_bf, _rows(n_s, SCORE_ROWS))
    fp, fs = _peer_dense(u_all, v_all, layer, prompt, sample, _rows(n_p, DENSE_ROWS))
    return (_res_ln(xp, fp, g, beta, _rows(n_p, RES_LN_ROWS)),
            _res_ln(xs, fs, g, beta, _rows(n_s, RES_LN_ROWS)))


def _rope_tables(pos):
    half = HEAD_DIM // 2
    inv = jnp.power(ROPE_THETA, -jnp.arange(half, dtype=F32) * 2.0 / HEAD_DIM)
    ang = pos.astype(F32)[:, None] * inv[None, :]
    cos = jnp.cos(ang)
    sin = jnp.sin(ang)
    reps = LANES // HEAD_DIM
    return (jnp.tile(jnp.concatenate([cos, cos], axis=1), (1, reps)),
            jnp.tile(jnp.concatenate([-sin, sin], axis=1), (1, reps)))


def _shifted(state, u3):
    t = u3.shape[1]
    up = jnp.concatenate([state, u3], axis=1)
    return up[:, 1:t + 1], up[:, 0:t], up[:, -(CONV_WIDTH - 1):]


def kernel(x_prompt, x_sample, cache_k, cache_v, state_conv, w_qkv, b_qkv, w_o, b_o, attn_sinks,
           w_conv_in, conv_w, w_conv_out, w_peer_q, peer_sub_keys, peer_u, peer_v, ln_g, ln_b):
    batch, seq, _ = x_prompt.shape
    dec_batch, dec_seq, _ = x_sample.shape
    n_p = batch * seq
    n_s = dec_batch * dec_seq
    xp = x_prompt.reshape(n_p, D_MODEL)
    xs = x_sample.reshape(n_s, D_MODEL)
    cos_p, sin_p = _rope_tables(jnp.tile(jnp.arange(seq), batch))
    cos_s, sin_s = _rope_tables(jnp.tile(PAST_LEN + jnp.arange(dec_seq), dec_batch))

    new_kp, new_vp, new_cp, new_ks, new_vs, new_cs = [], [], [], [], [], []
    for layer in range(DEPTH):
        j = layer // 2
        g0, b0 = ln_g[layer, 0], ln_b[layer, 0]
        if layer % 2 == 0:
            wqkv = w_qkv[j].astype(BF16)
            wo = w_o[j].astype(BF16)
            sinks = attn_sinks[j].reshape(N_Q_HEADS)
            qkv_p = _qkv_rope(xp, wqkv, b_qkv[j], cos_p, sin_p, _rows(n_p, QKV_ROWS))
            qkv_s = _qkv_rope(xs, wqkv, b_qkv[j], cos_s, sin_s, _rows(n_s, QKV_ROWS))
            o_p = _attn_prompt(qkv_p, sinks, batch, seq)
            qkv_s3 = qkv_s.reshape(dec_batch, dec_seq, QKV_COLS)
            qkv_s8 = jnp.pad(qkv_s3, ((0, 0), (0, SAMPLE_ROWS - dec_seq), (0, 0)))
            ck = cache_k[j].reshape(dec_batch, WINDOW, KV_COLS)
            cv = cache_v[j].reshape(dec_batch, WINDOW, KV_COLS)
            o_s = _attn_sample(qkv_s8, ck, cv, sinks, dec_seq)[:, :dec_seq].reshape(n_s, Q_COLS)
            kv_p = qkv_p.reshape(batch, seq, QKV_COLS)[:, -WINDOW:, Q_COLS:]
            new_kp.append(kv_p[..., :KV_COLS].reshape(batch, WINDOW, N_KV_HEADS, HEAD_DIM))
            new_vp.append(kv_p[..., KV_COLS:].reshape(batch, WINDOW, N_KV_HEADS, HEAD_DIM))
            k_s = jnp.concatenate([ck, qkv_s3[..., Q_COLS:Q_COLS + KV_COLS]], axis=1)[:, -WINDOW:]
            v_s = jnp.concatenate([cv, qkv_s3[..., Q_COLS + KV_COLS:]], axis=1)[:, -WINDOW:]
            new_ks.append(k_s.reshape(dec_batch, WINDOW, N_KV_HEADS, HEAD_DIM))
            new_vs.append(v_s.reshape(dec_batch, WINDOW, N_KV_HEADS, HEAD_DIM))
            xp = _proj_ln(o_p, wo, b_o[j], xp, g0, b0, _rows(n_p, PROJ_ROWS))
            xs = _proj_ln(o_s, wo, b_o[j], xs, g0, b0, _rows(n_s, PROJ_ROWS))
        else:
            w_in = w_conv_in[j].astype(BF16)
            w_out = w_conv_out[j].astype(BF16)
            gb_p, u_p = _conv_in(xp, w_in, _rows(n_p, CONV_IN_ROWS))
            gb_s, u_s = _conv_in(xs, w_in, _rows(n_s, CONV_IN_ROWS))
            um1_s, um2_s, cs = _shifted(state_conv[j], u_s.reshape(dec_batch, dec_seq, D_MODEL))
            new_cp.append(u_p.reshape(batch, seq, D_MODEL)[:, -(CONV_WIDTH - 1):])
            new_cs.append(cs)
            xp = _conv_out_ln(gb_p, u_p, None, conv_w[j], w_out, xp, g0, b0, _rows(n_p, CONV_OUT_ROWS), seq=seq)
            xs = _conv_out_ln(gb_s, u_s, (um1_s.reshape(n_s, D_MODEL), um2_s.reshape(n_s, D_MODEL)),
                              conv_w[j], w_out, xs, g0, b0, _rows(n_s, CONV_OUT_ROWS))
        wq = w_peer_q[layer].astype(BF16)
        keys = peer_sub_keys[layer].astype(BF16)
        g1, b1 = ln_g[layer, 1], ln_b[layer, 1]
        xp, xs = _peer_layer(xp, xs, wq, keys, peer_u, peer_v, layer, g1, b1)

    return (xp.reshape(batch, seq, D_MODEL), xs.reshape(dec_batch, dec_seq, D_MODEL),
            jnp.stack(new_kp), jnp.stack(new_vp), jnp.stack(new_cp),
            jnp.stack(new_ks), jnp.stack(new_vs), jnp.stack(new_cs))
```
